```python
import jax, jax.numpy as jnp
from jax import lax
import numpy as np

D_MODEL = 1024
BATCH = 8
SEQ = 4096
DEPTH = 4

MEM_LEN = 256
BLK = 128
ROPE_THETA = 10000.0
EPS = 1e-6
NEG_INF = -1e30

A_HEADS = 8
A_KV_HEADS = 2
A_HEAD_DIM = 64
A_WINDOW = 128

B_HEADS = 8
B_Q_RANK = 256
B_KV_RANK = 128
B_NOPE = 64
B_ROPE = 32
B_V = 64

M_HEADS = 4
M_HEAD_DIM = 128

D_FF = 3584
N_EXPERTS = 8
TOP_K = 2

N_BRANCHES = 3
BRANCH_WIDTH = 512

IN_SIZES = (A_HEADS * A_HEAD_DIM,
            A_KV_HEADS * A_HEAD_DIM,
            A_KV_HEADS * A_HEAD_DIM,
            B_Q_RANK,
            B_KV_RANK,
            B_ROPE,
            M_HEADS * M_HEAD_DIM,
            N_BRANCHES * D_MODEL)
IN_WIDTH = 512 + 128 + 128 + 256 + 128 + 32 + 512 + 3 * 1024

kernel_name = "hybrid_gated_swa_mla_mem_moe_encoder"


def rms_norm(x, g):
    xf = x.astype(jnp.float32)
    y = xf * lax.rsqrt(jnp.mean(xf * xf, axis=-1, keepdims=True) + EPS)
    return (y * g.astype(jnp.float32)).astype(x.dtype)


def rotary(x, pos):
    d = x.shape[-1]
    half = d // 2
    inv_freq = jnp.power(jnp.float32(ROPE_THETA), -jnp.arange(half, dtype=jnp.float32) * (2.0 / d))
    ang = pos.astype(jnp.float32)[:, :, None, None] * inv_freq
    cos, sin = jnp.cos(ang), jnp.sin(ang)
    xf = x.astype(jnp.float32)
    x1, x2 = xf[..., :half], xf[..., half:]
    return jnp.concatenate([x1 * cos - x2 * sin, x2 * cos + x1 * sin], axis=-1).astype(x.dtype)


def window_gqa(q, k, v, sink):
    b, s_len = q.shape[0], q.shape[1]
    nb = s_len // BLK
    grp = A_HEADS // A_KV_HEADS
    qb = q.reshape(b, nb, BLK, A_KV_HEADS, grp, A_HEAD_DIM)

    def band(t):
        tp = jnp.pad(t, ((0, 0), (BLK, BLK), (0, 0), (0, 0))).reshape(b, nb + 2, BLK, A_KV_HEADS, A_HEAD_DIM)
        return jnp.concatenate([tp[:, :-2], tp[:, 1:-1], tp[:, 2:]], axis=2)

    kw, vw = band(k), band(v)
    rel = jnp.arange(3 * BLK)[None, :] - BLK - jnp.arange(BLK)[:, None]
    k_abs = (jnp.arange(nb)[:, None] - 1) * BLK + jnp.arange(3 * BLK)[None, :]
    mask = (jnp.abs(rel) <= A_WINDOW)[None] & ((k_abs >= 0) & (k_abs < s_len))[:, None, :]
    sc = jnp.einsum('bnqkgd,bnjkd->bkgnqj', qb, kw, preferred_element_type=jnp.float32) * (A_HEAD_DIM ** -0.5)
    sc = jnp.where(mask, sc, NEG_INF)
    sk = sink.astype(jnp.float32).reshape(1, A_KV_HEADS, grp, 1, 1, 1)
    m = jnp.maximum(sc.max(axis=-1, keepdims=True), sk)
    e = jnp.exp(sc - m)
    p = e / (e.sum(axis=-1, keepdims=True) + jnp.exp(sk - m))
    o = jnp.einsum('bkgnqj,bnjkd->bnqkgd', p.astype(v.dtype), vw)
    return o.reshape(b, s_len, A_HEADS * A_HEAD_DIM)


def latent_attention(c_q, c_kv, k_rope_raw, pos, g_q, w_uq, g_kv, w_ukv):
    b, s_len = c_q.shape[0], c_q.shape[1]
    q = (rms_norm(c_q, g_q) @ w_uq).reshape(b, s_len, B_HEADS, B_NOPE + B_ROPE)
    q_nope, q_rope = q[..., :B_NOPE], rotary(q[..., B_NOPE:], pos)
    kv = (rms_norm(c_kv, g_kv) @ w_ukv).reshape(b, s_len, B_HEADS, B_NOPE + B_V)
    k_nope, v = kv[..., :B_NOPE], kv[..., B_NOPE:]
    k_rope = rotary(k_rope_raw[:, :, None, :], pos)[:, :, 0]
    scale = (B_NOPE + B_ROPE) ** -0.5
    nb = s_len // BLK

    def to_blocks(t):
        return jnp.moveaxis(t.reshape(b, nb, BLK, *t.shape[2:]), 1, 0)

    def attend(blk):
        qn, qr = blk
        sc = (jnp.einsum('bqhd,bkhd->bhqk', qn, k_nope, preferred_element_type=jnp.float32)
              + jnp.einsum('bqhd,bkd->bhqk', qr, k_rope, preferred_element_type=jnp.float32)) * scale
        p = jax.nn.softmax(sc, axis=-1)
        return jnp.einsum('bhqk,bkhd->bqhd', p.astype(v.dtype), v)

    o = lax.map(attend, (to_blocks(q_nope), to_blocks(q_rope)))
    return jnp.moveaxis(o, 0, 1).reshape(b, s_len, B_HEADS * B_V)


def memory_attention(q_flat, mem_n, w_kv):
    b, s_len = q_flat.shape[0], q_flat.shape[1]
    n_mem = mem_n.shape[1]
    kv = (mem_n @ w_kv).reshape(b, n_mem, 2, M_HEADS, M_HEAD_DIM)
    k, v = kv[:, :, 0], kv[:, :, 1]
    q = q_flat.reshape(b, s_len, M_HEADS, M_HEAD_DIM)
    sc = jnp.einsum('bqhd,bkhd->bhqk', q, k, preferred_element_type=jnp.float32) * (M_HEAD_DIM ** -0.5)
    p = jax.nn.softmax(sc, axis=-1)
    o = jnp.einsum('bhqk,bkhd->bqhd', p.astype(v.dtype), v)
    return o.reshape(b, s_len, M_HEADS * M_HEAD_DIM)


def swiglu(h, w_gate, w_up, w_down):
    return (jax.nn.silu(h @ w_gate) * (h @ w_up)) @ w_down


def moe_swiglu(h, w_router, w_gate, w_up, w_down):
    b, s_len, d = h.shape
    n_tok = b * s_len
    n_asg = n_tok * TOP_K
    ht = h.reshape(n_tok, d)
    logits = jnp.matmul(ht, w_router, preferred_element_type=jnp.float32)
    top_v, top_i = lax.top_k(logits, TOP_K)
    top_w = jax.nn.softmax(top_v, axis=-1)
    flat_e = top_i.reshape(n_asg)
    flat_tok = jnp.repeat(jnp.arange(n_tok, dtype=jnp.int32), TOP_K)
    flat_w = top_w.reshape(n_asg)
    order = jnp.argsort(flat_e)
    sorted_e = flat_e[order]
    counts = jnp.bincount(flat_e, length=N_EXPERTS)
    padded = (counts + BLK - 1) // BLK * BLK
    start = jnp.cumsum(counts) - counts
    pad_end = jnp.cumsum(padded)
    pad_start = pad_end - padded
    dest = pad_start[sorted_e] + jnp.arange(n_asg) - start[sorted_e]
    n_slots = n_asg + N_EXPERTS * BLK
    n_blk = n_slots // BLK
    tok_buf = jnp.full((n_slots,), n_tok, jnp.int32).at[dest].set(flat_tok[order])
    w_buf = jnp.zeros((n_slots,), jnp.float32).at[dest].set(flat_w[order])
    blk_e = jnp.minimum(jnp.searchsorted(pad_end, jnp.arange(n_blk) * BLK, side='right'), N_EXPERTS - 1)
    h_pad = jnp.concatenate([ht, jnp.zeros((1, d), ht.dtype)], axis=0)

    def expert_block(args):
        tok, wt, e = args
        xb = h_pad[tok]
        y = (jax.nn.silu(xb @ w_gate[e]) * (xb @ w_up[e])) @ w_down[e]
        return y * wt[:, None].astype(h.dtype)

    y = lax.map(expert_block, (tok_buf.reshape(n_blk, BLK), w_buf.reshape(n_blk, BLK), blk_e))
    out = jnp.zeros((n_tok + 1, d), h.dtype).at[tok_buf].add(y.reshape(n_slots, d))
    return out[:n_tok].reshape(b, s_len, d)


def setup_inputs(seed: int = 0) -> dict:
    key = jax.random.key(seed)
    ks = jax.random.split(key, 24)
    n_dense = (DEPTH + 1) // 2
    n_moe = DEPTH // 2

    def w(k, shape, fan_in):
        return jax.random.normal(k, shape, jnp.float32) * (fan_in ** -0.5)

    def gain(k, shape):
        return 1.0 + 0.02 * jax.random.normal(k, shape, jnp.float32)

    x = jax.random.normal(ks[0], (BATCH, SEQ, D_MODEL), jnp.float32)
    mem = jax.random.normal(ks[1], (BATCH, MEM_LEN, D_MODEL), jnp.float32)
    positions = (jax.random.randint(ks[2], (BATCH, 1), 0, SEQ, dtype=jnp.int32)
                 + jnp.arange(SEQ, dtype=jnp.int32)[None, :])
    return {
        "x": x,
        "mem": mem,
        "positions": positions,
        "g_mix": gain(ks[3], (DEPTH, D_MODEL)),
        "w_in": w(ks[4], (DEPTH, D_MODEL, IN_WIDTH), D_MODEL),
        "g_q_lat": gain(ks[5], (DEPTH, B_Q_RANK)),
        "w_uq": w(ks[6], (DEPTH, B_Q_RANK, B_HEADS * (B_NOPE + B_ROPE)), B_Q_RANK),
        "g_kv_lat": gain(ks[7], (DEPTH, B_KV_RANK)),
        "w_ukv": w(ks[8], (DEPTH, B_KV_RANK, B_HEADS * (B_NOPE + B_V)), B_KV_RANK),
        "sink": 0.5 * jax.random.normal(ks[9], (DEPTH, A_HEADS), jnp.float32),
        "w_mem_kv": w(ks[10], (DEPTH, D_MODEL, 2 * M_HEADS * M_HEAD_DIM), D_MODEL),
        "w_branch": w(ks[11], (DEPTH, N_BRANCHES, BRANCH_WIDTH, D_MODEL), BRANCH_WIDTH),
        "w_out": w(ks[12], (DEPTH, D_MODEL, D_MODEL), D_MODEL),
        "g_ffn": gain(ks[13], (DEPTH, D_MODEL)),
        "w_gate_dense": w(ks[14], (n_dense, D_MODEL, D_FF), D_MODEL),
        "w_up_dense": w(ks[15], (n_dense, D_MODEL, D_FF), D_MODEL),
        "w_down_dense": w(ks[16], (n_dense, D_FF, D_MODEL), D_FF),
        "w_router": w(ks[17], (n_moe, D_MODEL, N_EXPERTS), D_MODEL),
        "w_gate_exp": w(ks[18], (n_moe, N_EXPERTS, D_MODEL, D_FF), D_MODEL),
        "w_up_exp": w(ks[19], (n_moe, N_EXPERTS, D_MODEL, D_FF), D_MODEL),
        "w_down_exp": w(ks[20], (n_moe, N_EXPERTS, D_FF, D_MODEL), D_FF),
        "g_mem": gain(ks[21], (D_MODEL,)),
        "g_final": gain(ks[22], (D_MODEL,)),
    }


def reference(x, mem, positions, g_mix, w_in, g_q_lat, w_uq, g_kv_lat, w_ukv, sink, w_mem_kv,
              w_branch, w_out, g_ffn, w_gate_dense, w_up_dense, w_down_dense, w_router,
              w_gate_exp, w_up_exp, w_down_exp, g_mem, g_final):
    b, s_len, _ = x.shape
    mem_n = rms_norm(mem, g_mem)
    split_points = [int(p) for p in np.cumsum(IN_SIZES)[:-1]]
    for l in range(DEPTH):
        h = rms_norm(x, g_mix[l])
        qa, ka, va, cq, ckv, kr, qm, gates = jnp.split(h @ w_in[l], split_points, axis=-1)
        y_a = window_gqa(rotary(qa.reshape(b, s_len, A_HEADS, A_HEAD_DIM), positions),
                         rotary(ka.reshape(b, s_len, A_KV_HEADS, A_HEAD_DIM), positions),
                         va.reshape(b, s_len, A_KV_HEADS, A_HEAD_DIM), sink[l])
        y_b = latent_attention(cq, ckv, kr, positions, g_q_lat[l], w_uq[l], g_kv_lat[l], w_ukv[l])
        y_m = memory_attention(qm, mem_n, w_mem_kv[l])
        g = jax.nn.sigmoid(gates.astype(jnp.float32)).astype(x.dtype).reshape(b, s_len, N_BRANCHES, D_MODEL)
        mixed = (g[:, :, 0] * (y_a @ w_branch[l, 0])
                 + g[:, :, 1] * (y_b @ w_branch[l, 1])
                 + g[:, :, 2] * (y_m @ w_branch[l, 2]))
        x = x + mixed @ w_out[l]
        h = rms_norm(x, g_ffn[l])
        if l % 2 == 0:
            i = l // 2
            x = x + swiglu(h, w_gate_dense[i], w_up_dense[i], w_down_dense[i])
        else:
            i = l // 2
            x = x + moe_swiglu(h, w_router[i], w_gate_exp[i], w_up_exp[i], w_down_exp[i])
    return rms_norm(x, g_final)
```

```python
import functools

import jax
import jax.numpy as jnp
from jax import lax
from jax.experimental import pallas as pl
from jax.experimental.pallas import tpu as pltpu

F32 = jnp.float32
BF16 = jnp.bfloat16

ROPE_THETA = 10000.0
EPS = 1e-6
NEG_INF = -1e30

A_HEADS, A_KV_HEADS, A_HEAD_DIM, A_WINDOW = 8, 2, 64, 128
B_HEADS, B_Q_RANK, B_KV_RANK, B_NOPE, B_ROPE, B_V = 8, 256, 128, 64, 32, 64
M_HEADS, M_HEAD_DIM = 4, 128
N_EXPERTS, TOP_K = 8, 2
LANES = 128
BLK = 128

P_GATES = 0
P_QA = 3072
P_QM = 3584
P_KA = 4096
P_VA = 4352
P_CQ = 4608
P_WIDTH = 4864

TM_PROJ = 512
TM_PREP = 512
TM_MIX = 512
TM_MEM = 512
TM_FFN = 1024
TF_FFN = 512
TQ_MLA = 512
TK_MLA = 512
TM_ROUTE = 1024
TM_EXP = 512
TM_COMB = 256


def _rms(x, g):
    return x * lax.rsqrt(jnp.mean(x * x, axis=-1, keepdims=True) + EPS) * g


def _rot_tile(x, cos, sin_signed, first, shift):
    up = pltpu.roll(x, LANES - shift, 1)
    dn = pltpu.roll(x, shift, 1)
    return x * cos + jnp.where(first, up, dn) * sin_signed


def _rope_table_kernel(pos_ref, par_ref, ca_ref, sa_ref, cb_ref, sb_ref):
    pos = pos_ref[...]
    ang_a = pos * par_ref[0:1, :]
    ca_ref[...] = jnp.cos(ang_a)
    sa_ref[...] = jnp.sin(ang_a) * par_ref[1:2, :]
    ang_b = pos * par_ref[2:3, :]
    cb_ref[...] = jnp.cos(ang_b)
    sb_ref[...] = jnp.sin(ang_b) * par_ref[3:4, :]


def _rope_tables(positions):
    t = positions.size
    tm = 2048
    lane = jnp.arange(LANES)
    half_a = A_HEAD_DIM // 2
    inv_a = jnp.power(F32(ROPE_THETA), -jnp.arange(half_a, dtype=F32) * (2.0 / A_HEAD_DIM))
    inv_a = inv_a[lane % half_a]
    sgn_a = jnp.where((lane % A_HEAD_DIM) < half_a, -1.0, 1.0).astype(F32)
    half_b = B_ROPE // 2
    inv_b16 = jnp.power(F32(ROPE_THETA), -jnp.arange(half_b, dtype=F32) * (2.0 / B_ROPE))
    in_rope = (lane >= B_NOPE) & (lane < B_NOPE + B_ROPE)
    inv_b = jnp.where(in_rope, inv_b16[(lane - B_NOPE) % half_b], 0.0).astype(F32)
    sgn_b = jnp.where(in_rope, jnp.where(lane < B_NOPE + half_b, -1.0, 1.0), 0.0).astype(F32)
    par = jnp.zeros((8, LANES), F32).at[0].set(inv_a).at[1].set(sgn_a).at[2].set(inv_b).at[3].set(sgn_b)
    pos = positions.reshape(t, 1).astype(F32)
    tab = jax.ShapeDtypeStruct((t, LANES), F32)
    return pl.pallas_call(
        _rope_table_kernel,
        grid=(t // tm,),
        in_specs=[pl.BlockSpec((tm, 1), lambda i: (i, 0)),
                  pl.BlockSpec((8, LANES), lambda i: (0, 0))],
        out_specs=[pl.BlockSpec((tm, LANES), lambda i: (i, 0))] * 4,
        out_shape=[tab] * 4,
        name="rope_tables",
    )(pos, par)


def _in_proj_kernel(x_ref, g_ref, w_ref, cos_ref, sin_ref, o_ref):
    h = _rms(x_ref[...], g_ref[...]).astype(BF16)
    cos, sin = cos_ref[...], sin_ref[...]
    lane = lax.broadcasted_iota(jnp.int32, cos.shape, 1)
    first = (lane % A_HEAD_DIM) < (A_HEAD_DIM // 2)

    def rot(acc, n_tiles, scale):
        tiles = []
        for k in range(acc.shape[1] // LANES):
            xk = acc[:, k * LANES:(k + 1) * LANES]
            if k < n_tiles:
                xk = _rot_tile(xk, cos, sin, first, A_HEAD_DIM // 2)
                if scale != 1.0:
                    xk = xk * scale
            tiles.append(xk)
        return jnp.concatenate(tiles, axis=1)

    def proj(c0, c1):
        return jnp.dot(h, w_ref[:, c0:c1], preferred_element_type=F32)

    for c0 in range(P_GATES, P_QA, 512):
        o_ref[:, c0:c0 + 512] = proj(c0, c0 + 512).astype(BF16)
    o_ref[:, P_QA:P_QM] = rot(proj(P_QA, P_QM), 4, A_HEAD_DIM ** -0.5).astype(BF16)
    o_ref[:, P_QM:P_KA] = (proj(P_QM, P_KA) * (M_HEAD_DIM ** -0.5)).astype(BF16)
    o_ref[:, P_KA:P_VA] = rot(proj(P_KA, P_VA), 1, 1.0).astype(BF16)
    o_ref[:, P_VA:P_WIDTH] = proj(P_VA, P_WIDTH).astype(BF16)


def _in_proj(x, g, w, cos_a, sin_a):
    t, d = x.shape
    tm = TM_PROJ
    return pl.pallas_call(
        _in_proj_kernel,
        grid=(t // tm,),
        in_specs=[pl.BlockSpec((tm, d), lambda i: (i, 0)),
                  pl.BlockSpec((1, d), lambda i: (0, 0)),
                  pl.BlockSpec((d, P_WIDTH), lambda i: (0, 0)),
                  pl.BlockSpec((tm, LANES), lambda i: (i, 0)),
                  pl.BlockSpec((tm, LANES), lambda i: (i, 0))],
        out_specs=pl.BlockSpec((tm, P_WIDTH), lambda i: (i, 0)),
        out_shape=jax.ShapeDtypeStruct((t, P_WIDTH), BF16),
        name="in_proj",
    )(x, g, w, cos_a, sin_a)


def _mla_prep_kernel(cq_ref, ckv_ref, kr_ref, gq_ref, gkv_ref, wq_ref, wk_ref, wv_ref,
                     cos_ref, sin_ref, q_ref, k_ref, v_ref):
    cos, sin = cos_ref[...], sin_ref[...]
    lane = lax.broadcasted_iota(jnp.int32, cos.shape, 1)
    half = B_ROPE // 2
    first = (lane >= B_NOPE) & (lane < B_NOPE + half)
    cq = _rms(cq_ref[...].astype(F32), gq_ref[...]).astype(BF16)
    ckv = _rms(ckv_ref[:, LANES:].astype(F32), gkv_ref[...]).astype(BF16)
    kr = pltpu.roll(kr_ref[:, LANES:].astype(F32), B_NOPE, 1)
    q = jnp.dot(cq, wq_ref[...], preferred_element_type=F32)
    k = jnp.dot(ckv, wk_ref[...], preferred_element_type=F32)
    v = jnp.dot(ckv, wv_ref[...], preferred_element_type=F32)
    scale = (B_NOPE + B_ROPE) ** -0.5
    ones_col = (lane == B_V).astype(F32)
    for hd in range(B_HEADS):
        sl = slice(hd * LANES, (hd + 1) * LANES)
        q_ref[:, sl] = (_rot_tile(q[:, sl], cos, sin, first, half) * scale).astype(BF16)
        k_ref[:, sl] = _rot_tile(k[:, sl] + kr, cos, sin, first, half).astype(BF16)
        v_ref[:, sl] = (v[:, sl] + ones_col).astype(BF16)


def _mla_prep(p, gq, gkv, wq, wk, wv, cos_b, sin_b):
    t = p.shape[0]
    tm = TM_PREP
    hw = B_HEADS * LANES
    out = jax.ShapeDtypeStruct((t, hw), BF16)
    const = lambda i: (0, 0)
    return pl.pallas_call(
        _mla_prep_kernel,
        grid=(t // tm,),
        in_specs=[pl.BlockSpec((tm, 256), lambda i: (i, P_CQ // 256)),
                  pl.BlockSpec((tm, 256), lambda i: (i, P_VA // 256)),
                  pl.BlockSpec((tm, 256), lambda i: (i, P_KA // 256)),
                  pl.BlockSpec((1, B_Q_RANK), const),
                  pl.BlockSpec((1, B_KV_RANK), const),
                  pl.BlockSpec((B_Q_RANK, hw), const),
                  pl.BlockSpec((B_KV_RANK, hw), const),
                  pl.BlockSpec((B_KV_RANK, hw), const),
                  pl.BlockSpec((tm, LANES), lambda i: (i, 0)),
                  pl.BlockSpec((tm, LANES), lambda i: (i, 0))],
        out_specs=[pl.BlockSpec((tm, hw), lambda i: (i, 0))] * 3,
        out_shape=[out] * 3,
        name="mla_prep",
    )(p, p, p, gq, gkv, wq, wk, wv, cos_b, sin_b)


def _mla_attn_kernel(q_ref, k_ref, v_ref, o_ref, m_sc, acc_sc, *, tk):
    s_len = k_ref.shape[0]
    outs = []
    for hd in range(2):
        sl = slice(hd * LANES, (hd + 1) * LANES)
        q = q_ref[:, sl]
        m_sc[...] = jnp.full(m_sc.shape, -jnp.inf, F32)
        acc_sc[...] = jnp.zeros(acc_sc.shape, F32)

        def body(j, carry):
            rows = pl.ds(pl.multiple_of(j * tk, tk), tk)
            k = k_ref[rows, sl]
            s = lax.dot_general(q, k, (((1,), (1,)), ((), ())), preferred_element_type=F32)
            m_old = m_sc[...]
            m_new = jnp.maximum(m_old, jnp.max(s, axis=-1, keepdims=True))
            p = jnp.exp(s - m_new).astype(BF16)
            acc_sc[...] = acc_sc[...] * jnp.exp(m_old - m_new) + jnp.dot(
                p, v_ref[rows, sl], preferred_element_type=F32)
            m_sc[...] = m_new
            return carry

        lax.fori_loop(0, s_len // tk, body, 0)
        acc = acc_sc[...]
        outs.append(acc * (1.0 / acc[:, B_V:B_V + 1]))
    lane = lax.broadcasted_iota(jnp.int32, outs[0].shape, 1)
    o_ref[...] = jnp.where(lane < B_V, outs[0], pltpu.roll(outs[1], B_V, 1)).astype(BF16)


def _mla_attn(q, k, v, batch):
    t, hw = q.shape
    s_len = t // batch
    tq = TQ_MLA
    q3, k3, v3 = (a.reshape(batch, s_len, hw) for a in (q, k, v))
    out = pl.pallas_call(
        functools.partial(_mla_attn_kernel, tk=TK_MLA),
        grid=(batch, B_HEADS // 2, s_len // tq),
        in_specs=[pl.BlockSpec((None, tq, 2 * LANES), lambda b, p, i: (b, i, p)),
                  pl.BlockSpec((None, s_len, 2 * LANES), lambda b, p, i: (b, 0, p)),
                  pl.BlockSpec((None, s_len, 2 * LANES), lambda b, p, i: (b, 0, p))],
        out_specs=pl.BlockSpec((None, tq, LANES), lambda b, p, i: (b, i, p)),
        out_shape=jax.ShapeDtypeStruct((batch, s_len, B_HEADS * B_V), BF16),
        scratch_shapes=[pltpu.VMEM((tq, 1), F32), pltpu.VMEM((tq, LANES), F32)],
        name="mla_attn",
    )(q3, k3, v3)
    return out.reshape(t, B_HEADS * B_V)


def _win_attn_kernel(sink_ref, q_ref, kp_ref, kc_ref, kn_ref, vp_ref, vc_ref, vn_ref, o_ref, *, s_len):
    n = pl.program_id(1)
    grp = A_HEADS // A_KV_HEADS
    dh = A_HEAD_DIM
    qi = lax.broadcasted_iota(jnp.int32, (BLK, 3 * BLK), 0)
    kj = lax.broadcasted_iota(jnp.int32, (BLK, 3 * BLK), 1)
    k_abs = (n - 1) * BLK + kj
    ok = (jnp.abs(kj - BLK - qi) <= A_WINDOW) & (k_abs >= 0) & (k_abs < s_len)
    ok = jnp.concatenate([ok] * grp, axis=0)
    row_head = lax.broadcasted_iota(jnp.int32, (grp * BLK, 1), 0) // BLK
    outs = []
    for g in range(A_KV_HEADS):
        ksl = slice(g * dh, (g + 1) * dh)
        kb = jnp.concatenate([kp_ref[:, ksl], kc_ref[:, ksl], kn_ref[:, ksl]], axis=0)
        vb = jnp.concatenate([vp_ref[:, ksl], vc_ref[:, ksl], vn_ref[:, ksl]], axis=0)
        q4 = jnp.concatenate([q_ref[:, (g * grp + a) * dh:(g * grp + a + 1) * dh] for a in range(grp)], axis=0)
        sc = lax.dot_general(q4, kb, (((1,), (1,)), ((), ())), preferred_element_type=F32)
        sc = jnp.where(ok, sc, NEG_INF)
        sk = jnp.zeros((grp * BLK, 1), F32)
        for a in range(grp):
            sk = jnp.where(row_head == a, sink_ref[g * grp + a], sk)
        m = jnp.maximum(jnp.max(sc, axis=-1, keepdims=True), sk)
        e = jnp.exp(sc - m)
        den = jnp.sum(e, axis=-1, keepdims=True) + jnp.exp(sk - m)
        pr = (e * (1.0 / den)).astype(BF16)
        o4 = jnp.dot(pr, vb, preferred_element_type=F32)
        outs.extend(o4[a * BLK:(a + 1) * BLK] for a in range(grp))
    o_ref[...] = jnp.concatenate(outs, axis=1).astype(BF16)


def _win_attn(p, sink, batch):
    t = p.shape[0]
    s_len = t // batch
    nb = s_len // BLK
    p3 = p.reshape(batch, s_len, P_WIDTH)
    kcol, vcol = P_KA // 256, P_VA // 256
    prev = lambda b, n, s: (b, jnp.maximum(n - 1, 0), kcol)
    cur = lambda b, n, s: (b, n, kcol)
    nxt = lambda b, n, s: (b, jnp.minimum(n + 1, nb - 1), kcol)
    vprev = lambda b, n, s: (b, jnp.maximum(n - 1, 0), vcol)
    vcur = lambda b, n, s: (b, n, vcol)
    vnxt = lambda b, n, s: (b, jnp.minimum(n + 1, nb - 1), vcol)
    blk = lambda im: pl.BlockSpec((None, BLK, 256), im)
    out = pl.pallas_call(
        functools.partial(_win_attn_kernel, s_len=s_len),
        grid_spec=pltpu.PrefetchScalarGridSpec(
            num_scalar_prefetch=1,
            grid=(batch, nb),
            in_specs=[pl.BlockSpec((None, BLK, 512), lambda b, n, s: (b, n, P_QA // 512)),
                      blk(prev), blk(cur), blk(nxt), blk(vprev), blk(vcur), blk(vnxt)],
            out_specs=pl.BlockSpec((None, BLK, A_HEADS * A_HEAD_DIM), lambda b, n, s: (b, n, 0)),
        ),
        out_shape=jax.ShapeDtypeStruct((batch, s_len, A_HEADS * A_HEAD_DIM), BF16),
        name="win_attn",
    )(sink, p3, p3, p3, p3, p3, p3, p3)
    return out.reshape(t, A_HEADS * A_HEAD_DIM)


def _mem_kv_kernel(mem_ref, g_ref, w_ref, o_ref):
    mem_n = _rms(mem_ref[...], g_ref[...]).astype(BF16)
    o_ref[...] = jnp.dot(mem_n, w_ref[...], preferred_element_type=F32).astype(BF16)


def _mem_kv(mem2, g_mem, w_mem_kv):
    depth, d, width = w_mem_kv.shape
    rows = mem2.shape[0]
    return pl.pallas_call(
        _mem_kv_kernel,
        grid=(depth,),
        in_specs=[pl.BlockSpec((rows, d), lambda l: (0, 0)),
                  pl.BlockSpec((1, d), lambda l: (0, 0)),
                  pl.BlockSpec((None, d, width), lambda l: (l, 0, 0))],
        out_specs=pl.BlockSpec((None, rows, width), lambda l: (l, 0, 0)),
        out_shape=jax.ShapeDtypeStruct((depth, rows, width), BF16),
        name="mem_kv",
    )(mem2, g_mem, w_mem_kv)


def _mem_attn_kernel(q_ref, kv_ref, o_ref):
    dh = M_HEAD_DIM
    for hd in range(M_HEADS):
        q = q_ref[:, hd * dh:(hd + 1) * dh]
        k = kv_ref[:, hd * dh:(hd + 1) * dh]
        v = kv_ref[:, (M_HEADS + hd) * dh:(M_HEADS + hd + 1) * dh]
        sc = lax.dot_general(q, k, (((1,), (1,)), ((), ())), preferred_element_type=F32)
        e = jnp.exp(sc - jnp.max(sc, axis=-1, keepdims=True))
        pr = (e * (1.0 / jnp.sum(e, axis=-1, keepdims=True))).astype(BF16)
        o_ref[:, hd * dh:(hd + 1) * dh] = jnp.dot(pr, v, preferred_element_type=F32).astype(BF16)


def _mem_attn(p, mem_kv_l, batch):
    t = p.shape[0]
    s_len = t // batch
    tm = TM_MEM
    n_mem = mem_kv_l.shape[0] // batch
    width = M_HEADS * M_HEAD_DIM
    p3 = p.reshape(batch, s_len, P_WIDTH)
    kv3 = mem_kv_l.reshape(batch, n_mem, 2 * width)
    out = pl.pallas_call(
        _mem_attn_kernel,
        grid=(batch, s_len // tm),
        in_specs=[pl.BlockSpec((None, tm, width), lambda b, i: (b, i, P_QM // width)),
                  pl.BlockSpec((None, n_mem, 2 * width), lambda b, i: (b, 0, 0))],
        out_specs=pl.BlockSpec((None, tm, width), lambda b, i: (b, i, 0)),
        out_shape=jax.ShapeDtypeStruct((batch, s_len, width), BF16),
        name="mem_attn",
    )(p3, kv3)
    return out.reshape(t, width)


def _mix_kernel(x_ref, ya_ref, yb_ref, ym_ref, g_ref, wb_ref, wo_ref, o_ref):
    d = x_ref.shape[1]
    mixed = None
    for i, y_ref in enumerate((ya_ref, yb_ref, ym_ref)):
        gate = jax.nn.sigmoid(g_ref[:, i * d:(i + 1) * d].astype(F32))
        term = gate * jnp.dot(y_ref[...], wb_ref[i], preferred_element_type=F32)
        mixed = term if mixed is None else mixed + term
    o_ref[...] = x_ref[...] + jnp.dot(mixed.astype(BF16), wo_ref[...], preferred_element_type=F32)


def _mix(x, ya, yb, ym, p, wb, wo):
    t, d = x.shape
    tm = TM_MIX
    bw = ya.shape[1]
    row = lambda i: (i, 0)
    return pl.pallas_call(
        _mix_kernel,
        grid=(t // tm,),
        in_specs=[pl.BlockSpec((tm, d), row),
                  pl.BlockSpec((tm, bw), row), pl.BlockSpec((tm, bw), row), pl.BlockSpec((tm, bw), row),
                  pl.BlockSpec((tm, 3 * d), lambda i: (i, P_GATES)),
                  pl.BlockSpec((3, bw, d), lambda i: (0, 0, 0)),
                  pl.BlockSpec((d, d), lambda i: (0, 0))],
        out_specs=pl.BlockSpec((tm, d), row),
        out_shape=jax.ShapeDtypeStruct((t, d), F32),
        name="mix_out",
    )(x, ya, yb, ym, p, wb, wo)


def _swiglu_step(h, wg, wu, wd):
    gate = jnp.dot(h, wg, preferred_element_type=F32)
    up = jnp.dot(h, wu, preferred_element_type=F32)
    act = (gate * jax.nn.sigmoid(gate) * up).astype(BF16)
    return jnp.dot(act, wd, preferred_element_type=F32)


def _ffn_kernel(x_ref, g_ref, wg_ref, wu_ref, wd_ref, o_ref, h_sc, acc_sc):
    j = pl.program_id(1)

    @pl.when(j == 0)
    def _():
        h_sc[...] = _rms(x_ref[...], g_ref[...]).astype(BF16)
        acc_sc[...] = jnp.zeros(acc_sc.shape, F32)

    acc_sc[...] += _swiglu_step(h_sc[...], wg_ref[...], wu_ref[...], wd_ref[...])

    @pl.when(j == pl.num_programs(1) - 1)
    def _():
        o_ref[...] = x_ref[...] + acc_sc[...]


def _ffn(x, g, wg, wu, wd):
    t, d = x.shape
    f = wg.shape[1]
    tm, tf = TM_FFN, TF_FFN
    return pl.pallas_call(
        _ffn_kernel,
        grid=(t // tm, f // tf),
        in_specs=[pl.BlockSpec((tm, d), lambda i, j: (i, 0)),
                  pl.BlockSpec((1, d), lambda i, j: (0, 0)),
                  pl.BlockSpec((d, tf), lambda i, j: (0, j)),
                  pl.BlockSpec((d, tf), lambda i, j: (0, j)),
                  pl.BlockSpec((tf, d), lambda i, j: (j, 0))],
        out_specs=pl.BlockSpec((tm, d), lambda i, j: (i, 0)),
        out_shape=jax.ShapeDtypeStruct((t, d), F32),
        scratch_shapes=[pltpu.VMEM((tm, d), BF16), pltpu.VMEM((tm, d), F32)],
        name="ffn_dense",
    )(x, g, wg, wu, wd)


def _router_kernel(x_ref, g_ref, wr_ref, idx_ref, wgt_ref):
    h = _rms(x_ref[...], g_ref[...])
    logits = lax.dot_general(wr_ref[...], h, (((1,), (1,)), ((), ())),
                             precision=lax.Precision.HIGHEST, preferred_element_type=F32)
    e_id = lax.broadcasted_iota(jnp.int32, logits.shape, 0)
    m1 = jnp.max(logits, axis=0, keepdims=True)
    i1 = jnp.min(jnp.where(logits == m1, e_id, N_EXPERTS), axis=0, keepdims=True)
    rest = jnp.where(e_id == i1, -jnp.inf, logits)
    m2 = jnp.max(rest, axis=0, keepdims=True)
    i2 = jnp.min(jnp.where(rest == m2, e_id, N_EXPERTS), axis=0, keepdims=True)
    e2 = jnp.exp(m2 - m1)
    w1 = 1.0 / (1.0 + e2)
    idx_ref[...] = jnp.concatenate([i1, i2], axis=0)
    wgt_ref[...] = jnp.concatenate([w1, e2 * w1], axis=0)


def _router(x, g, wr_t):
    t, d = x.shape
    tm = TM_ROUTE
    return pl.pallas_call(
        _router_kernel,
        grid=(t // tm,),
        in_specs=[pl.BlockSpec((tm, d), lambda i: (i, 0)),
                  pl.BlockSpec((1, d), lambda i: (0, 0)),
                  pl.BlockSpec((N_EXPERTS, d), lambda i: (0, 0))],
        out_specs=[pl.BlockSpec((TOP_K, tm), lambda i: (0, i)),
                   pl.BlockSpec((TOP_K, tm), lambda i: (0, i))],
        out_shape=[jax.ShapeDtypeStruct((TOP_K, t), jnp.int32),
                   jax.ShapeDtypeStruct((TOP_K, t), F32)],
        name="moe_router",
    )(x, g, wr_t)


def _slot_tables(top_i, n_tok):
    flat_e = top_i.T.reshape(n_tok * TOP_K)
    onehot = (flat_e[:, None] == jnp.arange(N_EXPERTS, dtype=jnp.int32)[None, :]).astype(jnp.int32)
    csum = jnp.cumsum(onehot, axis=0)
    rank = jnp.take_along_axis(csum, flat_e[:, None], axis=1)[:, 0] - 1
    counts = csum[-1]
    padded = (counts + TM_EXP - 1) // TM_EXP * TM_EXP
    pad_end = jnp.cumsum(padded)
    dest = (pad_end - padded)[flat_e] + rank
    n_slots = n_tok * TOP_K + N_EXPERTS * TM_EXP
    flat_tok = jnp.arange(n_tok * TOP_K, dtype=jnp.int32) // TOP_K
    slot_tok = jnp.zeros((n_slots,), jnp.int32).at[dest].set(flat_tok)
    n_blk = n_slots // TM_EXP
    blk_e = jnp.minimum(jnp.searchsorted(pad_end, jnp.arange(n_blk, dtype=jnp.int32) * TM_EXP, side='right'),
                        N_EXPERTS - 1).astype(jnp.int32)
    return dest.astype(jnp.int32), slot_tok, blk_e


def _row_copy(src_hbm, row, dst, k, sem):
    return pltpu.make_async_copy(src_hbm.at[pl.ds(row, 1), :], dst.at[pl.ds(k, 1), :], sem)


def _moe_ffn_kernel(blk_e_ref, slot_tok_ref, x_hbm, g_ref, wg_ref, wu_ref, wd_ref, y_ref,
                    xg_sc, h_sc, acc_sc, sem):
    i, j = pl.program_id(0), pl.program_id(1)
    rows = xg_sc.shape[0]

    @pl.when(j == 0)
    def _():
        base = i * rows

        def start(k, c):
            _row_copy(x_hbm, slot_tok_ref[base + k], xg_sc, k, sem).start()
            return c

        lax.fori_loop(0, rows, start, 0)

        def wait(k, c):
            _row_copy(x_hbm, 0, xg_sc, k, sem).wait()
            return c

        lax.fori_loop(0, rows, wait, 0)
        h_sc[...] = _rms(xg_sc[...], g_ref[...]).astype(BF16)
        acc_sc[...] = jnp.zeros(acc_sc.shape, F32)

    acc_sc[...] += _swiglu_step(h_sc[...], wg_ref[...], wu_ref[...], wd_ref[...])

    @pl.when(j == pl.num_programs(1) - 1)
    def _():
        y_ref[...] = acc_sc[...]


def _moe_ffn(x, g, wg, wu, wd, blk_e, slot_tok):
    t, d = x.shape
    f = wg.shape[2]
    tm, tf = TM_EXP, TF_FFN
    n_slots = slot_tok.shape[0]
    return pl.pallas_call(
        _moe_ffn_kernel,
        grid_spec=pltpu.PrefetchScalarGridSpec(
            num_scalar_prefetch=2,
            grid=(n_slots // tm, f // tf),
            in_specs=[pl.BlockSpec(memory_space=pl.ANY),
                      pl.BlockSpec((1, d), lambda i, j, be, st: (0, 0)),
                      pl.BlockSpec((None, d, tf), lambda i, j, be, st: (be[i], 0, j)),
                      pl.BlockSpec((None, d, tf), lambda i, j, be, st: (be[i], 0, j)),
                      pl.BlockSpec((None, tf, d), lambda i, j, be, st: (be[i], j, 0))],
            out_specs=pl.BlockSpec((tm, d), lambda i, j, be, st: (i, 0)),
            scratch_shapes=[pltpu.VMEM((tm, d), F32), pltpu.VMEM((tm, d), BF16),
                            pltpu.VMEM((tm, d), F32), pltpu.SemaphoreType.DMA(())],
        ),
        out_shape=jax.ShapeDtypeStruct((n_slots, d), F32),
        name="moe_ffn",
    )(blk_e, slot_tok, x, g, wg, wu, wd)


def _moe_combine_kernel(dest_ref, x_ref, w_ref, y_hbm, gf_ref, o_ref, y0_sc, y1_sc, sem, *, final_norm):
    i = pl.program_id(0)
    rows = x_ref.shape[0]
    base = i * rows * TOP_K

    def start(k, c):
        _row_copy(y_hbm, dest_ref[base + TOP_K * k], y0_sc, k, sem).start()
        _row_copy(y_hbm, dest_ref[base + TOP_K * k + 1], y1_sc, k, sem).start()
        return c

    lax.fori_loop(0, rows, start, 0)

    def wait(k, c):
        _row_copy(y_hbm, 0, y0_sc, k, sem).wait()
        _row_copy(y_hbm, 0, y1_sc, k, sem).wait()
        return c

    lax.fori_loop(0, rows, wait, 0)
    out = x_ref[...] + w_ref[:, 0:1] * y0_sc[...] + w_ref[:, 1:2] * y1_sc[...]
    if final_norm:
        out = _rms(out, gf_ref[...])
    o_ref[...] = out


def _moe_combine(x, top_w, y, dest, g_final, final_norm):
    t, d = x.shape
    tm = TM_COMB
    return pl.pallas_call(
        functools.partial(_moe_combine_kernel, final_norm=final_norm),
        grid_spec=pltpu.PrefetchScalarGridSpec(
            num_scalar_prefetch=1,
            grid=(t // tm,),
            in_specs=[pl.BlockSpec((tm, d), lambda i, ds: (i, 0)),
                      pl.BlockSpec((tm, TOP_K), lambda i, ds: (i, 0)),
                      pl.BlockSpec(memory_space=pl.ANY),
                      pl.BlockSpec((1, d), lambda i, ds: (0, 0))],
            out_specs=pl.BlockSpec((tm, d), lambda i, ds: (i, 0)),
            scratch_shapes=[pltpu.VMEM((tm, d), F32), pltpu.VMEM((tm, d), F32),
                            pltpu.SemaphoreType.DMA(())],
        ),
        out_shape=jax.ShapeDtypeStruct((t, d), F32),
        name="moe_combine",
    )(dest, x, top_w, y, g_final)


def _final_norm_kernel(x_ref, g_ref, o_ref):
    o_ref[...] = _rms(x_ref[...], g_ref[...])


def _final_norm(x, g):
    t, d = x.shape
    tm = 1024
    return pl.pallas_call(
        _final_norm_kernel,
        grid=(t // tm,),
        in_specs=[pl.BlockSpec((tm, d), lambda i: (i, 0)), pl.BlockSpec((1, d), lambda i: (0, 0))],
        out_specs=pl.BlockSpec((tm, d), lambda i: (i, 0)),
        out_shape=jax.ShapeDtypeStruct((t, d), F32),
        name="final_norm",
    )(x, g)


def _relayout_w_in(w_in):
    qa, ka, va, cq, ckv, kr, qm, gates = jnp.split(
        w_in, [512, 640, 768, 1024, 1152, 1184, 1696], axis=-1)
    pad = jnp.zeros(w_in.shape[:2] + (LANES - B_ROPE,), w_in.dtype)
    return jnp.concatenate([gates, qa, qm, ka, kr, pad, va, ckv, cq], axis=-1).astype(BF16)


def _relayout_latent(w_uq, w_ukv):
    depth = w_uq.shape[0]
    q = w_uq.reshape(depth, B_Q_RANK, B_HEADS, B_NOPE + B_ROPE)
    q = jnp.pad(q, ((0, 0), (0, 0), (0, 0), (0, LANES - B_NOPE - B_ROPE)))
    kv = w_ukv.reshape(depth, B_KV_RANK, B_HEADS, B_NOPE + B_V)
    k = jnp.pad(kv[..., :B_NOPE], ((0, 0), (0, 0), (0, 0), (0, LANES - B_NOPE)))
    v = jnp.pad(kv[..., B_NOPE:], ((0, 0), (0, 0), (0, 0), (0, LANES - B_V)))
    flat = lambda a, r: a.reshape(depth, r, B_HEADS * LANES).astype(BF16)
    return flat(q, B_Q_RANK), flat(k, B_KV_RANK), flat(v, B_KV_RANK)


def kernel(x, mem, positions, g_mix, w_in, g_q_lat, w_uq, g_kv_lat, w_ukv, sink, w_mem_kv, w_branch, w_out,
           g_ffn, w_gate_dense, w_up_dense, w_down_dense, w_router, w_gate_exp, w_up_exp, w_down_exp,
           g_mem, g_final):
    batch, s_len, d = x.shape
    depth = w_in.shape[0]
    t = batch * s_len
    row = lambda a: a.reshape(1, -1).astype(F32)

    w_in_p = _relayout_w_in(w_in)
    wq_p, wk_p, wv_p = _relayout_latent(w_uq, w_ukv)
    w_branch_b, w_out_b = w_branch.astype(BF16), w_out.astype(BF16)
    wgd, wud, wdd = w_gate_dense.astype(BF16), w_up_dense.astype(BF16), w_down_dense.astype(BF16)
    wge, wue, wde = w_gate_exp.astype(BF16), w_up_exp.astype(BF16), w_down_exp.astype(BF16)
    wr_t = jnp.swapaxes(w_router, 1, 2).astype(F32)

    cos_a, sin_a, cos_b, sin_b = _rope_tables(positions)
    mem_kv = _mem_kv(mem.reshape(batch * mem.shape[1], d).astype(F32), row(g_mem), w_mem_kv.astype(BF16))

    xt = x.reshape(t, d).astype(F32)
    for l in range(depth):
        p = _in_proj(xt, row(g_mix[l]), w_in_p[l], cos_a, sin_a)
        y_a = _win_attn(p, sink[l].astype(F32), batch)
        q, k, v = _mla_prep(p, row(g_q_lat[l]), row(g_kv_lat[l]), wq_p[l], wk_p[l], wv_p[l], cos_b, sin_b)
        y_b = _mla_attn(q, k, v, batch)
        y_m = _mem_attn(p, mem_kv[l], batch)
        xt = _mix(xt, y_a, y_b, y_m, p, w_branch_b[l], w_out_b[l])
        i = l // 2
        if l % 2 == 0:
            xt = _ffn(xt, row(g_ffn[l]), wgd[i], wud[i], wdd[i])
            if l == depth - 1:
                xt = _final_norm(xt, row(g_final))
        else:
            top_i, top_w = _router(xt, row(g_ffn[l]), wr_t[i])
            dest, slot_tok, blk_e = _slot_tables(top_i, t)
            y = _moe_ffn(xt, row(g_ffn[l]), wge[i], wue[i], wde[i], blk_e, slot_tok)
            xt = _moe_combine(xt, top_w.T, y, dest, row(g_final), final_norm=(l == depth - 1))
    return xt.reshape(batch, s_len, d).astype(x.dtype)
```

```python
import functools

import jax
import jax.numpy as jnp
from jax import lax
from jax.experimental import pallas as pl
from jax.experimental.pallas import tpu as pltpu

F32 = jnp.float32
BF16 = jnp.bfloat16

ROPE_THETA = 10000.0
EPS = 1e-6
NEG_INF = -1e30

A_HEADS, A_KV_HEADS, A_HEAD_DIM, A_WINDOW = 8, 2, 64, 128
B_HEADS, B_Q_RANK, B_KV_RANK, B_NOPE, B_ROPE, B_V = 8, 256, 128, 64, 32, 64
M_HEADS, M_HEAD_DIM = 4, 128
N_EXPERTS, TOP_K = 8, 2
LANES = 128
BLK = 128

P_GATES = 0
P_QA = 3072
P_QM = 3584
P_KA = 4096
P_VA = 4352
P_CQ = 4608
P_WIDTH = 4864

TM_PROJ = 512
TM_PREP = 512
TM_MIX = 512
TM_MEM = 512
TM_FFN = 1024
TF_FFN = 512
TQ_MLA = 512
TK_MLA = 512
TM_ROUTE = 1024
TM_EXP = 512
TM_COMB = 512


def _rms(x, g):
    return x * lax.rsqrt(jnp.mean(x * x, axis=-1, keepdims=True) + EPS) * g


def _rot_tile(x, cos, sin_signed, first, shift):
    up = pltpu.roll(x, LANES - shift, 1)
    dn = pltpu.roll(x, shift, 1)
    return x * cos + jnp.where(first, up, dn) * sin_signed


def _rope_table_kernel(pos_ref, posr_ref, par_ref, invc_ref, ca_ref, sa_ref, cb_ref, sb_ref, cbt_ref, sbt_ref):
    pos = pos_ref[...]
    ang_a = pos * par_ref[0:1, :]
    ca_ref[...] = jnp.cos(ang_a)
    sa_ref[...] = jnp.sin(ang_a) * par_ref[1:2, :]
    ang_b = pos * par_ref[2:3, :]
    cb_ref[...] = jnp.cos(ang_b)
    sb_ref[...] = jnp.sin(ang_b) * par_ref[3:4, :]
    ang_t = invc_ref[...] * posr_ref[...]
    cbt_ref[...] = jnp.cos(ang_t)
    sbt_ref[...] = jnp.sin(ang_t)


def _rope_tables(positions):
    t = positions.size
    tm = 2048
    lane = jnp.arange(LANES)
    half_a = A_HEAD_DIM // 2
    inv_a = jnp.power(F32(ROPE_THETA), -jnp.arange(half_a, dtype=F32) * (2.0 / A_HEAD_DIM))
    inv_a = inv_a[lane % half_a]
    sgn_a = jnp.where((lane % A_HEAD_DIM) < half_a, -1.0, 1.0).astype(F32)
    half_b = B_ROPE // 2
    inv_b16 = jnp.power(F32(ROPE_THETA), -jnp.arange(half_b, dtype=F32) * (2.0 / B_ROPE))
    in_rope = (lane >= B_NOPE) & (lane < B_NOPE + B_ROPE)
    inv_b = jnp.where(in_rope, inv_b16[(lane - B_NOPE) % half_b], 0.0).astype(F32)
    sgn_b = jnp.where(in_rope, jnp.where(lane < B_NOPE + half_b, -1.0, 1.0), 0.0).astype(F32)
    par = jnp.zeros((8, LANES), F32).at[0].set(inv_a).at[1].set(sgn_a).at[2].set(inv_b).at[3].set(sgn_b)
    pos = positions.reshape(t, 1).astype(F32)
    tab = jax.ShapeDtypeStruct((t, LANES), F32)
    tab_t = jax.ShapeDtypeStruct((half_b, t), F32)
    return pl.pallas_call(
        _rope_table_kernel,
        grid=(t // tm,),
        in_specs=[pl.BlockSpec((tm, 1), lambda i: (i, 0)),
                  pl.BlockSpec((1, tm), lambda i: (0, i)),
                  pl.BlockSpec((8, LANES), lambda i: (0, 0)),
                  pl.BlockSpec((half_b, 1), lambda i: (0, 0))],
        out_specs=[pl.BlockSpec((tm, LANES), lambda i: (i, 0))] * 4
        + [pl.BlockSpec((half_b, tm), lambda i: (0, i))] * 2,
        out_shape=[tab] * 4 + [tab_t] * 2,
        name="rope_tables",
    )(pos, pos.reshape(1, t), par, inv_b16.reshape(half_b, 1))


def _in_proj_kernel(x_ref, g_ref, w_ref, cos_ref, sin_ref, o_ref):
    h = _rms(x_ref[...], g_ref[...]).astype(BF16)
    cos, sin = cos_ref[...], sin_ref[...]
    lane = lax.broadcasted_iota(jnp.int32, cos.shape, 1)
    first = (lane % A_HEAD_DIM) < (A_HEAD_DIM // 2)

    def rot(acc, n_tiles, scale):
        tiles = []
        for k in range(acc.shape[1] // LANES):
            xk = acc[:, k * LANES:(k + 1) * LANES]
            if k < n_tiles:
                xk = _rot_tile(xk, cos, sin, first, A_HEAD_DIM // 2)
                if scale != 1.0:
                    xk = xk * scale
            tiles.append(xk)
        return jnp.concatenate(tiles, axis=1)

    def proj(c0, c1):
        return jnp.dot(h, w_ref[:, c0:c1], preferred_element_type=F32)

    for c0 in range(P_GATES, P_QA, 512):
        o_ref[:, c0:c0 + 512] = proj(c0, c0 + 512).astype(BF16)
    o_ref[:, P_QA:P_QM] = rot(proj(P_QA, P_QM), 4, A_HEAD_DIM ** -0.5).astype(BF16)
    o_ref[:, P_QM:P_KA] = (proj(P_QM, P_KA) * (M_HEAD_DIM ** -0.5)).astype(BF16)
    o_ref[:, P_KA:P_VA] = rot(proj(P_KA, P_VA), 1, 1.0).astype(BF16)
    o_ref[:, P_VA:P_WIDTH] = proj(P_VA, P_WIDTH).astype(BF16)


def _in_proj(x, g, w, cos_a, sin_a):
    t, d = x.shape
    tm = TM_PROJ
    return pl.pallas_call(
        _in_proj_kernel,
        grid=(t // tm,),
        in_specs=[pl.BlockSpec((tm, d), lambda i: (i, 0)),
                  pl.BlockSpec((1, d), lambda i: (0, 0)),
                  pl.BlockSpec((d, P_WIDTH), lambda i: (0, 0)),
                  pl.BlockSpec((tm, LANES), lambda i: (i, 0)),
                  pl.BlockSpec((tm, LANES), lambda i: (i, 0))],
        out_specs=pl.BlockSpec((tm, P_WIDTH), lambda i: (i, 0)),
        out_shape=jax.ShapeDtypeStruct((t, P_WIDTH), BF16),
        name="in_proj",
    )(x, g, w, cos_a, sin_a)


def _mla_prep_kernel(cq_ref, ckv_ref, kr_ref, gq_ref, gkv_ref, wqt_ref, wk_ref, wvt_ref,
                     cos_ref, sin_ref, cost_ref, sint_ref, qt_ref, k_ref, vt_ref):
    half = B_ROPE // 2
    nt = (((1,), (1,)), ((), ()))
    cq = _rms(cq_ref[...].astype(F32), gq_ref[...]).astype(BF16)
    ckv = _rms(ckv_ref[:, LANES:].astype(F32), gkv_ref[...]).astype(BF16)

    cos, sin = cos_ref[...], sin_ref[...]
    lane = lax.broadcasted_iota(jnp.int32, cos.shape, 1)
    first = (lane >= B_NOPE) & (lane < B_NOPE + half)
    kr = _rot_tile(pltpu.roll(kr_ref[:, LANES:].astype(F32), B_NOPE, 1), cos, sin, first, half)
    k = jnp.dot(ckv, wk_ref[...], preferred_element_type=F32)
    for hd in range(B_HEADS):
        sl = slice(hd * LANES, (hd + 1) * LANES)
        k_ref[:, sl] = (k[:, sl] + kr).astype(BF16)

    ct, st = cost_ref[...], sint_ref[...]
    qscale = (B_NOPE + B_ROPE) ** -0.5 * 1.4426950408889634
    qt = lax.dot_general(wqt_ref[...], cq, nt, preferred_element_type=F32)
    vt = lax.dot_general(wvt_ref[...], ckv, nt, preferred_element_type=F32)
    row = lax.broadcasted_iota(jnp.int32, (LANES, vt.shape[1]), 0)
    ones_row = (row == B_V).astype(F32)
    for hd in range(B_HEADS):
        r0 = hd * LANES
        x1 = qt[r0 + B_NOPE:r0 + B_NOPE + half]
        x2 = qt[r0 + B_NOPE + half:r0 + B_NOPE + B_ROPE]
        qh = jnp.concatenate([qt[r0:r0 + B_NOPE], x1 * ct - x2 * st, x2 * ct + x1 * st,
                              qt[r0 + B_NOPE + B_ROPE:r0 + LANES]], axis=0)
        qt_ref[r0:r0 + LANES, :] = (qh * qscale).astype(BF16)
        vt_ref[r0:r0 + LANES, :] = (vt[r0:r0 + LANES] + ones_row).astype(BF16)


def _mla_prep(p, gq, gkv, wqt, wk, wvt, cos_b, sin_b, cos_bt, sin_bt, batch):
    t = p.shape[0]
    s_len = t // batch
    tm = TM_PREP
    nt = s_len // tm
    hw = B_HEADS * LANES
    half = B_ROPE // 2
    const = lambda b, i: (0, 0)
    tok = lambda b, i: (b * nt + i, 0)
    return pl.pallas_call(
        _mla_prep_kernel,
        grid=(batch, nt),
        in_specs=[pl.BlockSpec((tm, 256), lambda b, i: (b * nt + i, P_CQ // 256)),
                  pl.BlockSpec((tm, 256), lambda b, i: (b * nt + i, P_VA // 256)),
                  pl.BlockSpec((tm, 256), lambda b, i: (b * nt + i, P_KA // 256)),
                  pl.BlockSpec((1, B_Q_RANK), const),
                  pl.BlockSpec((1, B_KV_RANK), const),
                  pl.BlockSpec((hw, B_Q_RANK), const),
                  pl.BlockSpec((B_KV_RANK, hw), const),
                  pl.BlockSpec((hw, B_KV_RANK), const),
                  pl.BlockSpec((tm, LANES), tok),
                  pl.BlockSpec((tm, LANES), tok),
                  pl.BlockSpec((half, tm), lambda b, i: (0, b * nt + i)),
                  pl.BlockSpec((half, tm), lambda b, i: (0, b * nt + i))],
        out_specs=[pl.BlockSpec((None, hw, tm), lambda b, i: (b, 0, i)),
                   pl.BlockSpec((None, tm, hw), lambda b, i: (b, i, 0)),
                   pl.BlockSpec((None, hw, tm), lambda b, i: (b, 0, i))],
        out_shape=[jax.ShapeDtypeStruct((batch, hw, s_len), BF16),
                   jax.ShapeDtypeStruct((batch, s_len, hw), BF16),
                   jax.ShapeDtypeStruct((batch, hw, s_len), BF16)],
        name="mla_prep",
    )(p, p, p, gq, gkv, wqt, wk, wvt, cos_b, sin_b, cos_bt, sin_bt)


def _mla_attn_kernel(qt_ref, k_ref, vt_ref, o_ref, m_sc, acc_sc, s0_sc, s1_sc, *, tk):
    n_chunks = k_ref.shape[0] // tk
    m_sc[...] = jnp.full(m_sc.shape, -jnp.inf, F32)
    acc_sc[...] = jnp.zeros(acc_sc.shape, F32)
    heads = [slice(hd * LANES, (hd + 1) * LANES) for hd in range(2)]

    def keys(j):
        return pl.ds(pl.multiple_of(j * tk, tk), tk)

    def scores(j, dst):
        for hd, sl in enumerate(heads):
            dst[hd] = jnp.dot(k_ref[keys(j), sl], qt_ref[sl, :], preferred_element_type=F32)

    def accumulate(j, src):
        for hd, sl in enumerate(heads):
            st = src[hd]
            m_old = m_sc[hd]
            m_new = jnp.maximum(m_old, jnp.max(st, axis=0, keepdims=True))
            pt = jnp.exp2(st - m_new).astype(BF16)
            acc_sc[hd] = acc_sc[hd] * jnp.exp2(m_old - m_new) + jnp.dot(
                vt_ref[sl, keys(j)], pt, preferred_element_type=F32)
            m_sc[hd] = m_new

    scores(0, s0_sc)

    def body(jj, carry):
        j = 2 * jj
        scores(j + 1, s1_sc)
        accumulate(j, s0_sc)
        scores(j + 2, s0_sc)
        accumulate(j + 1, s1_sc)
        return carry

    lax.fori_loop(0, n_chunks // 2 - 1, body, 0)
    scores(n_chunks - 1, s1_sc)
    accumulate(n_chunks - 2, s0_sc)
    accumulate(n_chunks - 1, s1_sc)
    outs = []
    for hd in range(2):
        acc = acc_sc[hd]
        outs.append(acc[:B_V] * (1.0 / acc[B_V:B_V + 1]))
    o_ref[...] = jnp.concatenate(outs, axis=0).T.astype(BF16)


def _mla_attn(qt, k, vt):
    batch, s_len, hw = k.shape
    tq = TQ_MLA
    out = pl.pallas_call(
        functools.partial(_mla_attn_kernel, tk=TK_MLA),
        grid=(batch, B_HEADS // 2, s_len // tq),
        in_specs=[pl.BlockSpec((None, 2 * LANES, tq), lambda b, p, i: (b, p, i)),
                  pl.BlockSpec((None, s_len, 2 * LANES), lambda b, p, i: (b, 0, p)),
                  pl.BlockSpec((None, 2 * LANES, s_len), lambda b, p, i: (b, p, 0))],
        out_specs=pl.BlockSpec((None, tq, LANES), lambda b, p, i: (b, i, p)),
        out_shape=jax.ShapeDtypeStruct((batch, s_len, B_HEADS * B_V), BF16),
        scratch_shapes=[pltpu.VMEM((2, 1, tq), F32), pltpu.VMEM((2, LANES, tq), F32),
                        pltpu.VMEM((2, TK_MLA, tq), F32), pltpu.VMEM((2, TK_MLA, tq), F32)],
        name="mla_attn",
    )(qt, k, vt)
    return out.reshape(batch * s_len, B_HEADS * B_V)


def _win_attn_kernel(sink_ref, q_ref, kp_ref, kc_ref, kn_ref, vp_ref, vc_ref, vn_ref, o_ref, *, s_len):
    n = pl.program_id(1)
    grp = A_HEADS // A_KV_HEADS
    dh = A_HEAD_DIM
    qi = lax.broadcasted_iota(jnp.int32, (BLK, 3 * BLK), 0)
    kj = lax.broadcasted_iota(jnp.int32, (BLK, 3 * BLK), 1)
    k_abs = (n - 1) * BLK + kj
    ok = (jnp.abs(kj - BLK - qi) <= A_WINDOW) & (k_abs >= 0) & (k_abs < s_len)
    ok = jnp.concatenate([ok] * grp, axis=0)
    row_head = lax.broadcasted_iota(jnp.int32, (grp * BLK, 1), 0) // BLK
    outs = []
    for g in range(A_KV_HEADS):
        ksl = slice(g * dh, (g + 1) * dh)
        kb = jnp.concatenate([kp_ref[:, ksl], kc_ref[:, ksl], kn_ref[:, ksl]], axis=0)
        vb = jnp.concatenate([vp_ref[:, ksl], vc_ref[:, ksl], vn_ref[:, ksl]], axis=0)
        q4 = jnp.concatenate([q_ref[:, (g * grp + a) * dh:(g * grp + a + 1) * dh] for a in range(grp)], axis=0)
        sc = lax.dot_general(q4, kb, (((1,), (1,)), ((), ())), preferred_element_type=F32)
        sc = jnp.where(ok, sc, NEG_INF)
        sk = jnp.zeros((grp * BLK, 1), F32)
        for a in range(grp):
            sk = jnp.where(row_head == a, sink_ref[g * grp + a], sk)
        m = jnp.maximum(jnp.max(sc, axis=-1, keepdims=True), sk)
        e = jnp.exp(sc - m)
        den = jnp.sum(e, axis=-1, keepdims=True) + jnp.exp(sk - m)
        pr = (e * (1.0 / den)).astype(BF16)
        o4 = jnp.dot(pr, vb, preferred_element_type=F32)
        outs.extend(o4[a * BLK:(a + 1) * BLK] for a in range(grp))
    o_ref[...] = jnp.concatenate(outs, axis=1).astype(BF16)


def _win_attn(p, sink, batch):
    t = p.shape[0]
    s_len = t // batch
    nb = s_len // BLK
    p3 = p.reshape(batch, s_len, P_WIDTH)
    kcol, vcol = P_KA // 256, P_VA // 256
    prev = lambda b, n, s: (b, jnp.maximum(n - 1, 0), kcol)
    cur = lambda b, n, s: (b, n, kcol)
    nxt = lambda b, n, s: (b, jnp.minimum(n + 1, nb - 1), kcol)
    vprev = lambda b, n, s: (b, jnp.maximum(n - 1, 0), vcol)
    vcur = lambda b, n, s: (b, n, vcol)
    vnxt = lambda b, n, s: (b, jnp.minimum(n + 1, nb - 1), vcol)
    blk = lambda im: pl.BlockSpec((None, BLK, 256), im)
    out = pl.pallas_call(
        functools.partial(_win_attn_kernel, s_len=s_len),
        grid_spec=pltpu.PrefetchScalarGridSpec(
            num_scalar_prefetch=1,
            grid=(batch, nb),
            in_specs=[pl.BlockSpec((None, BLK, 512), lambda b, n, s: (b, n, P_QA // 512)),
                      blk(prev), blk(cur), blk(nxt), blk(vprev), blk(vcur), blk(vnxt)],
            out_specs=pl.BlockSpec((None, BLK, A_HEADS * A_HEAD_DIM), lambda b, n, s: (b, n, 0)),
        ),
        out_shape=jax.ShapeDtypeStruct((batch, s_len, A_HEADS * A_HEAD_DIM), BF16),
        name="win_attn",
    )(sink, p3, p3, p3, p3, p3, p3, p3)
    return out.reshape(t, A_HEADS * A_HEAD_DIM)


def _mem_kv_kernel(mem_ref, g_ref, w_ref, o_ref):
    mem_n = _rms(mem_ref[...], g_ref[...]).astype(BF16)
    o_ref[...] = jnp.dot(mem_n, w_ref[...], preferred_element_type=F32).astype(BF16)


def _mem_kv(mem2, g_mem, w_mem_kv):
    depth, d, width = w_mem_kv.shape
    rows = mem2.shape[0]
    return pl.pallas_call(
        _mem_kv_kernel,
        grid=(depth,),
        in_specs=[pl.BlockSpec((rows, d), lambda l: (0, 0)),
                  pl.BlockSpec((1, d), lambda l: (0, 0)),
                  pl.BlockSpec((None, d, width), lambda l: (l, 0, 0))],
        out_specs=pl.BlockSpec((None, rows, width), lambda l: (l, 0, 0)),
        out_shape=jax.ShapeDtypeStruct((depth, rows, width), BF16),
        name="mem_kv",
    )(mem2, g_mem, w_mem_kv)


def _mem_attn_kernel(q_ref, kv_ref, o_ref):
    dh = M_HEAD_DIM
    for hd in range(M_HEADS):
        q = q_ref[:, hd * dh:(hd + 1) * dh]
        k = kv_ref[:, hd * dh:(hd + 1) * dh]
        v = kv_ref[:, (M_HEADS + hd) * dh:(M_HEADS + hd + 1) * dh]
        sc = lax.dot_general(q, k, (((1,), (1,)), ((), ())), preferred_element_type=F32)
        e = jnp.exp(sc - jnp.max(sc, axis=-1, keepdims=True))
        pr = (e * (1.0 / jnp.sum(e, axis=-1, keepdims=True))).astype(BF16)
        o_ref[:, hd * dh:(hd + 1) * dh] = jnp.dot(pr, v, preferred_element_type=F32).astype(BF16)


def _mem_attn(p, mem_kv_l, batch):
    t = p.shape[0]
    s_len = t // batch
    tm = TM_MEM
    n_mem = mem_kv_l.shape[0] // batch
    width = M_HEADS * M_HEAD_DIM
    p3 = p.reshape(batch, s_len, P_WIDTH)
    kv3 = mem_kv_l.reshape(batch, n_mem, 2 * width)
    out = pl.pallas_call(
        _mem_attn_kernel,
        grid=(batch, s_len // tm),
        in_specs=[pl.BlockSpec((None, tm, width), lambda b, i: (b, i, P_QM // width)),
                  pl.BlockSpec((None, n_mem, 2 * width), lambda b, i: (b, 0, 0))],
        out_specs=pl.BlockSpec((None, tm, width), lambda b, i: (b, i, 0)),
        out_shape=jax.ShapeDtypeStruct((batch, s_len, width), BF16),
        name="mem_attn",
    )(p3, kv3)
    return out.reshape(t, width)


def _mix_kernel(x_ref, ya_ref, yb_ref, ym_ref, g_ref, wb_ref, wo_ref, o_ref):
    d = x_ref.shape[1]
    mixed = None
    for i, y_ref in enumerate((ya_ref, yb_ref, ym_ref)):
        gate = jax.nn.sigmoid(g_ref[:, i * d:(i + 1) * d].astype(F32))
        term = gate * jnp.dot(y_ref[...], wb_ref[i], preferred_element_type=F32)
        mixed = term if mixed is None else mixed + term
    o_ref[...] = x_ref[...] + jnp.dot(mixed.astype(BF16), wo_ref[...], preferred_element_type=F32)


def _mix(x, ya, yb, ym, p, wb, wo):
    t, d = x.shape
    tm = TM_MIX
    bw = ya.shape[1]
    row = lambda i: (i, 0)
    return pl.pallas_call(
        _mix_kernel,
        grid=(t // tm,),
        in_specs=[pl.BlockSpec((tm, d), row),
                  pl.BlockSpec((tm, bw), row), pl.BlockSpec((tm, bw), row), pl.BlockSpec((tm, bw), row),
                  pl.BlockSpec((tm, 3 * d), lambda i: (i, P_GATES)),
                  pl.BlockSpec((3, bw, d), lambda i: (0, 0, 0)),
                  pl.BlockSpec((d, d), lambda i: (0, 0))],
        out_specs=pl.BlockSpec((tm, d), row),
        out_shape=jax.ShapeDtypeStruct((t, d), F32),
        name="mix_out",
    )(x, ya, yb, ym, p, wb, wo)


def _swiglu_step(h, wg, wu, wd):
    gate = jnp.dot(h, wg, preferred_element_type=F32)
    up = jnp.dot(h, wu, preferred_element_type=F32)
    act = (gate * jax.nn.sigmoid(gate) * up).astype(BF16)
    return jnp.dot(act, wd, preferred_element_type=F32)


def _ffn_kernel(x_ref, g_ref, wg_ref, wu_ref, wd_ref, o_ref, h_sc, acc_sc):
    j = pl.program_id(1)

    @pl.when(j == 0)
    def _():
        h_sc[...] = _rms(x_ref[...], g_ref[...]).astype(BF16)
        acc_sc[...] = jnp.zeros(acc_sc.shape, F32)

    acc_sc[...] += _swiglu_step(h_sc[...], wg_ref[...], wu_ref[...], wd_ref[...])

    @pl.when(j == pl.num_programs(1) - 1)
    def _():
        o_ref[...] = x_ref[...] + acc_sc[...]


def _ffn(x, g, wg, wu, wd):
    t, d = x.shape
    f = wg.shape[1]
    tm, tf = TM_FFN, TF_FFN
    return pl.pallas_call(
        _ffn_kernel,
        grid=(t // tm, f // tf),
        in_specs=[pl.BlockSpec((tm, d), lambda i, j: (i, 0)),
                  pl.BlockSpec((1, d), lambda i, j: (0, 0)),
                  pl.BlockSpec((d, tf), lambda i, j: (0, j)),
                  pl.BlockSpec((d, tf), lambda i, j: (0, j)),
                  pl.BlockSpec((tf, d), lambda i, j: (j, 0))],
        out_specs=pl.BlockSpec((tm, d), lambda i, j: (i, 0)),
        out_shape=jax.ShapeDtypeStruct((t, d), F32),
        scratch_shapes=[pltpu.VMEM((tm, d), BF16), pltpu.VMEM((tm, d), F32)],
        name="ffn_dense",
    )(x, g, wg, wu, wd)


def _router_kernel(x_ref, g_ref, wr_ref, idx_ref, wgt_ref):
    h = _rms(x_ref[...], g_ref[...])
    logits = lax.dot_general(wr_ref[...], h, (((1,), (1,)), ((), ())),
                             precision=lax.Precision.HIGHEST, preferred_element_type=F32)
    e_id = lax.broadcasted_iota(jnp.int32, logits.shape, 0)
    m1 = jnp.max(logits, axis=0, keepdims=True)
    i1 = jnp.min(jnp.where(logits == m1, e_id, N_EXPERTS), axis=0, keepdims=True)
    rest = jnp.where(e_id == i1, -jnp.inf, logits)
    m2 = jnp.max(rest, axis=0, keepdims=True)
    i2 = jnp.min(jnp.where(rest == m2, e_id, N_EXPERTS), axis=0, keepdims=True)
    e2 = jnp.exp(m2 - m1)
    w1 = 1.0 / (1.0 + e2)
    idx_ref[...] = jnp.concatenate([i1, i2], axis=0)
    wgt_ref[...] = jnp.concatenate([w1, e2 * w1], axis=0)


def _router(x, g, wr_t):
    t, d = x.shape
    tm = TM_ROUTE
    return pl.pallas_call(
        _router_kernel,
        grid=(t // tm,),
        in_specs=[pl.BlockSpec((tm, d), lambda i: (i, 0)),
                  pl.BlockSpec((1, d), lambda i: (0, 0)),
                  pl.BlockSpec((N_EXPERTS, d), lambda i: (0, 0))],
        out_specs=[pl.BlockSpec((TOP_K, tm), lambda i: (0, i)),
                   pl.BlockSpec((TOP_K, tm), lambda i: (0, i))],
        out_shape=[jax.ShapeDtypeStruct((TOP_K, t), jnp.int32),
                   jax.ShapeDtypeStruct((TOP_K, t), F32)],
        name="moe_router",
    )(x, g, wr_t)


def _moe_rows_per_step(n_steps):
    sub = 8
    return -(-TM_EXP // (n_steps * sub)) * sub


def _slot_tables(top_i, n_tok, buf_rows):
    n_asg = n_tok * TOP_K
    flat_e = top_i.T.reshape(n_asg)
    onehot = (flat_e[:, None] == jnp.arange(N_EXPERTS, dtype=jnp.int32)[None, :]).astype(jnp.int32)
    csum = jnp.cumsum(onehot, axis=0)
    rank = jnp.take_along_axis(csum, flat_e[:, None], axis=1)[:, 0] - 1
    counts = csum[-1]
    padded = (counts + TM_EXP - 1) // TM_EXP * TM_EXP
    pad_end = jnp.cumsum(padded)
    dest = (pad_end - padded)[flat_e] + rank
    n_slots = n_asg + N_EXPERTS * TM_EXP
    n_blk = n_slots // TM_EXP
    extra = buf_rows - TM_EXP
    slot_asg1 = jnp.zeros((n_slots,), jnp.int32).at[dest].set(jnp.arange(1, n_asg + 1, dtype=jnp.int32))
    is_pad = slot_asg1 == 0
    slot_tok = jnp.where(is_pad, 0, (slot_asg1 - 1) // TOP_K)
    slot_row = jnp.where(is_pad, n_asg - 1 + jnp.cumsum(is_pad.astype(jnp.int32)),
                         ((slot_asg1 - 1) % TOP_K) * n_tok + slot_tok)
    gather_tok = jnp.concatenate([slot_tok.reshape(n_blk, TM_EXP), jnp.zeros((n_blk, extra), jnp.int32)], axis=1)
    spare = n_slots + jnp.arange((n_blk + 1) * buf_rows - n_slots, dtype=jnp.int32)
    lead, over = spare[:TM_EXP], spare[TM_EXP:].reshape(n_blk + 1, extra)
    scatter_row = jnp.concatenate(
        [jnp.concatenate([lead[None], slot_row.reshape(n_blk, TM_EXP)], axis=0), over], axis=1)
    blk_e = jnp.minimum(jnp.searchsorted(pad_end, jnp.arange(n_blk, dtype=jnp.int32) * TM_EXP, side='right'),
                        N_EXPERTS - 1).astype(jnp.int32)
    return blk_e, gather_tok.reshape(-1), scatter_row.reshape(-1).astype(jnp.int32)


def _moe_ffn_kernel(blk_e_ref, gtok_ref, srow_ref, x_hbm, g_ref, wg_ref, wu_ref, wd_ref, o_hbm,
                    xg_sc, ys_sc, h_sc, acc_sc, gsem, ssem, *, rows_per_step):
    i, j = pl.program_id(0), pl.program_id(1)
    n_blk, n_steps = pl.num_programs(0), pl.num_programs(1)
    rows = h_sc.shape[0]
    buf_rows = xg_sc.shape[1]
    slot = i % 2
    other = 1 - slot

    def gathered(s):
        return pltpu.make_async_copy(x_hbm.at[pl.ds(0, buf_rows), :], xg_sc.at[s], gsem.at[s])

    def scattered(s):
        return pltpu.make_async_copy(ys_sc.at[s], o_hbm.at[pl.ds(0, buf_rows), :], ssem.at[s])

    def get_row(s, k, entry):
        return pltpu.make_async_copy(x_hbm.at[pl.ds(gtok_ref[entry], 1), :], xg_sc.at[s, pl.ds(k, 1), :],
                                     gsem.at[s])

    def put_row(s, k, entry):
        return pltpu.make_async_copy(ys_sc.at[s, pl.ds(k, 1), :], o_hbm.at[pl.ds(srow_ref[entry], 1), :],
                                     ssem.at[s])

    @pl.when((i == 0) & (j == 0))
    def _():
        ys_sc[...] = jnp.zeros(ys_sc.shape, F32)

        def start(k, c):
            get_row(0, k, k).start()
            return c

        lax.fori_loop(0, buf_rows, start, 0)

    @pl.when(j == 0)
    def _():
        gathered(slot).wait()
        h_sc[...] = _rms(xg_sc[slot, 0:rows, :], g_ref[...]).astype(BF16)
        acc_sc[...] = jnp.zeros(acc_sc.shape, F32)

    nxt = jnp.minimum(i + 1, n_blk - 1)
    for r in range(rows_per_step):
        k = j * rows_per_step + r
        get_row(other, k, nxt * buf_rows + k).start()
        put_row(other, k, i * buf_rows + k).start()

    acc_sc[...] += _swiglu_step(h_sc[...], wg_ref[...], wu_ref[...], wd_ref[...])

    @pl.when(j == n_steps - 1)
    def _():
        @pl.when(i > 0)
        def _():
            scattered(slot).wait()

        ys_sc[slot, 0:rows, :] = acc_sc[...]

    @pl.when((i == n_blk - 1) & (j == n_steps - 1))
    def _():
        def start(k, c):
            put_row(slot, k, n_blk * buf_rows + k).start()
            return c

        lax.fori_loop(0, buf_rows, start, 0)
        scattered(other).wait()
        scattered(slot).wait()
        gathered(other).wait()


def _moe_ffn(x, g, wg, wu, wd, blk_e, gather_tok, scatter_row):
    t, d = x.shape
    f = wg.shape[2]
    tm, tf = TM_EXP, TF_FFN
    n_steps = f // tf
    rows_per_step = _moe_rows_per_step(n_steps)
    buf_rows = rows_per_step * n_steps
    n_blk = gather_tok.shape[0] // buf_rows
    return pl.pallas_call(
        functools.partial(_moe_ffn_kernel, rows_per_step=rows_per_step),
        grid_spec=pltpu.PrefetchScalarGridSpec(
            num_scalar_prefetch=3,
            grid=(n_blk, n_steps),
            in_specs=[pl.BlockSpec(memory_space=pl.ANY),
                      pl.BlockSpec((1, d), lambda i, j, be, gt, sr: (0, 0)),
                      pl.BlockSpec((None, d, tf), lambda i, j, be, gt, sr: (be[i], 0, j)),
                      pl.BlockSpec((None, d, tf), lambda i, j, be, gt, sr: (be[i], 0, j)),
                      pl.BlockSpec((None, tf, d), lambda i, j, be, gt, sr: (be[i], j, 0))],
            out_specs=pl.BlockSpec(memory_space=pl.ANY),
            scratch_shapes=[pltpu.VMEM((2, buf_rows, d), F32), pltpu.VMEM((2, buf_rows, d), F32),
                            pltpu.VMEM((tm, d), BF16), pltpu.VMEM((tm, d), F32),
                            pltpu.SemaphoreType.DMA((2,)), pltpu.SemaphoreType.DMA((2,))],
        ),
        out_shape=jax.ShapeDtypeStruct((scatter_row.shape[0], d), F32),
        name="moe_ffn",
    )(blk_e, gather_tok, scatter_row, x, g, wg, wu, wd)


def _moe_combine_kernel(x_ref, w_ref, y0_ref, y1_ref, gf_ref, o_ref, *, final_norm):
    out = x_ref[...] + w_ref[:, 0:1] * y0_ref[...] + w_ref[:, 1:2] * y1_ref[...]
    if final_norm:
        out = _rms(out, gf_ref[...])
    o_ref[...] = out


def _moe_combine(x, top_w, y, g_final, final_norm):
    t, d = x.shape
    tm = TM_COMB
    nt = t // tm
    return pl.pallas_call(
        functools.partial(_moe_combine_kernel, final_norm=final_norm),
        grid=(nt,),
        in_specs=[pl.BlockSpec((tm, d), lambda i: (i, 0)),
                  pl.BlockSpec((tm, TOP_K), lambda i: (i, 0)),
                  pl.BlockSpec((tm, d), lambda i: (i, 0)),
                  pl.BlockSpec((tm, d), lambda i: (nt + i, 0)),
                  pl.BlockSpec((1, d), lambda i: (0, 0))],
        out_specs=pl.BlockSpec((tm, d), lambda i: (i, 0)),
        out_shape=jax.ShapeDtypeStruct((t, d), F32),
        name="moe_combine",
    )(x, top_w, y, y, g_final)


def _final_norm_kernel(x_ref, g_ref, o_ref):
    o_ref[...] = _rms(x_ref[...], g_ref[...])


def _final_norm(x, g):
    t, d = x.shape
    tm = 1024
    return pl.pallas_call(
        _final_norm_kernel,
        grid=(t // tm,),
        in_specs=[pl.BlockSpec((tm, d), lambda i: (i, 0)), pl.BlockSpec((1, d), lambda i: (0, 0))],
        out_specs=pl.BlockSpec((tm, d), lambda i: (i, 0)),
        out_shape=jax.ShapeDtypeStruct((t, d), F32),
        name="final_norm",
    )(x, g)


def _relayout_w_in(w_in):
    qa, ka, va, cq, ckv, kr, qm, gates = jnp.split(
        w_in, [512, 640, 768, 1024, 1152, 1184, 1696], axis=-1)
    pad = jnp.zeros(w_in.shape[:2] + (LANES - B_ROPE,), w_in.dtype)
    return jnp.concatenate([gates, qa, qm, ka, kr, pad, va, ckv, cq], axis=-1).astype(BF16)


def _relayout_latent(w_uq, w_ukv):
    depth = w_uq.shape[0]
    q = w_uq.reshape(depth, B_Q_RANK, B_HEADS, B_NOPE + B_ROPE)
    q = jnp.pad(q, ((0, 0), (0, 0), (0, 0), (0, LANES - B_NOPE - B_ROPE)))
    kv = w_ukv.reshape(depth, B_KV_RANK, B_HEADS, B_NOPE + B_V)
    k = jnp.pad(kv[..., :B_NOPE], ((0, 0), (0, 0), (0, 0), (0, LANES - B_NOPE)))
    v = jnp.pad(kv[..., B_NOPE:], ((0, 0), (0, 0), (0, 0), (0, LANES - B_V)))
    flat = lambda a, r: a.reshape(depth, r, B_HEADS * LANES).astype(BF16)
    return (jnp.swapaxes(flat(q, B_Q_RANK), 1, 2), flat(k, B_KV_RANK),
            jnp.swapaxes(flat(v, B_KV_RANK), 1, 2))


def kernel(x, mem, positions, g_mix, w_in, g_q_lat, w_uq, g_kv_lat, w_ukv, sink, w_mem_kv, w_branch, w_out,
           g_ffn, w_gate_dense, w_up_dense, w_down_dense, w_router, w_gate_exp, w_up_exp, w_down_exp,
           g_mem, g_final):
    batch, s_len, d = x.shape
    depth = w_in.shape[0]
    t = batch * s_len
    row = lambda a: a.reshape(1, -1).astype(F32)

    w_in_p = _relayout_w_in(w_in)
    wq_p, wk_p, wv_p = _relayout_latent(w_uq, w_ukv)
    w_branch_b, w_out_b = w_branch.astype(BF16), w_out.astype(BF16)
    wgd, wud, wdd = w_gate_dense.astype(BF16), w_up_dense.astype(BF16), w_down_dense.astype(BF16)
    wge, wue, wde = w_gate_exp.astype(BF16), w_up_exp.astype(BF16), w_down_exp.astype(BF16)
    wr_t = jnp.swapaxes(w_router, 1, 2).astype(F32)

    cos_a, sin_a, cos_b, sin_b, cos_bt, sin_bt = _rope_tables(positions)
    mem_kv = _mem_kv(mem.reshape(batch * mem.shape[1], d).astype(F32), row(g_mem), w_mem_kv.astype(BF16))

    xt = x.reshape(t, d).astype(F32)
    for l in range(depth):
        p = _in_proj(xt, row(g_mix[l]), w_in_p[l], cos_a, sin_a)
        y_a = _win_attn(p, sink[l].astype(F32), batch)
        qt, k, vt = _mla_prep(p, row(g_q_lat[l]), row(g_kv_lat[l]), wq_p[l], wk_p[l], wv_p[l],
                              cos_b, sin_b, cos_bt, sin_bt, batch)
        y_b = _mla_attn(qt, k, vt)
        y_m = _mem_attn(p, mem_kv[l], batch)
        xt = _mix(xt, y_a, y_b, y_m, p, w_branch_b[l], w_out_b[l])
        i = l // 2
        if l % 2 == 0:
            xt = _ffn(xt, row(g_ffn[l]), wgd[i], wud[i], wdd[i])
            if l == depth - 1:
                xt = _final_norm(xt, row(g_final))
        else:
            top_i, top_w = _router(xt, row(g_ffn[l]), wr_t[i])
            buf_rows = _moe_rows_per_step(wge.shape[3] // TF_FFN) * (wge.shape[3] // TF_FFN)
            blk_e, gather_tok, scatter_row = _slot_tables(top_i, t, buf_rows)
            y = _moe_ffn(xt, row(g_ffn[l]), wge[i], wue[i], wde[i], blk_e, gather_tok, scatter_row)
            xt = _moe_combine(xt, top_w.T, y, row(g_final), final_norm=(l == depth - 1))
    return xt.reshape(batch, s_len, d).astype(x.dtype)
```

```python
import functools

import jax
import jax.numpy as jnp
from jax import lax
from jax.experimental import pallas as pl
from jax.experimental.pallas import tpu as pltpu

F32 = jnp.float32
BF16 = jnp.bfloat16

ROPE_THETA = 10000.0
EPS = 1e-6
NEG_INF = -1e30

A_HEADS, A_KV_HEADS, A_HEAD_DIM, A_WINDOW = 8, 2, 64, 128
B_HEADS, B_Q_RANK, B_KV_RANK, B_NOPE, B_ROPE, B_V = 8, 256, 128, 64, 32, 64
M_HEADS, M_HEAD_DIM = 4, 128
N_EXPERTS, TOP_K = 8, 2
LANES = 128
SUBLANES = 8
BLK = 128

P_GATES = 0
P_QA = 3072
P_QM = 3584
P_KA = 4096
P_VA = 4352
P_CQ = 4608
P_WIDTH = 4864

TM_PROJ = 512
TM_PREP = 512
TM_MIX = 512
TM_MEM = 512
TM_FFN = 1024
TF_FFN = 512
TQ_MLA = 512
TK_MLA = 512
TM_ROUTE = 1024
TM_EXP = 512
TM_COMB = 512


def _rms(x, g):
    return x * lax.rsqrt(jnp.mean(x * x, axis=-1, keepdims=True) + EPS) * g


def _rot_tile(x, cos, sin_signed, first, shift):
    up = pltpu.roll(x, LANES - shift, 1)
    dn = pltpu.roll(x, shift, 1)
    return x * cos + jnp.where(first, up, dn) * sin_signed


def _rope_table_kernel(pos_ref, posr_ref, par_ref, invc_ref, ca_ref, sa_ref, cb_ref, sb_ref, cbt_ref, sbt_ref):
    pos = pos_ref[...]
    ang_a = pos * par_ref[0:1, :]
    ca_ref[...] = jnp.cos(ang_a)
    sa_ref[...] = jnp.sin(ang_a) * par_ref[1:2, :]
    ang_b = pos * par_ref[2:3, :]
    cb_ref[...] = jnp.cos(ang_b)
    sb_ref[...] = jnp.sin(ang_b) * par_ref[3:4, :]
    ang_t = invc_ref[...] * posr_ref[...]
    cbt_ref[...] = jnp.cos(ang_t)
    sbt_ref[...] = jnp.sin(ang_t)


def _rope_tables(positions):
    t = positions.size
    tm = 2048
    lane = jnp.arange(LANES)
    half_a = A_HEAD_DIM // 2
    inv_a = jnp.power(F32(ROPE_THETA), -jnp.arange(half_a, dtype=F32) * (2.0 / A_HEAD_DIM))
    inv_a = inv_a[lane % half_a]
    sgn_a = jnp.where((lane % A_HEAD_DIM) < half_a, -1.0, 1.0).astype(F32)
    half_b = B_ROPE // 2
    inv_b16 = jnp.power(F32(ROPE_THETA), -jnp.arange(half_b, dtype=F32) * (2.0 / B_ROPE))
    in_rope = (lane >= B_NOPE) & (lane < B_NOPE + B_ROPE)
    inv_b = jnp.where(in_rope, inv_b16[(lane - B_NOPE) % half_b], 0.0).astype(F32)
    sgn_b = jnp.where(in_rope, jnp.where(lane < B_NOPE + half_b, -1.0, 1.0), 0.0).astype(F32)
    par = jnp.zeros((8, LANES), F32).at[0].set(inv_a).at[1].set(sgn_a).at[2].set(inv_b).at[3].set(sgn_b)
    pos = positions.reshape(t, 1).astype(F32)
    tab = jax.ShapeDtypeStruct((t, LANES), F32)
    tab_t = jax.ShapeDtypeStruct((half_b, t), F32)
    return pl.pallas_call(
        _rope_table_kernel,
        grid=(t // tm,),
        in_specs=[pl.BlockSpec((tm, 1), lambda i: (i, 0)),
                  pl.BlockSpec((1, tm), lambda i: (0, i)),
                  pl.BlockSpec((8, LANES), lambda i: (0, 0)),
                  pl.BlockSpec((half_b, 1), lambda i: (0, 0))],
        out_specs=[pl.BlockSpec((tm, LANES), lambda i: (i, 0))] * 4
        + [pl.BlockSpec((half_b, tm), lambda i: (0, i))] * 2,
        out_shape=[tab] * 4 + [tab_t] * 2,
        name="rope_tables",
    )(pos, pos.reshape(1, t), par, inv_b16.reshape(half_b, 1))


def _in_proj_kernel(x_ref, g_ref, w_ref, cos_ref, sin_ref, o_ref):
    h = _rms(x_ref[...], g_ref[...]).astype(BF16)
    cos, sin = cos_ref[...], sin_ref[...]
    lane = lax.broadcasted_iota(jnp.int32, cos.shape, 1)
    first = (lane % A_HEAD_DIM) < (A_HEAD_DIM // 2)

    def rot(acc, n_tiles, scale):
        tiles = []
        for k in range(acc.shape[1] // LANES):
            xk = acc[:, k * LANES:(k + 1) * LANES]
            if k < n_tiles:
                xk = _rot_tile(xk, cos, sin, first, A_HEAD_DIM // 2)
                if scale != 1.0:
                    xk = xk * scale
            tiles.append(xk)
        return jnp.concatenate(tiles, axis=1)

    def proj(c0, c1):
        return jnp.dot(h, w_ref[:, c0:c1], preferred_element_type=F32)

    for c0 in range(P_GATES, P_QA, 512):
        o_ref[:, c0:c0 + 512] = proj(c0, c0 + 512).astype(BF16)
    o_ref[:, P_QA:P_QM] = rot(proj(P_QA, P_QM), 4, A_HEAD_DIM ** -0.5).astype(BF16)
    o_ref[:, P_QM:P_KA] = (proj(P_QM, P_KA) * (M_HEAD_DIM ** -0.5)).astype(BF16)
    o_ref[:, P_KA:P_VA] = rot(proj(P_KA, P_VA), 1, 1.0).astype(BF16)
    o_ref[:, P_VA:P_WIDTH] = proj(P_VA, P_WIDTH).astype(BF16)


def _in_proj(x, g, w, cos_a, sin_a):
    t, d = x.shape
    tm = TM_PROJ
    return pl.pallas_call(
        _in_proj_kernel,
        grid=(t // tm,),
        in_specs=[pl.BlockSpec((tm, d), lambda i: (i, 0)),
                  pl.BlockSpec((1, d), lambda i: (0, 0)),
                  pl.BlockSpec((d, P_WIDTH), lambda i: (0, 0)),
                  pl.BlockSpec((tm, LANES), lambda i: (i, 0)),
                  pl.BlockSpec((tm, LANES), lambda i: (i, 0))],
        out_specs=pl.BlockSpec((tm, P_WIDTH), lambda i: (i, 0)),
        out_shape=jax.ShapeDtypeStruct((t, P_WIDTH), BF16),
        name="in_proj",
    )(x, g, w, cos_a, sin_a)


def _mla_prep_kernel(cq_ref, ckv_ref, kr_ref, gq_ref, gkv_ref, wqt_ref, wk_ref, wvt_ref,
                     cos_ref, sin_ref, cost_ref, sint_ref, qt_ref, k_ref, vt_ref):
    half = B_ROPE // 2
    nt = (((1,), (1,)), ((), ()))
    cq = _rms(cq_ref[...].astype(F32), gq_ref[...]).astype(BF16)
    ckv = _rms(ckv_ref[:, LANES:].astype(F32), gkv_ref[...]).astype(BF16)

    cos, sin = cos_ref[...], sin_ref[...]
    lane = lax.broadcasted_iota(jnp.int32, cos.shape, 1)
    first = (lane >= B_NOPE) & (lane < B_NOPE + half)
    kr = _rot_tile(pltpu.roll(kr_ref[:, LANES:].astype(F32), B_NOPE, 1), cos, sin, first, half)
    k = jnp.dot(ckv, wk_ref[...], preferred_element_type=F32)
    for hd in range(B_HEADS):
        sl = slice(hd * LANES, (hd + 1) * LANES)
        k_ref[:, sl] = (k[:, sl] + kr).astype(BF16)

    ct, st = cost_ref[...], sint_ref[...]
    qscale = (B_NOPE + B_ROPE) ** -0.5 * 1.4426950408889634
    qt = lax.dot_general(wqt_ref[...], cq, nt, preferred_element_type=F32)
    vt = lax.dot_general(wvt_ref[...], ckv, nt, preferred_element_type=F32)
    row = lax.broadcasted_iota(jnp.int32, (LANES, vt.shape[1]), 0)
    ones_row = (row == B_V).astype(F32)
    for hd in range(B_HEADS):
        r0 = hd * LANES
        x1 = qt[r0 + B_NOPE:r0 + B_NOPE + half]
        x2 = qt[r0 + B_NOPE + half:r0 + B_NOPE + B_ROPE]
        qh = jnp.concatenate([qt[r0:r0 + B_NOPE], x1 * ct - x2 * st, x2 * ct + x1 * st,
                              qt[r0 + B_NOPE + B_ROPE:r0 + LANES]], axis=0)
        qt_ref[r0:r0 + LANES, :] = (qh * qscale).astype(BF16)
        vt_ref[r0:r0 + LANES, :] = (vt[r0:r0 + LANES] + ones_row).astype(BF16)


def _mla_prep(p, gq, gkv, wqt, wk, wvt, cos_b, sin_b, cos_bt, sin_bt, batch):
    t = p.shape[0]
    s_len = t // batch
    tm = TM_PREP
    nt = s_len // tm
    hw = B_HEADS * LANES
    half = B_ROPE // 2
    const = lambda b, i: (0, 0)
    tok = lambda b, i: (b * nt + i, 0)
    return pl.pallas_call(
        _mla_prep_kernel,
        grid=(batch, nt),
        in_specs=[pl.BlockSpec((tm, 256), lambda b, i: (b * nt + i, P_CQ // 256)),
                  pl.BlockSpec((tm, 256), lambda b, i: (b * nt + i, P_VA // 256)),
                  pl.BlockSpec((tm, 256), lambda b, i: (b * nt + i, P_KA // 256)),
                  pl.BlockSpec((1, B_Q_RANK), const),
                  pl.BlockSpec((1, B_KV_RANK), const),
                  pl.BlockSpec((hw, B_Q_RANK), const),
                  pl.BlockSpec((B_KV_RANK, hw), const),
                  pl.BlockSpec((hw, B_KV_RANK), const),
                  pl.BlockSpec((tm, LANES), tok),
                  pl.BlockSpec((tm, LANES), tok),
                  pl.BlockSpec((half, tm), lambda b, i: (0, b * nt + i)),
                  pl.BlockSpec((half, tm), lambda b, i: (0, b * nt + i))],
        out_specs=[pl.BlockSpec((None, hw, tm), lambda b, i: (b, 0, i)),
                   pl.BlockSpec((None, tm, hw), lambda b, i: (b, i, 0)),
                   pl.BlockSpec((None, hw, tm), lambda b, i: (b, 0, i))],
        out_shape=[jax.ShapeDtypeStruct((batch, hw, s_len), BF16),
                   jax.ShapeDtypeStruct((batch, s_len, hw), BF16),
                   jax.ShapeDtypeStruct((batch, hw, s_len), BF16)],
        name="mla_prep",
    )(p, p, p, gq, gkv, wqt, wk, wvt, cos_b, sin_b, cos_bt, sin_bt)


def _mla_attn_kernel(qt_ref, k_ref, vt_ref, o_ref, m_sc, acc_sc, s0_sc, s1_sc, *, tk):
    n_chunks = k_ref.shape[0] // tk
    m_sc[...] = jnp.full(m_sc.shape, -jnp.inf, F32)
    acc_sc[...] = jnp.zeros(acc_sc.shape, F32)
    heads = [slice(hd * LANES, (hd + 1) * LANES) for hd in range(2)]

    def keys(j):
        return pl.ds(pl.multiple_of(j * tk, tk), tk)

    def scores(j, dst):
        for hd, sl in enumerate(heads):
            dst[hd] = jnp.dot(k_ref[keys(j), sl], qt_ref[sl, :], preferred_element_type=F32)

    def accumulate(j, src):
        for hd, sl in enumerate(heads):
            st = src[hd]
            m_old = m_sc[hd]
            m_new = jnp.maximum(m_old, jnp.max(st, axis=0, keepdims=True))
            pt = jnp.exp2(st - m_new).astype(BF16)
            acc_sc[hd] = acc_sc[hd] * jnp.exp2(m_old - m_new) + jnp.dot(
                vt_ref[sl, keys(j)], pt, preferred_element_type=F32)
            m_sc[hd] = m_new

    scores(0, s0_sc)

    def body(jj, carry):
        j = 2 * jj
        scores(j + 1, s1_sc)
        accumulate(j, s0_sc)
        scores(j + 2, s0_sc)
        accumulate(j + 1, s1_sc)
        return carry

    lax.fori_loop(0, n_chunks // 2 - 1, body, 0)
    scores(n_chunks - 1, s1_sc)
    accumulate(n_chunks - 2, s0_sc)
    accumulate(n_chunks - 1, s1_sc)
    outs = []
    for hd in range(2):
        acc = acc_sc[hd]
        outs.append(acc[:B_V] * (1.0 / acc[B_V:B_V + 1]))
    o_ref[...] = jnp.concatenate(outs, axis=0).T.astype(BF16)


def _mla_attn(qt, k, vt):
    batch, s_len, hw = k.shape
    tq = TQ_MLA
    out = pl.pallas_call(
        functools.partial(_mla_attn_kernel, tk=TK_MLA),
        grid=(batch, B_HEADS // 2, s_len // tq),
        in_specs=[pl.BlockSpec((None, 2 * LANES, tq), lambda b, p, i: (b, p, i)),
                  pl.BlockSpec((None, s_len, 2 * LANES), lambda b, p, i: (b, 0, p)),
                  pl.BlockSpec((None, 2 * LANES, s_len), lambda b, p, i: (b, p, 0))],
        out_specs=pl.BlockSpec((None, tq, LANES), lambda b, p, i: (b, i, p)),
        out_shape=jax.ShapeDtypeStruct((batch, s_len, B_HEADS * B_V), BF16),
        scratch_shapes=[pltpu.VMEM((2, 1, tq), F32), pltpu.VMEM((2, LANES, tq), F32),
                        pltpu.VMEM((2, TK_MLA, tq), F32), pltpu.VMEM((2, TK_MLA, tq), F32)],
        name="mla_attn",
    )(qt, k, vt)
    return out.reshape(batch * s_len, B_HEADS * B_V)


def _win_attn_kernel(sink_ref, q_ref, kp_ref, kc_ref, kn_ref, vp_ref, vc_ref, vn_ref, o_ref, *, s_len):
    n = pl.program_id(1)
    grp = A_HEADS // A_KV_HEADS
    dh = A_HEAD_DIM
    qi = lax.broadcasted_iota(jnp.int32, (BLK, 3 * BLK), 0)
    kj = lax.broadcasted_iota(jnp.int32, (BLK, 3 * BLK), 1)
    k_abs = (n - 1) * BLK + kj
    ok = (jnp.abs(kj - BLK - qi) <= A_WINDOW) & (k_abs >= 0) & (k_abs < s_len)
    ok = jnp.concatenate([ok] * grp, axis=0)
    row_head = lax.broadcasted_iota(jnp.int32, (grp * BLK, 1), 0) // BLK
    outs = []
    for g in range(A_KV_HEADS):
        ksl = slice(g * dh, (g + 1) * dh)
        kb = jnp.concatenate([kp_ref[:, ksl], kc_ref[:, ksl], kn_ref[:, ksl]], axis=0)
        vb = jnp.concatenate([vp_ref[:, ksl], vc_ref[:, ksl], vn_ref[:, ksl]], axis=0)
        q4 = jnp.concatenate([q_ref[:, (g * grp + a) * dh:(g * grp + a + 1) * dh] for a in range(grp)], axis=0)
        sc = lax.dot_general(q4, kb, (((1,), (1,)), ((), ())), preferred_element_type=F32)
        sc = jnp.where(ok, sc, NEG_INF)
        sk = jnp.zeros((grp * BLK, 1), F32)
        for a in range(grp):
            sk = jnp.where(row_head == a, sink_ref[g * grp + a], sk)
        m = jnp.maximum(jnp.max(sc, axis=-1, keepdims=True), sk)
        e = jnp.exp(sc - m)
        den = jnp.sum(e, axis=-1, keepdims=True) + jnp.exp(sk - m)
        pr = (e * (1.0 / den)).astype(BF16)
        o4 = jnp.dot(pr, vb, preferred_element_type=F32)
        outs.extend(o4[a * BLK:(a + 1) * BLK] for a in range(grp))
    o_ref[...] = jnp.concatenate(outs, axis=1).astype(BF16)


def _win_attn(p, sink, batch):
    t = p.shape[0]
    s_len = t // batch
    nb = s_len // BLK
    p3 = p.reshape(batch, s_len, P_WIDTH)
    kcol, vcol = P_KA // 256, P_VA // 256
    prev = lambda b, n, s: (b, jnp.maximum(n - 1, 0), kcol)
    cur = lambda b, n, s: (b, n, kcol)
    nxt = lambda b, n, s: (b, jnp.minimum(n + 1, nb - 1), kcol)
    vprev = lambda b, n, s: (b, jnp.maximum(n - 1, 0), vcol)
    vcur = lambda b, n, s: (b, n, vcol)
    vnxt = lambda b, n, s: (b, jnp.minimum(n + 1, nb - 1), vcol)
    blk = lambda im: pl.BlockSpec((None, BLK, 256), im)
    out = pl.pallas_call(
        functools.partial(_win_attn_kernel, s_len=s_len),
        grid_spec=pltpu.PrefetchScalarGridSpec(
            num_scalar_prefetch=1,
            grid=(batch, nb),
            in_specs=[pl.BlockSpec((None, BLK, 512), lambda b, n, s: (b, n, P_QA // 512)),
                      blk(prev), blk(cur), blk(nxt), blk(vprev), blk(vcur), blk(vnxt)],
            out_specs=pl.BlockSpec((None, BLK, A_HEADS * A_HEAD_DIM), lambda b, n, s: (b, n, 0)),
        ),
        out_shape=jax.ShapeDtypeStruct((batch, s_len, A_HEADS * A_HEAD_DIM), BF16),
        name="win_attn",
    )(sink, p3, p3, p3, p3, p3, p3, p3)
    return out.reshape(t, A_HEADS * A_HEAD_DIM)


def _mem_kv_kernel(mem_ref, g_ref, w_ref, o_ref):
    mem_n = _rms(mem_ref[...], g_ref[...]).astype(BF16)
    o_ref[...] = jnp.dot(mem_n, w_ref[...], preferred_element_type=F32).astype(BF16)


def _mem_kv(mem2, g_mem, w_mem_kv):
    depth, d, width = w_mem_kv.shape
    rows = mem2.shape[0]
    return pl.pallas_call(
        _mem_kv_kernel,
        grid=(depth,),
        in_specs=[pl.BlockSpec((rows, d), lambda l: (0, 0)),
                  pl.BlockSpec((1, d), lambda l: (0, 0)),
                  pl.BlockSpec((None, d, width), lambda l: (l, 0, 0))],
        out_specs=pl.BlockSpec((None, rows, width), lambda l: (l, 0, 0)),
        out_shape=jax.ShapeDtypeStruct((depth, rows, width), BF16),
        name="mem_kv",
    )(mem2, g_mem, w_mem_kv)


def _mem_attn_kernel(q_ref, kv_ref, o_ref):
    dh = M_HEAD_DIM
    for hd in range(M_HEADS):
        q = q_ref[:, hd * dh:(hd + 1) * dh]
        k = kv_ref[:, hd * dh:(hd + 1) * dh]
        v = kv_ref[:, (M_HEADS + hd) * dh:(M_HEADS + hd + 1) * dh]
        sc = lax.dot_general(q, k, (((1,), (1,)), ((), ())), preferred_element_type=F32)
        e = jnp.exp(sc - jnp.max(sc, axis=-1, keepdims=True))
        pr = (e * (1.0 / jnp.sum(e, axis=-1, keepdims=True))).astype(BF16)
        o_ref[:, hd * dh:(hd + 1) * dh] = jnp.dot(pr, v, preferred_element_type=F32).astype(BF16)


def _mem_attn(p, mem_kv_l, batch):
    t = p.shape[0]
    s_len = t // batch
    tm = TM_MEM
    n_mem = mem_kv_l.shape[0] // batch
    width = M_HEADS * M_HEAD_DIM
    p3 = p.reshape(batch, s_len, P_WIDTH)
    kv3 = mem_kv_l.reshape(batch, n_mem, 2 * width)
    out = pl.pallas_call(
        _mem_attn_kernel,
        grid=(batch, s_len // tm),
        in_specs=[pl.BlockSpec((None, tm, width), lambda b, i: (b, i, P_QM // width)),
                  pl.BlockSpec((None, n_mem, 2 * width), lambda b, i: (b, 0, 0))],
        out_specs=pl.BlockSpec((None, tm, width), lambda b, i: (b, i, 0)),
        out_shape=jax.ShapeDtypeStruct((batch, s_len, width), BF16),
        name="mem_attn",
    )(p3, kv3)
    return out.reshape(t, width)


def _mix_kernel(x_ref, ya_ref, yb_ref, ym_ref, g_ref, wb_ref, wo_ref, o_ref):
    d = x_ref.shape[1]
    mixed = None
    for i, y_ref in enumerate((ya_ref, yb_ref, ym_ref)):
        gate = jax.nn.sigmoid(g_ref[:, i * d:(i + 1) * d].astype(F32))
        term = gate * jnp.dot(y_ref[...], wb_ref[i], preferred_element_type=F32)
        mixed = term if mixed is None else mixed + term
    o_ref[...] = x_ref[...] + jnp.dot(mixed.astype(BF16), wo_ref[...], preferred_element_type=F32)


def _mix(x, ya, yb, ym, p, wb, wo):
    t, d = x.shape
    tm = TM_MIX
    bw = ya.shape[1]
    row = lambda i: (i, 0)
    return pl.pallas_call(
        _mix_kernel,
        grid=(t // tm,),
        in_specs=[pl.BlockSpec((tm, d), row),
                  pl.BlockSpec((tm, bw), row), pl.BlockSpec((tm, bw), row), pl.BlockSpec((tm, bw), row),
                  pl.BlockSpec((tm, 3 * d), lambda i: (i, P_GATES)),
                  pl.BlockSpec((3, bw, d), lambda i: (0, 0, 0)),
                  pl.BlockSpec((d, d), lambda i: (0, 0))],
        out_specs=pl.BlockSpec((tm, d), row),
        out_shape=jax.ShapeDtypeStruct((t, d), F32),
        name="mix_out",
    )(x, ya, yb, ym, p, wb, wo)


def _swiglu_step(h, wg, wu, wd):
    gate = jnp.dot(h, wg, preferred_element_type=F32)
    up = jnp.dot(h, wu, preferred_element_type=F32)
    act = (gate * jax.nn.sigmoid(gate) * up).astype(BF16)
    return jnp.dot(act, wd, preferred_element_type=F32)


def _ffn_kernel(x_ref, g_ref, wg_ref, wu_ref, wd_ref, o_ref, h_sc, acc_sc):
    j = pl.program_id(1)

    @pl.when(j == 0)
    def _():
        h_sc[...] = _rms(x_ref[...], g_ref[...]).astype(BF16)
        acc_sc[...] = jnp.zeros(acc_sc.shape, F32)

    acc_sc[...] += _swiglu_step(h_sc[...], wg_ref[...], wu_ref[...], wd_ref[...])

    @pl.when(j == pl.num_programs(1) - 1)
    def _():
        o_ref[...] = x_ref[...] + acc_sc[...]


def _ffn(x, g, wg, wu, wd):
    t, d = x.shape
    f = wg.shape[1]
    tm, tf = TM_FFN, TF_FFN
    return pl.pallas_call(
        _ffn_kernel,
        grid=(t // tm, f // tf),
        in_specs=[pl.BlockSpec((tm, d), lambda i, j: (i, 0)),
                  pl.BlockSpec((1, d), lambda i, j: (0, 0)),
                  pl.BlockSpec((d, tf), lambda i, j: (0, j)),
                  pl.BlockSpec((d, tf), lambda i, j: (0, j)),
                  pl.BlockSpec((tf, d), lambda i, j: (j, 0))],
        out_specs=pl.BlockSpec((tm, d), lambda i, j: (i, 0)),
        out_shape=jax.ShapeDtypeStruct((t, d), F32),
        scratch_shapes=[pltpu.VMEM((tm, d), BF16), pltpu.VMEM((tm, d), F32)],
        name="ffn_dense",
    )(x, g, wg, wu, wd)


def _router_kernel(x_ref, g_ref, wr_ref, idx_ref, wgt_ref, xt_ref):
    x = x_ref[...]
    for s in range(x.shape[1] // LANES):
        xt_ref[pl.ds(s, x.shape[0], stride=SUBLANES), :] = x[:, s * LANES:(s + 1) * LANES]
    h = _rms(x, g_ref[...])
    logits = lax.dot_general(wr_ref[...], h, (((1,), (1,)), ((), ())),
                             precision=lax.Precision.HIGHEST, preferred_element_type=F32)
    e_id = lax.broadcasted_iota(jnp.int32, logits.shape, 0)
    m1 = jnp.max(logits, axis=0, keepdims=True)
    i1 = jnp.min(jnp.where(logits == m1, e_id, N_EXPERTS), axis=0, keepdims=True)
    rest = jnp.where(e_id == i1, -jnp.inf, logits)
    m2 = jnp.max(rest, axis=0, keepdims=True)
    i2 = jnp.min(jnp.where(rest == m2, e_id, N_EXPERTS), axis=0, keepdims=True)
    e2 = jnp.exp(m2 - m1)
    w1 = 1.0 / (1.0 + e2)
    idx_ref[...] = jnp.concatenate([i1, i2], axis=0)
    wgt_ref[...] = jnp.concatenate([w1, e2 * w1], axis=0)


def _router(x, g, wr_t):
    t, d = x.shape
    tm = TM_ROUTE
    return pl.pallas_call(
        _router_kernel,
        grid=(t // tm,),
        in_specs=[pl.BlockSpec((tm, d), lambda i: (i, 0)),
                  pl.BlockSpec((1, d), lambda i: (0, 0)),
                  pl.BlockSpec((N_EXPERTS, d), lambda i: (0, 0))],
        out_specs=[pl.BlockSpec((TOP_K, tm), lambda i: (0, i)),
                   pl.BlockSpec((TOP_K, tm), lambda i: (0, i)),
                   pl.BlockSpec((tm * SUBLANES, LANES), lambda i: (i, 0))],
        out_shape=[jax.ShapeDtypeStruct((TOP_K, t), jnp.int32),
                   jax.ShapeDtypeStruct((TOP_K, t), F32),
                   jax.ShapeDtypeStruct((t * SUBLANES, LANES), F32)],
        name="moe_router",
    )(x, g, wr_t)


def _moe_rows_per_step(n_steps):
    return -(-TM_EXP // (n_steps * SUBLANES)) * SUBLANES


def _slot_tables(top_i, n_tok, buf_rows):
    n_asg = n_tok * TOP_K
    flat_e = top_i.T.reshape(n_asg)
    onehot = (flat_e[:, None] == jnp.arange(N_EXPERTS, dtype=jnp.int32)[None, :]).astype(jnp.int32)
    csum = jnp.cumsum(onehot, axis=0)
    rank = jnp.take_along_axis(csum, flat_e[:, None], axis=1)[:, 0] - 1
    counts = csum[-1]
    padded = (counts + TM_EXP - 1) // TM_EXP * TM_EXP
    pad_end = jnp.cumsum(padded)
    dest = (pad_end - padded)[flat_e] + rank
    n_slots = n_asg + N_EXPERTS * TM_EXP
    n_blk = n_slots // TM_EXP
    extra = buf_rows - TM_EXP
    slot_asg1 = jnp.zeros((n_slots,), jnp.int32).at[dest].set(jnp.arange(1, n_asg + 1, dtype=jnp.int32))
    is_pad = slot_asg1 == 0
    slot_tok = jnp.where(is_pad, 0, (slot_asg1 - 1) // TOP_K)
    slot_row = jnp.where(is_pad, n_asg - 1 + jnp.cumsum(is_pad.astype(jnp.int32)),
                         ((slot_asg1 - 1) % TOP_K) * n_tok + slot_tok)
    gather_tok = jnp.concatenate([slot_tok.reshape(n_blk, TM_EXP), jnp.zeros((n_blk, extra), jnp.int32)], axis=1)
    spare = n_slots + jnp.arange((n_blk + 1) * buf_rows - n_slots, dtype=jnp.int32)
    lead, over = spare[:TM_EXP], spare[TM_EXP:].reshape(n_blk + 1, extra)
    scatter_row = jnp.concatenate(
        [jnp.concatenate([lead[None], slot_row.reshape(n_blk, TM_EXP)], axis=0), over], axis=1)
    blk_e = jnp.minimum(jnp.searchsorted(pad_end, jnp.arange(n_blk, dtype=jnp.int32) * TM_EXP, side='right'),
                        N_EXPERTS - 1).astype(jnp.int32)
    return blk_e, gather_tok.reshape(-1), scatter_row.reshape(-1).astype(jnp.int32)


def _moe_ffn_kernel(blk_e_ref, gtok_ref, srow_ref, x_hbm, g_ref, wg_ref, wu_ref, wd_ref, o_hbm,
                    xg_sc, ys_sc, h_sc, acc_sc, gsem, ssem, *, rows_per_step):
    i, j = pl.program_id(0), pl.program_id(1)
    n_blk, n_steps = pl.num_programs(0), pl.num_programs(1)
    rows, d = h_sc.shape
    tr = d // LANES
    buf_rows = xg_sc.shape[1] // tr
    slot = i % 2
    other = 1 - slot

    def tile(ref, k):
        return ref.at[pl.ds(pl.multiple_of(k * tr, tr), tr), :]

    def gathered(s):
        return pltpu.make_async_copy(x_hbm.at[pl.ds(0, buf_rows * tr), :], xg_sc.at[s], gsem.at[s])

    def scattered(s):
        return pltpu.make_async_copy(ys_sc.at[s], o_hbm.at[pl.ds(0, buf_rows * tr), :], ssem.at[s])

    def get_row(s, k, entry):
        return pltpu.make_async_copy(tile(x_hbm, gtok_ref[entry]), tile(xg_sc.at[s], k), gsem.at[s])

    def put_row(s, k, entry):
        return pltpu.make_async_copy(tile(ys_sc.at[s], k), tile(o_hbm, srow_ref[entry]), ssem.at[s])

    def slabs(ref, s):
        return [ref[s, pl.ds(c, rows, stride=tr), :] for c in range(tr)]

    @pl.when((i == 0) & (j == 0))
    def _():
        ys_sc[...] = jnp.zeros(ys_sc.shape, F32)

        def start(k, c):
            get_row(0, k, k).start()
            return c

        lax.fori_loop(0, buf_rows, start, 0)

    @pl.when(j == 0)
    def _():
        gathered(slot).wait()
        for s in range(2):
            @pl.when(slot == s)
            def _():
                xs = slabs(xg_sc, s)
                ms = sum(jnp.sum(x * x, axis=-1, keepdims=True) for x in xs) * (1.0 / d)
                inv = lax.rsqrt(ms + EPS)
                for c, x in enumerate(xs):
                    cols = slice(c * LANES, (c + 1) * LANES)
                    h_sc[:, cols] = (x * inv * g_ref[:, cols]).astype(BF16)

        acc_sc[...] = jnp.zeros(acc_sc.shape, F32)

    nxt = jnp.minimum(i + 1, n_blk - 1)
    for r in range(rows_per_step):
        k = j * rows_per_step + r
        get_row(other, k, nxt * buf_rows + k).start()
        put_row(other, k, i * buf_rows + k).start()

    acc_sc[...] += _swiglu_step(h_sc[...], wg_ref[...], wu_ref[...], wd_ref[...])

    @pl.when(j == n_steps - 1)
    def _():
        @pl.when(i > 0)
        def _():
            scattered(slot).wait()

        for s in range(2):
            @pl.when(slot == s)
            def _():
                for c in range(tr):
                    ys_sc[s, pl.ds(c, rows, stride=tr), :] = acc_sc[:, c * LANES:(c + 1) * LANES]

    @pl.when((i == n_blk - 1) & (j == n_steps - 1))
    def _():
        def start(k, c):
            put_row(slot, k, n_blk * buf_rows + k).start()
            return c

        lax.fori_loop(0, buf_rows, start, 0)
        scattered(other).wait()
        scattered(slot).wait()
        gathered(other).wait()


def _moe_ffn(x_tiles, g, wg, wu, wd, blk_e, gather_tok, scatter_row):
    d, f = wg.shape[1], wg.shape[2]
    tr = d // LANES
    assert tr % SUBLANES == 0
    tm, tf = TM_EXP, TF_FFN
    n_steps = f // tf
    rows_per_step = _moe_rows_per_step(n_steps)
    buf_rows = rows_per_step * n_steps
    n_blk = gather_tok.shape[0] // buf_rows
    return pl.pallas_call(
        functools.partial(_moe_ffn_kernel, rows_per_step=rows_per_step),
        grid_spec=pltpu.PrefetchScalarGridSpec(
            num_scalar_prefetch=3,
            grid=(n_blk, n_steps),
            in_specs=[pl.BlockSpec(memory_space=pl.ANY),
                      pl.BlockSpec((1, d), lambda i, j, be, gt, sr: (0, 0)),
                      pl.BlockSpec((None, d, tf), lambda i, j, be, gt, sr: (be[i], 0, j)),
                      pl.BlockSpec((None, d, tf), lambda i, j, be, gt, sr: (be[i], 0, j)),
                      pl.BlockSpec((None, tf, d), lambda i, j, be, gt, sr: (be[i], j, 0))],
            out_specs=pl.BlockSpec(memory_space=pl.ANY),
            scratch_shapes=[pltpu.VMEM((2, buf_rows * tr, LANES), F32), pltpu.VMEM((2, buf_rows * tr, LANES), F32),
                            pltpu.VMEM((tm, d), BF16), pltpu.VMEM((tm, d), F32),
                            pltpu.SemaphoreType.DMA((2,)), pltpu.SemaphoreType.DMA((2,))],
        ),
        out_shape=jax.ShapeDtypeStruct((scatter_row.shape[0] * tr, LANES), F32),
        name="moe_ffn",
    )(blk_e, gather_tok, scatter_row, x_tiles, g, wg, wu, wd)


def _moe_combine_kernel(x_ref, w_ref, y0_ref, y1_ref, gf_ref, o_ref, *, final_norm):
    tm, d = x_ref.shape
    tr = d // LANES
    w0, w1 = w_ref[:, 0:1], w_ref[:, 1:2]
    outs = []
    for c in range(tr):
        rows = pl.ds(c, tm, stride=tr)
        outs.append(x_ref[:, c * LANES:(c + 1) * LANES] + w0 * y0_ref[rows, :] + w1 * y1_ref[rows, :])
    out = jnp.concatenate(outs, axis=1)
    if final_norm:
        out = _rms(out, gf_ref[...])
    o_ref[...] = out


def _moe_combine(x, top_w, y, g_final, final_norm):
    t, d = x.shape
    tr = d // LANES
    tm = TM_COMB
    nt = t // tm
    return pl.pallas_call(
        functools.partial(_moe_combine_kernel, final_norm=final_norm),
        grid=(nt,),
        in_specs=[pl.BlockSpec((tm, d), lambda i: (i, 0)),
                  pl.BlockSpec((tm, TOP_K), lambda i: (i, 0)),
                  pl.BlockSpec((tm * tr, LANES), lambda i: (i, 0)),
                  pl.BlockSpec((tm * tr, LANES), lambda i: (nt + i, 0)),
                  pl.BlockSpec((1, d), lambda i: (0, 0))],
        out_specs=pl.BlockSpec((tm, d), lambda i: (i, 0)),
        out_shape=jax.ShapeDtypeStruct((t, d), F32),
        name="moe_combine",
    )(x, top_w, y, y, g_final)


def _final_norm_kernel(x_ref, g_ref, o_ref):
    o_ref[...] = _rms(x_ref[...], g_ref[...])


def _final_norm(x, g):
    t, d = x.shape
    tm = 1024
    return pl.pallas_call(
        _final_norm_kernel,
        grid=(t // tm,),
        in_specs=[pl.BlockSpec((tm, d), lambda i: (i, 0)), pl.BlockSpec((1, d), lambda i: (0, 0))],
        out_specs=pl.BlockSpec((tm, d), lambda i: (i, 0)),
        out_shape=jax.ShapeDtypeStruct((t, d), F32),
        name="final_norm",
    )(x, g)


def _relayout_w_in(w_in):
    qa, ka, va, cq, ckv, kr, qm, gates = jnp.split(
        w_in, [512, 640, 768, 1024, 1152, 1184, 1696], axis=-1)
    pad = jnp.zeros(w_in.shape[:2] + (LANES - B_ROPE,), w_in.dtype)
    return jnp.concatenate([gates, qa, qm, ka, kr, pad, va, ckv, cq], axis=-1).astype(BF16)


def _relayout_latent(w_uq, w_ukv):
    depth = w_uq.shape[0]
    q = w_uq.reshape(depth, B_Q_RANK, B_HEADS, B_NOPE + B_ROPE)
    q = jnp.pad(q, ((0, 0), (0, 0), (0, 0), (0, LANES - B_NOPE - B_ROPE)))
    kv = w_ukv.reshape(depth, B_KV_RANK, B_HEADS, B_NOPE + B_V)
    k = jnp.pad(kv[..., :B_NOPE], ((0, 0), (0, 0), (0, 0), (0, LANES - B_NOPE)))
    v = jnp.pad(kv[..., B_NOPE:], ((0, 0), (0, 0), (0, 0), (0, LANES - B_V)))
    flat = lambda a, r: a.reshape(depth, r, B_HEADS * LANES).astype(BF16)
    return (jnp.swapaxes(flat(q, B_Q_RANK), 1, 2), flat(k, B_KV_RANK),
            jnp.swapaxes(flat(v, B_KV_RANK), 1, 2))


def kernel(x, mem, positions, g_mix, w_in, g_q_lat, w_uq, g_kv_lat, w_ukv, sink, w_mem_kv, w_branch, w_out,
           g_ffn, w_gate_dense, w_up_dense, w_down_dense, w_router, w_gate_exp, w_up_exp, w_down_exp,
           g_mem, g_final):
    batch, s_len, d = x.shape
    depth = w_in.shape[0]
    t = batch * s_len
    row = lambda a: a.reshape(1, -1).astype(F32)

    w_in_p = _relayout_w_in(w_in)
    wq_p, wk_p, wv_p = _relayout_latent(w_uq, w_ukv)
    w_branch_b, w_out_b = w_branch.astype(BF16), w_out.astype(BF16)
    wgd, wud, wdd = w_gate_dense.astype(BF16), w_up_dense.astype(BF16), w_down_dense.astype(BF16)
    wge, wue, wde = w_gate_exp.astype(BF16), w_up_exp.astype(BF16), w_down_exp.astype(BF16)
    wr_t = jnp.swapaxes(w_router, 1, 2).astype(F32)

    cos_a, sin_a, cos_b, sin_b, cos_bt, sin_bt = _rope_tables(positions)
    mem_kv = _mem_kv(mem.reshape(batch * mem.shape[1], d).astype(F32), row(g_mem), w_mem_kv.astype(BF16))

    xt = x.reshape(t, d).astype(F32)
    for l in range(depth):
        p = _in_proj(xt, row(g_mix[l]), w_in_p[l], cos_a, sin_a)
        y_a = _win_attn(p, sink[l].astype(F32), batch)
        qt, k, vt = _mla_prep(p, row(g_q_lat[l]), row(g_kv_lat[l]), wq_p[l], wk_p[l], wv_p[l],
                              cos_b, sin_b, cos_bt, sin_bt, batch)
        y_b = _mla_attn(qt, k, vt)
        y_m = _mem_attn(p, mem_kv[l], batch)
        xt = _mix(xt, y_a, y_b, y_m, p, w_branch_b[l], w_out_b[l])
        i = l // 2
        if l % 2 == 0:
            xt = _ffn(xt, row(g_ffn[l]), wgd[i], wud[i], wdd[i])
            if l == depth - 1:
                xt = _final_norm(xt, row(g_final))
        else:
            top_i, top_w, x_tiles = _router(xt, row(g_ffn[l]), wr_t[i])
            buf_rows = _moe_rows_per_step(wge.shape[3] // TF_FFN) * (wge.shape[3] // TF_FFN)
            blk_e, gather_tok, scatter_row = _slot_tables(top_i, t, buf_rows)
            y = _moe_ffn(x_tiles, row(g_ffn[l]), wge[i], wue[i], wde[i], blk_e, gather_tok, scatter_row)
            xt = _moe_combine(xt, top_w.T, y, row(g_final), final_norm=(l == depth - 1))
    return xt.reshape(batch, s_len, d).astype(x.dtype)
```

```python
import functools

import jax
import jax.numpy as jnp
from jax import lax
from jax.experimental import pallas as pl
from jax.experimental.pallas import tpu as pltpu

F32 = jnp.float32
BF16 = jnp.bfloat16

ROPE_THETA = 10000.0
EPS = 1e-6
NEG_INF = -1e30
LOG2E = 1.4426950408889634

A_HEADS, A_KV_HEADS, A_HEAD_DIM, A_WINDOW = 8, 2, 64, 128
B_HEADS, B_Q_RANK, B_KV_RANK, B_NOPE, B_ROPE, B_V = 8, 256, 128, 64, 32, 64
M_HEADS, M_HEAD_DIM = 4, 128
N_EXPERTS, TOP_K = 8, 2
LANES = 128
SUBLANES = 8
BLK = 128

P_GATES = 0
P_QM = 3072
P_KA = 3584
P_CQ = 3840
P_CKV = 4096
P_WIDTH = 4224

TM_PROJ = 512
TM_PREP = 512
TM_MIX = 512
TM_MEM = 512
TM_FFN = 1024
TF_FFN = 512
TQ_WIN = 256
TQ_MLA = 512
TK_MLA = 512
TM_ROUTE = 1024
TM_EXP = 1024
TM_COMB = 512


def _rms(x, g):
    return x * lax.rsqrt(jnp.mean(x * x, axis=-1, keepdims=True) + EPS) * g


def _rot_tile(x, cos, sin_signed, first, shift):
    up = pltpu.roll(x, LANES - shift, 1)
    dn = pltpu.roll(x, shift, 1)
    return x * cos + jnp.where(first, up, dn) * sin_signed


def _rope_table_kernel(pos_ref, posr_ref, par_ref, invc_ref, ca_ref, sa_ref, cb_ref, sb_ref, cbt_ref, sbt_ref):
    pos = pos_ref[...]
    ang_a = pos * par_ref[0:1, :]
    ca_ref[...] = jnp.cos(ang_a)
    sa_ref[...] = jnp.sin(ang_a) * par_ref[1:2, :]
    ang_b = pos * par_ref[2:3, :]
    cb_ref[...] = jnp.cos(ang_b)
    sb_ref[...] = jnp.sin(ang_b) * par_ref[3:4, :]
    ang_t = invc_ref[...] * posr_ref[...]
    cbt_ref[...] = jnp.cos(ang_t)
    sbt_ref[...] = jnp.sin(ang_t)


def _rope_tables(positions):
    t = positions.size
    tm = 2048
    lane = jnp.arange(LANES)
    half_a = A_HEAD_DIM // 2
    inv_a = jnp.power(F32(ROPE_THETA), -jnp.arange(half_a, dtype=F32) * (2.0 / A_HEAD_DIM))
    inv_a = inv_a[lane % half_a]
    sgn_a = jnp.where((lane % A_HEAD_DIM) < half_a, -1.0, 1.0).astype(F32)
    half_b = B_ROPE // 2
    inv_b16 = jnp.power(F32(ROPE_THETA), -jnp.arange(half_b, dtype=F32) * (2.0 / B_ROPE))
    in_rope = (lane >= B_NOPE) & (lane < B_NOPE + B_ROPE)
    inv_b = jnp.where(in_rope, inv_b16[(lane - B_NOPE) % half_b], 0.0).astype(F32)
    sgn_b = jnp.where(in_rope, jnp.where(lane < B_NOPE + half_b, -1.0, 1.0), 0.0).astype(F32)
    par = jnp.zeros((8, LANES), F32).at[0].set(inv_a).at[1].set(sgn_a).at[2].set(inv_b).at[3].set(sgn_b)
    pos = positions.reshape(t, 1).astype(F32)
    tab = jax.ShapeDtypeStruct((t, LANES), F32)
    n_freq = half_a + half_b
    inv_col = jnp.concatenate([inv_a[:half_a], inv_b16]).reshape(n_freq, 1)
    tab_t = jax.ShapeDtypeStruct((n_freq, t), F32)
    return pl.pallas_call(
        _rope_table_kernel,
        grid=(t // tm,),
        in_specs=[pl.BlockSpec((tm, 1), lambda i: (i, 0)),
                  pl.BlockSpec((1, tm), lambda i: (0, i)),
                  pl.BlockSpec((8, LANES), lambda i: (0, 0)),
                  pl.BlockSpec((n_freq, 1), lambda i: (0, 0))],
        out_specs=[pl.BlockSpec((tm, LANES), lambda i: (i, 0))] * 4
        + [pl.BlockSpec((n_freq, tm), lambda i: (0, i))] * 2,
        out_shape=[tab] * 4 + [tab_t] * 2,
        name="rope_tables",
    )(pos, pos.reshape(1, t), par, inv_col)


def _in_proj_kernel(x_ref, g_ref, w_ref, wqt_ref, wvt_ref, cos_ref, sin_ref, cost_ref, sint_ref,
                    o_ref, qt_ref, vt_ref):
    h = _rms(x_ref[...], g_ref[...]).astype(BF16)
    half = A_HEAD_DIM // 2

    def proj(c0, c1):
        return jnp.dot(h, w_ref[:, c0:c1], preferred_element_type=F32)

    for c0 in range(P_GATES, P_QM, 512):
        o_ref[:, c0:c0 + 512] = proj(c0, c0 + 512).astype(BF16)
    o_ref[:, P_QM:P_KA] = (proj(P_QM, P_KA) * (M_HEAD_DIM ** -0.5)).astype(BF16)
    cos, sin = cos_ref[...], sin_ref[...]
    lane = lax.broadcasted_iota(jnp.int32, cos.shape, 1)
    ka = proj(P_KA, P_CQ)
    o_ref[:, P_KA:P_KA + LANES] = _rot_tile(ka[:, :LANES], cos, sin, (lane % A_HEAD_DIM) < half, half).astype(BF16)
    o_ref[:, P_KA + LANES:P_CQ] = ka[:, LANES:].astype(BF16)
    o_ref[:, P_CQ:P_WIDTH] = proj(P_CQ, P_WIDTH).astype(BF16)

    nt = (((1,), (1,)), ((), ()))
    ct, st = cost_ref[...], sint_ref[...]
    qscale = A_HEAD_DIM ** -0.5 * LOG2E
    qt = lax.dot_general(wqt_ref[...], h, nt, preferred_element_type=F32)
    for hd in range(A_HEADS):
        r0 = hd * A_HEAD_DIM
        x1, x2 = qt[r0:r0 + half], qt[r0 + half:r0 + A_HEAD_DIM]
        qt_ref[r0:r0 + half, :] = ((x1 * ct - x2 * st) * qscale).astype(BF16)
        qt_ref[r0 + half:r0 + A_HEAD_DIM, :] = ((x2 * ct + x1 * st) * qscale).astype(BF16)
    vt = lax.dot_general(wvt_ref[...], h, nt, preferred_element_type=F32)
    row = lax.broadcasted_iota(jnp.int32, (LANES - A_HEAD_DIM, vt.shape[1]), 0)
    ones_pad = (row == 0).astype(BF16)
    for g in range(A_KV_HEADS):
        vt_ref[g * LANES:g * LANES + A_HEAD_DIM, :] = vt[g * A_HEAD_DIM:(g + 1) * A_HEAD_DIM].astype(BF16)
        vt_ref[g * LANES + A_HEAD_DIM:(g + 1) * LANES, :] = ones_pad


def _in_proj(x, g, w, wqt, wvt, cos_a, sin_a, cos_t, sin_t):
    t, d = x.shape
    tm = TM_PROJ
    half = A_HEAD_DIM // 2
    qrows, vrows = A_HEADS * A_HEAD_DIM, A_KV_HEADS * LANES
    const = lambda i: (0, 0)
    return pl.pallas_call(
        _in_proj_kernel,
        grid=(t // tm,),
        in_specs=[pl.BlockSpec((tm, d), lambda i: (i, 0)),
                  pl.BlockSpec((1, d), const),
                  pl.BlockSpec((d, P_WIDTH), const),
                  pl.BlockSpec((qrows, d), const),
                  pl.BlockSpec((A_KV_HEADS * A_HEAD_DIM, d), const),
                  pl.BlockSpec((tm, LANES), lambda i: (i, 0)),
                  pl.BlockSpec((tm, LANES), lambda i: (i, 0)),
                  pl.BlockSpec((half, tm), lambda i: (0, i)),
                  pl.BlockSpec((half, tm), lambda i: (0, i))],
        out_specs=[pl.BlockSpec((tm, P_WIDTH), lambda i: (i, 0)),
                   pl.BlockSpec((qrows, tm), lambda i: (0, i)),
                   pl.BlockSpec((vrows, tm), lambda i: (0, i))],
        out_shape=[jax.ShapeDtypeStruct((t, P_WIDTH), BF16),
                   jax.ShapeDtypeStruct((qrows, t), BF16),
                   jax.ShapeDtypeStruct((vrows, t), BF16)],
        name="in_proj",
    )(x, g, w, wqt, wvt, cos_a, sin_a, cos_t, sin_t)


def _mla_prep_kernel(cq_ref, ckv_ref, kr_ref, gq_ref, gkv_ref, wqt_ref, wk_ref, wvt_ref,
                     cos_ref, sin_ref, cost_ref, sint_ref, qt_ref, k_ref, vt_ref):
    half = B_ROPE // 2
    nt = (((1,), (1,)), ((), ()))
    cq = _rms(cq_ref[...].astype(F32), gq_ref[...]).astype(BF16)
    ckv = _rms(ckv_ref[...].astype(F32), gkv_ref[...]).astype(BF16)

    cos, sin = cos_ref[...], sin_ref[...]
    lane = lax.broadcasted_iota(jnp.int32, cos.shape, 1)
    first = (lane >= B_NOPE) & (lane < B_NOPE + half)
    kr = _rot_tile(pltpu.roll(kr_ref[:, LANES:].astype(F32), B_NOPE, 1), cos, sin, first, half)
    k = jnp.dot(ckv, wk_ref[...], preferred_element_type=F32)
    for hd in range(B_HEADS):
        sl = slice(hd * LANES, (hd + 1) * LANES)
        k_ref[:, sl] = (k[:, sl] + kr).astype(BF16)

    ct, st = cost_ref[...], sint_ref[...]
    qscale = (B_NOPE + B_ROPE) ** -0.5 * LOG2E
    qt = lax.dot_general(wqt_ref[...], cq, nt, preferred_element_type=F32)
    vt = lax.dot_general(wvt_ref[...], ckv, nt, preferred_element_type=F32)
    row = lax.broadcasted_iota(jnp.int32, (LANES, vt.shape[1]), 0)
    ones_row = (row == B_V).astype(F32)
    for hd in range(B_HEADS):
        r0 = hd * LANES
        x1 = qt[r0 + B_NOPE:r0 + B_NOPE + half]
        x2 = qt[r0 + B_NOPE + half:r0 + B_NOPE + B_ROPE]
        qh = jnp.concatenate([qt[r0:r0 + B_NOPE], x1 * ct - x2 * st, x2 * ct + x1 * st,
                              qt[r0 + B_NOPE + B_ROPE:r0 + LANES]], axis=0)
        qt_ref[r0:r0 + LANES, :] = (qh * qscale).astype(BF16)
        vt_ref[r0:r0 + LANES, :] = (vt[r0:r0 + LANES] + ones_row).astype(BF16)


def _mla_prep(p, gq, gkv, wqt, wk, wvt, cos_b, sin_b, cos_bt, sin_bt, batch):
    t = p.shape[0]
    s_len = t // batch
    tm = TM_PREP
    nt = s_len // tm
    hw = B_HEADS * LANES
    half = B_ROPE // 2
    const = lambda b, i: (0, 0)
    tok = lambda b, i: (b * nt + i, 0)
    return pl.pallas_call(
        _mla_prep_kernel,
        grid=(batch, nt),
        in_specs=[pl.BlockSpec((tm, B_Q_RANK), lambda b, i: (b * nt + i, P_CQ // B_Q_RANK)),
                  pl.BlockSpec((tm, B_KV_RANK), lambda b, i: (b * nt + i, P_CKV // B_KV_RANK)),
                  pl.BlockSpec((tm, 2 * LANES), lambda b, i: (b * nt + i, P_KA // (2 * LANES))),
                  pl.BlockSpec((1, B_Q_RANK), const),
                  pl.BlockSpec((1, B_KV_RANK), const),
                  pl.BlockSpec((hw, B_Q_RANK), const),
                  pl.BlockSpec((B_KV_RANK, hw), const),
                  pl.BlockSpec((hw, B_KV_RANK), const),
                  pl.BlockSpec((tm, LANES), tok),
                  pl.BlockSpec((tm, LANES), tok),
                  pl.BlockSpec((half, tm), lambda b, i: (A_HEAD_DIM // 2 // half, b * nt + i)),
                  pl.BlockSpec((half, tm), lambda b, i: (A_HEAD_DIM // 2 // half, b * nt + i))],
        out_specs=[pl.BlockSpec((None, hw, tm), lambda b, i: (b, 0, i)),
                   pl.BlockSpec((None, tm, hw), lambda b, i: (b, i, 0)),
                   pl.BlockSpec((None, hw, tm), lambda b, i: (b, 0, i))],
        out_shape=[jax.ShapeDtypeStruct((batch, hw, s_len), BF16),
                   jax.ShapeDtypeStruct((batch, s_len, hw), BF16),
                   jax.ShapeDtypeStruct((batch, hw, s_len), BF16)],
        name="mla_prep",
    )(p, p, p, gq, gkv, wqt, wk, wvt, cos_b, sin_b, cos_bt, sin_bt)


def _mla_attn_kernel(qt_ref, k_ref, vt_ref, o_ref, m_sc, acc_sc, s0_sc, s1_sc, *, tk):
    n_chunks = k_ref.shape[0] // tk
    m_sc[...] = jnp.full(m_sc.shape, -jnp.inf, F32)
    acc_sc[...] = jnp.zeros(acc_sc.shape, F32)
    heads = [slice(hd * LANES, (hd + 1) * LANES) for hd in range(2)]

    def keys(j):
        return pl.ds(pl.multiple_of(j * tk, tk), tk)

    def scores(j, dst):
        for hd, sl in enumerate(heads):
            dst[hd] = jnp.dot(k_ref[keys(j), sl], qt_ref[sl, :], preferred_element_type=F32)

    def accumulate(j, src):
        for hd, sl in enumerate(heads):
            st = src[hd]
            m_old = m_sc[hd]
            m_new = jnp.maximum(m_old, jnp.max(st, axis=0, keepdims=True))
            pt = jnp.exp2(st - m_new).astype(BF16)
            acc_sc[hd] = acc_sc[hd] * jnp.exp2(m_old - m_new) + jnp.dot(
                vt_ref[sl, keys(j)], pt, preferred_element_type=F32)
            m_sc[hd] = m_new

    scores(0, s0_sc)

    def body(jj, carry):
        j = 2 * jj
        scores(j + 1, s1_sc)
        accumulate(j, s0_sc)
        scores(j + 2, s0_sc)
        accumulate(j + 1, s1_sc)
        return carry

    lax.fori_loop(0, n_chunks // 2 - 1, body, 0)
    scores(n_chunks - 1, s1_sc)
    accumulate(n_chunks - 2, s0_sc)
    accumulate(n_chunks - 1, s1_sc)
    outs = []
    for hd in range(2):
        acc = acc_sc[hd]
        outs.append(acc[:B_V] * (1.0 / acc[B_V:B_V + 1]))
    o_ref[...] = jnp.concatenate(outs, axis=0).T.astype(BF16)


def _mla_attn(qt, k, vt):
    batch, s_len, hw = k.shape
    tq = TQ_MLA
    out = pl.pallas_call(
        functools.partial(_mla_attn_kernel, tk=TK_MLA),
        grid=(batch, B_HEADS // 2, s_len // tq),
        in_specs=[pl.BlockSpec((None, 2 * LANES, tq), lambda b, p, i: (b, p, i)),
                  pl.BlockSpec((None, s_len, 2 * LANES), lambda b, p, i: (b, 0, p)),
                  pl.BlockSpec((None, 2 * LANES, s_len), lambda b, p, i: (b, p, 0))],
        out_specs=pl.BlockSpec((None, tq, LANES), lambda b, p, i: (b, i, p)),
        out_shape=jax.ShapeDtypeStruct((batch, s_len, B_HEADS * B_V), BF16),
        scratch_shapes=[pltpu.VMEM((2, 1, tq), F32), pltpu.VMEM((2, LANES, tq), F32),
                        pltpu.VMEM((2, TK_MLA, tq), F32), pltpu.VMEM((2, TK_MLA, tq), F32)],
        name="mla_attn",
    )(qt, k, vt)
    return out.reshape(batch * s_len, B_HEADS * B_V)


def _win_attn_kernel(sink_ref, qt_ref, kp_ref, kc_ref, kn_ref, vp_ref, vc_ref, vn_ref, o_ref, *, s_len):
    n = pl.program_id(1)
    grp = A_HEADS // A_KV_HEADS
    dh = A_HEAD_DIM
    tq = kc_ref.shape[0]
    kj = lax.broadcasted_iota(jnp.int32, (tq + 2 * BLK, tq), 0)
    qi = lax.broadcasted_iota(jnp.int32, (tq + 2 * BLK, tq), 1)
    k_abs = n * tq - BLK + kj
    ok = (jnp.abs(kj - BLK - qi) <= A_WINDOW) & (k_abs >= 0) & (k_abs < s_len)
    ok = jnp.concatenate([ok] * grp, axis=1)
    lane_head = lax.broadcasted_iota(jnp.int32, (1, grp * tq), 1) // tq
    kband = jnp.concatenate([kp_ref[:, :LANES], kc_ref[:, :LANES], kn_ref[:, :LANES]], axis=0)
    zeros = jnp.zeros((dh, tq), BF16)
    heads = []
    for g in range(A_KV_HEADS):
        tiles = []
        for a in range(grp):
            qh = qt_ref[(g * grp + a) * dh:(g * grp + a + 1) * dh, :]
            tiles.append(jnp.concatenate([zeros] * g + [qh] + [zeros] * (A_KV_HEADS - 1 - g), axis=0))
        qg = jnp.concatenate(tiles, axis=1)
        st = jnp.where(ok, jnp.dot(kband, qg, preferred_element_type=F32), NEG_INF)
        sk = jnp.zeros((1, grp * tq), F32)
        for a in range(grp):
            sk = jnp.where(lane_head == a, sink_ref[g * grp + a] * LOG2E, sk)
        m = jnp.maximum(jnp.max(st, axis=0, keepdims=True), sk)
        pt = jnp.exp2(st - m).astype(BF16)
        rows = slice(g * LANES, (g + 1) * LANES)
        vband = jnp.concatenate([vp_ref[rows, :], vc_ref[rows, :], vn_ref[rows, :]], axis=1)
        ot = jnp.dot(vband, pt, preferred_element_type=F32)
        o = ot[:dh] * (1.0 / (ot[dh:dh + 1] + jnp.exp2(sk - m)))
        heads.extend(o[:, a * tq:(a + 1) * tq] for a in range(grp))
    for pair in range(A_HEADS // 2):
        o_ref[:, pair * LANES:(pair + 1) * LANES] = jnp.concatenate(
            [heads[2 * pair], heads[2 * pair + 1]], axis=0).T.astype(BF16)


def _win_attn(p, qt, vt, sink, batch):
    t = p.shape[0]
    s_len = t // batch
    tq = TQ_WIN
    nq, nb, r = s_len // tq, s_len // BLK, tq // BLK
    p3 = p.reshape(batch, s_len, P_WIDTH)
    kcol = P_KA // (2 * LANES)
    lo = lambda n: jnp.maximum(n * r - 1, 0)
    hi = lambda n: jnp.minimum((n + 1) * r, nb - 1)
    kedge = lambda f: pl.BlockSpec((None, BLK, 2 * LANES), lambda b, n, s: (b, f(n), kcol))
    vedge = lambda f: pl.BlockSpec((A_KV_HEADS * LANES, BLK), lambda b, n, s: (0, b * nb + f(n)))
    out = pl.pallas_call(
        functools.partial(_win_attn_kernel, s_len=s_len),
        grid_spec=pltpu.PrefetchScalarGridSpec(
            num_scalar_prefetch=1,
            grid=(batch, nq),
            in_specs=[pl.BlockSpec((A_HEADS * A_HEAD_DIM, tq), lambda b, n, s: (0, b * nq + n)),
                      kedge(lo), pl.BlockSpec((None, tq, 2 * LANES), lambda b, n, s: (b, n, kcol)), kedge(hi),
                      vedge(lo), pl.BlockSpec((A_KV_HEADS * LANES, tq), lambda b, n, s: (0, b * nq + n)), vedge(hi)],
            out_specs=pl.BlockSpec((None, tq, A_HEADS * A_HEAD_DIM), lambda b, n, s: (b, n, 0)),
        ),
        out_shape=jax.ShapeDtypeStruct((batch, s_len, A_HEADS * A_HEAD_DIM), BF16),
        name="win_attn",
    )(sink, qt, p3, p3, p3, vt, vt, vt)
    return out.reshape(t, A_HEADS * A_HEAD_DIM)


def _mem_kv_kernel(mem_ref, g_ref, w_ref, o_ref):
    mem_n = _rms(mem_ref[...], g_ref[...]).astype(BF16)
    o_ref[...] = jnp.dot(mem_n, w_ref[...], preferred_element_type=F32).astype(BF16)


def _mem_kv(mem2, g_mem, w_mem_kv):
    depth, d, width = w_mem_kv.shape
    rows = mem2.shape[0]
    return pl.pallas_call(
        _mem_kv_kernel,
        grid=(depth,),
        in_specs=[pl.BlockSpec((rows, d), lambda l: (0, 0)),
                  pl.BlockSpec((1, d), lambda l: (0, 0)),
                  pl.BlockSpec((None, d, width), lambda l: (l, 0, 0))],
        out_specs=pl.BlockSpec((None, rows, width), lambda l: (l, 0, 0)),
        out_shape=jax.ShapeDtypeStruct((depth, rows, width), BF16),
        name="mem_kv",
    )(mem2, g_mem, w_mem_kv)


def _mem_attn_kernel(q_ref, kv_ref, o_ref):
    dh = M_HEAD_DIM
    for hd in range(M_HEADS):
        q = q_ref[:, hd * dh:(hd + 1) * dh]
        k = kv_ref[:, hd * dh:(hd + 1) * dh]
        v = kv_ref[:, (M_HEADS + hd) * dh:(M_HEADS + hd + 1) * dh]
        sc = lax.dot_general(q, k, (((1,), (1,)), ((), ())), preferred_element_type=F32)
        e = jnp.exp(sc - jnp.max(sc, axis=-1, keepdims=True))
        pr = (e * (1.0 / jnp.sum(e, axis=-1, keepdims=True))).astype(BF16)
        o_ref[:, hd * dh:(hd + 1) * dh] = jnp.dot(pr, v, preferred_element_type=F32).astype(BF16)


def _mem_attn(p, mem_kv_l, batch):
    t = p.shape[0]
    s_len = t // batch
    tm = TM_MEM
    n_mem = mem_kv_l.shape[0] // batch
    width = M_HEADS * M_HEAD_DIM
    p3 = p.reshape(batch, s_len, P_WIDTH)
    kv3 = mem_kv_l.reshape(batch, n_mem, 2 * width)
    out = pl.pallas_call(
        _mem_attn_kernel,
        grid=(batch, s_len // tm),
        in_specs=[pl.BlockSpec((None, tm, width), lambda b, i: (b, i, P_QM // width)),
                  pl.BlockSpec((None, n_mem, 2 * width), lambda b, i: (b, 0, 0))],
        out_specs=pl.BlockSpec((None, tm, width), lambda b, i: (b, i, 0)),
        out_shape=jax.ShapeDtypeStruct((batch, s_len, width), BF16),
        name="mem_attn",
    )(p3, kv3)
    return out.reshape(t, width)


def _mix_kernel(x_ref, ya_ref, yb_ref, ym_ref, g_ref, wb_ref, wo_ref, o_ref):
    d = x_ref.shape[1]
    mixed = None
    for i, y_ref in enumerate((ya_ref, yb_ref, ym_ref)):
        gate = jax.nn.sigmoid(g_ref[:, i * d:(i + 1) * d].astype(F32))
        term = gate * jnp.dot(y_ref[...], wb_ref[i], preferred_element_type=F32)
        mixed = term if mixed is None else mixed + term
    o_ref[...] = x_ref[...] + jnp.dot(mixed.astype(BF16), wo_ref[...], preferred_element_type=F32)


def _mix(x, ya, yb, ym, p, wb, wo):
    t, d = x.shape
    tm = TM_MIX
    bw = ya.shape[1]
    row = lambda i: (i, 0)
    return pl.pallas_call(
        _mix_kernel,
        grid=(t // tm,),
        in_specs=[pl.BlockSpec((tm, d), row),
                  pl.BlockSpec((tm, bw), row), pl.BlockSpec((tm, bw), row), pl.BlockSpec((tm, bw), row),
                  pl.BlockSpec((tm, 3 * d), lambda i: (i, P_GATES)),
                  pl.BlockSpec((3, bw, d), lambda i: (0, 0, 0)),
                  pl.BlockSpec((d, d), lambda i: (0, 0))],
        out_specs=pl.BlockSpec((tm, d), row),
        out_shape=jax.ShapeDtypeStruct((t, d), F32),
        name="mix_out",
    )(x, ya, yb, ym, p, wb, wo)


def _swiglu_step(h, wg, wu, wd):
    gate = jnp.dot(h, wg, preferred_element_type=F32)
    up = jnp.dot(h, wu, preferred_element_type=F32)
    act = (gate * jax.nn.sigmoid(gate) * up).astype(BF16)
    return jnp.dot(act, wd, preferred_element_type=F32)


def _ffn_kernel(x_ref, g_ref, wg_ref, wu_ref, wd_ref, o_ref, h_sc, acc_sc):
    j = pl.program_id(1)

    @pl.when(j == 0)
    def _():
        h_sc[...] = _rms(x_ref[...], g_ref[...]).astype(BF16)
        acc_sc[...] = jnp.zeros(acc_sc.shape, F32)

    acc_sc[...] += _swiglu_step(h_sc[...], wg_ref[...], wu_ref[...], wd_ref[...])

    @pl.when(j == pl.num_programs(1) - 1)
    def _():
        o_ref[...] = x_ref[...] + acc_sc[...]


def _ffn(x, g, wg, wu, wd):
    t, d = x.shape
    f = wg.shape[1]
    tm, tf = TM_FFN, TF_FFN
    return pl.pallas_call(
        _ffn_kernel,
        grid=(t // tm, f // tf),
        in_specs=[pl.BlockSpec((tm, d), lambda i, j: (i, 0)),
                  pl.BlockSpec((1, d), lambda i, j: (0, 0)),
                  pl.BlockSpec((d, tf), lambda i, j: (0, j)),
                  pl.BlockSpec((d, tf), lambda i, j: (0, j)),
                  pl.BlockSpec((tf, d), lambda i, j: (j, 0))],
        out_specs=pl.BlockSpec((tm, d), lambda i, j: (i, 0)),
        out_shape=jax.ShapeDtypeStruct((t, d), F32),
        scratch_shapes=[pltpu.VMEM((tm, d), BF16), pltpu.VMEM((tm, d), F32)],
        name="ffn_dense",
    )(x, g, wg, wu, wd)


def _router_kernel(x_ref, g_ref, wr_ref, idx_ref, wgt_ref, xt_ref):
    x = x_ref[...]
    for s in range(x.shape[1] // LANES):
        xt_ref[pl.ds(s, x.shape[0], stride=SUBLANES), :] = x[:, s * LANES:(s + 1) * LANES]
    h = _rms(x, g_ref[...])
    logits = lax.dot_general(wr_ref[...], h, (((1,), (1,)), ((), ())),
                             precision=lax.Precision.HIGHEST, preferred_element_type=F32)
    e_id = lax.broadcasted_iota(jnp.int32, logits.shape, 0)
    m1 = jnp.max(logits, axis=0, keepdims=True)
    i1 = jnp.min(jnp.where(logits == m1, e_id, N_EXPERTS), axis=0, keepdims=True)
    rest = jnp.where(e_id == i1, -jnp.inf, logits)
    m2 = jnp.max(rest, axis=0, keepdims=True)
    i2 = jnp.min(jnp.where(rest == m2, e_id, N_EXPERTS), axis=0, keepdims=True)
    e2 = jnp.exp(m2 - m1)
    w1 = 1.0 / (1.0 + e2)
    idx_ref[...] = jnp.concatenate([i1, i2], axis=0)
    wgt_ref[...] = jnp.concatenate([w1, e2 * w1], axis=0)


def _router(x, g, wr_t):
    t, d = x.shape
    tm = TM_ROUTE
    return pl.pallas_call(
        _router_kernel,
        grid=(t // tm,),
        in_specs=[pl.BlockSpec((tm, d), lambda i: (i, 0)),
                  pl.BlockSpec((1, d), lambda i: (0, 0)),
                  pl.BlockSpec((N_EXPERTS, d), lambda i: (0, 0))],
        out_specs=[pl.BlockSpec((TOP_K, tm), lambda i: (0, i)),
                   pl.BlockSpec((TOP_K, tm), lambda i: (0, i)),
                   pl.BlockSpec((tm * SUBLANES, LANES), lambda i: (i, 0))],
        out_shape=[jax.ShapeDtypeStruct((TOP_K, t), jnp.int32),
                   jax.ShapeDtypeStruct((TOP_K, t), F32),
                   jax.ShapeDtypeStruct((t * SUBLANES, LANES), F32)],
        name="moe_router",
    )(x, g, wr_t)


def _moe_rows_per_step(n_steps):
    return -(-TM_EXP // (n_steps * SUBLANES)) * SUBLANES


def _slot_tables(top_i, n_tok, buf_rows):
    n_asg = n_tok * TOP_K
    flat_e = top_i.T.reshape(n_asg)
    onehot = (flat_e[:, None] == jnp.arange(N_EXPERTS, dtype=jnp.int32)[None, :]).astype(jnp.int32)
    csum = jnp.cumsum(onehot, axis=0)
    rank = jnp.take_along_axis(csum, flat_e[:, None], axis=1)[:, 0] - 1
    counts = csum[-1]
    padded = (counts + TM_EXP - 1) // TM_EXP * TM_EXP
    pad_end = jnp.cumsum(padded)
    dest = (pad_end - padded)[flat_e] + rank
    n_slots = n_asg + N_EXPERTS * TM_EXP
    n_blk = n_slots // TM_EXP
    extra = buf_rows - TM_EXP
    slot_asg1 = jnp.zeros((n_slots,), jnp.int32).at[dest].set(jnp.arange(1, n_asg + 1, dtype=jnp.int32))
    is_pad = slot_asg1 == 0
    slot_tok = jnp.where(is_pad, 0, (slot_asg1 - 1) // TOP_K)
    slot_row = jnp.where(is_pad, n_asg - 1 + jnp.cumsum(is_pad.astype(jnp.int32)),
                         ((slot_asg1 - 1) % TOP_K) * n_tok + slot_tok)
    gather_tok = jnp.concatenate([slot_tok.reshape(n_blk, TM_EXP), jnp.zeros((n_blk, extra), jnp.int32)], axis=1)
    spare = n_slots + jnp.arange((n_blk + 1) * buf_rows - n_slots, dtype=jnp.int32)
    lead, over = spare[:TM_EXP], spare[TM_EXP:].reshape(n_blk + 1, extra)
    scatter_row = jnp.concatenate(
        [jnp.concatenate([lead[None], slot_row.reshape(n_blk, TM_EXP)], axis=0), over], axis=1)
    blk_e = jnp.minimum(jnp.searchsorted(pad_end, jnp.arange(n_blk, dtype=jnp.int32) * TM_EXP, side='right'),
                        N_EXPERTS - 1).astype(jnp.int32)
    return blk_e, gather_tok.reshape(-1), scatter_row.reshape(-1).astype(jnp.int32)


def _moe_ffn_kernel(blk_e_ref, gtok_ref, srow_ref, x_hbm, g_ref, wg_ref, wu_ref, wd_ref, o_hbm,
                    xg_sc, ys_sc, h_sc, acc_sc, gsem, ssem, *, rows_per_step):
    i, j = pl.program_id(0), pl.program_id(1)
    n_blk, n_steps = pl.num_programs(0), pl.num_programs(1)
    rows, d = h_sc.shape
    tr = d // LANES
    buf_rows = xg_sc.shape[1] // tr
    slot = i % 2
    other = 1 - slot

    def tile(ref, k):
        return ref.at[pl.ds(pl.multiple_of(k * tr, tr), tr), :]

    def gathered(s):
        return pltpu.make_async_copy(x_hbm.at[pl.ds(0, buf_rows * tr), :], xg_sc.at[s], gsem.at[s])

    def scattered(s):
        return pltpu.make_async_copy(ys_sc.at[s], o_hbm.at[pl.ds(0, buf_rows * tr), :], ssem.at[s])

    def get_row(s, k, entry):
        return pltpu.make_async_copy(tile(x_hbm, gtok_ref[entry]), tile(xg_sc.at[s], k), gsem.at[s])

    def put_row(s, k, entry):
        return pltpu.make_async_copy(tile(ys_sc.at[s], k), tile(o_hbm, srow_ref[entry]), ssem.at[s])

    def slabs(ref, s):
        return [ref[s, pl.ds(c, rows, stride=tr), :] for c in range(tr)]

    @pl.when((i == 0) & (j == 0))
    def _():
        ys_sc[...] = jnp.zeros(ys_sc.shape, F32)

        def start(k, c):
            get_row(0, k, k).start()
            return c

        lax.fori_loop(0, buf_rows, start, 0)

    @pl.when(j == 0)
    def _():
        gathered(slot).wait()
        for s in range(2):
            @pl.when(slot == s)
            def _():
                xs = slabs(xg_sc, s)
                ms = sum(jnp.sum(x * x, axis=-1, keepdims=True) for x in xs) * (1.0 / d)
                inv = lax.rsqrt(ms + EPS)
                for c, x in enumerate(xs):
                    cols = slice(c * LANES, (c + 1) * LANES)
                    h_sc[:, cols] = (x * inv * g_ref[:, cols]).astype(BF16)

        acc_sc[...] = jnp.zeros(acc_sc.shape, F32)

    nxt = jnp.minimum(i + 1, n_blk - 1)
    for r in range(rows_per_step):
        k = j * rows_per_step + r
        get_row(other, k, nxt * buf_rows + k).start()
        put_row(other, k, i * buf_rows + k).start()

    acc_sc[...] += _swiglu_step(h_sc[...], wg_ref[...], wu_ref[...], wd_ref[...])

    @pl.when(j == n_steps - 1)
    def _():
        @pl.when(i > 0)
        def _():
            scattered(slot).wait()

        for s in range(2):
            @pl.when(slot == s)
            def _():
                for c in range(tr):
                    ys_sc[s, pl.ds(c, rows, stride=tr), :] = acc_sc[:, c * LANES:(c + 1) * LANES]

    @pl.when((i == n_blk - 1) & (j == n_steps - 1))
    def _():
        def start(k, c):
            put_row(slot, k, n_blk * buf_rows + k).start()
            return c

        lax.fori_loop(0, buf_rows, start, 0)
        scattered(other).wait()
        scattered(slot).wait()
        gathered(other).wait()


def _moe_ffn(x_tiles, g, wg, wu, wd, blk_e, gather_tok, scatter_row):
    d, f = wg.shape[1], wg.shape[2]
    tr = d // LANES
    assert tr % SUBLANES == 0
    tm, tf = TM_EXP, TF_FFN
    n_steps = f // tf
    rows_per_step = _moe_rows_per_step(n_steps)
    buf_rows = rows_per_step * n_steps
    n_blk = gather_tok.shape[0] // buf_rows
    return pl.pallas_call(
        functools.partial(_moe_ffn_kernel, rows_per_step=rows_per_step),
        grid_spec=pltpu.PrefetchScalarGridSpec(
            num_scalar_prefetch=3,
            grid=(n_blk, n_steps),
            in_specs=[pl.BlockSpec(memory_space=pl.ANY),
                      pl.BlockSpec((1, d), lambda i, j, be, gt, sr: (0, 0)),
                      pl.BlockSpec((None, d, tf), lambda i, j, be, gt, sr: (be[i], 0, j)),
                      pl.BlockSpec((None, d, tf), lambda i, j, be, gt, sr: (be[i], 0, j)),
                      pl.BlockSpec((None, tf, d), lambda i, j, be, gt, sr: (be[i], j, 0))],
            out_specs=pl.BlockSpec(memory_space=pl.ANY),
            scratch_shapes=[pltpu.VMEM((2, buf_rows * tr, LANES), F32), pltpu.VMEM((2, buf_rows * tr, LANES), F32),
                            pltpu.VMEM((tm, d), BF16), pltpu.VMEM((tm, d), F32),
                            pltpu.SemaphoreType.DMA((2,)), pltpu.SemaphoreType.DMA((2,))],
        ),
        out_shape=jax.ShapeDtypeStruct((scatter_row.shape[0] * tr, LANES), F32),
        name="moe_ffn",
    )(blk_e, gather_tok, scatter_row, x_tiles, g, wg, wu, wd)


def _moe_combine_kernel(x_ref, w_ref, y0_ref, y1_ref, gf_ref, o_ref, *, final_norm):
    tm, d = x_ref.shape
    tr = d // LANES
    w0, w1 = w_ref[:, 0:1], w_ref[:, 1:2]
    outs = []
    for c in range(tr):
        rows = pl.ds(c, tm, stride=tr)
        outs.append(x_ref[:, c * LANES:(c + 1) * LANES] + w0 * y0_ref[rows, :] + w1 * y1_ref[rows, :])
    out = jnp.concatenate(outs, axis=1)
    if final_norm:
        out = _rms(out, gf_ref[...])
    o_ref[...] = out


def _moe_combine(x, top_w, y, g_final, final_norm):
    t, d = x.shape
    tr = d // LANES
    tm = TM_COMB
    nt = t // tm
    return pl.pallas_call(
        functools.partial(_moe_combine_kernel, final_norm=final_norm),
        grid=(nt,),
        in_specs=[pl.BlockSpec((tm, d), lambda i: (i, 0)),
                  pl.BlockSpec((tm, TOP_K), lambda i: (i, 0)),
                  pl.BlockSpec((tm * tr, LANES), lambda i: (i, 0)),
                  pl.BlockSpec((tm * tr, LANES), lambda i: (nt + i, 0)),
                  pl.BlockSpec((1, d), lambda i: (0, 0))],
        out_specs=pl.BlockSpec((tm, d), lambda i: (i, 0)),
        out_shape=jax.ShapeDtypeStruct((t, d), F32),
        name="moe_combine",
    )(x, top_w, y, y, g_final)


def _final_norm_kernel(x_ref, g_ref, o_ref):
    o_ref[...] = _rms(x_ref[...], g_ref[...])


def _final_norm(x, g):
    t, d = x.shape
    tm = 1024
    return pl.pallas_call(
        _final_norm_kernel,
        grid=(t // tm,),
        in_specs=[pl.BlockSpec((tm, d), lambda i: (i, 0)), pl.BlockSpec((1, d), lambda i: (0, 0))],
        out_specs=pl.BlockSpec((tm, d), lambda i: (i, 0)),
        out_shape=jax.ShapeDtypeStruct((t, d), F32),
        name="final_norm",
    )(x, g)


def _relayout_w_in(w_in):
    qa, ka, va, cq, ckv, kr, qm, gates = jnp.split(
        w_in, [512, 640, 768, 1024, 1152, 1184, 1696], axis=-1)
    pad = jnp.zeros(w_in.shape[:2] + (LANES - B_ROPE,), w_in.dtype)
    w_tok = jnp.concatenate([gates, qm, ka, kr, pad, cq, ckv], axis=-1).astype(BF16)
    return w_tok, jnp.swapaxes(qa, 1, 2).astype(BF16), jnp.swapaxes(va, 1, 2).astype(BF16)


def _relayout_latent(w_uq, w_ukv):
    depth = w_uq.shape[0]
    q = w_uq.reshape(depth, B_Q_RANK, B_HEADS, B_NOPE + B_ROPE)
    q = jnp.pad(q, ((0, 0), (0, 0), (0, 0), (0, LANES - B_NOPE - B_ROPE)))
    kv = w_ukv.reshape(depth, B_KV_RANK, B_HEADS, B_NOPE + B_V)
    k = jnp.pad(kv[..., :B_NOPE], ((0, 0), (0, 0), (0, 0), (0, LANES - B_NOPE)))
    v = jnp.pad(kv[..., B_NOPE:], ((0, 0), (0, 0), (0, 0), (0, LANES - B_V)))
    flat = lambda a, r: a.reshape(depth, r, B_HEADS * LANES).astype(BF16)
    return (jnp.swapaxes(flat(q, B_Q_RANK), 1, 2), flat(k, B_KV_RANK),
            jnp.swapaxes(flat(v, B_KV_RANK), 1, 2))


def kernel(x, mem, positions, g_mix, w_in, g_q_lat, w_uq, g_kv_lat, w_ukv, sink, w_mem_kv, w_branch, w_out,
           g_ffn, w_gate_dense, w_up_dense, w_down_dense, w_router, w_gate_exp, w_up_exp, w_down_exp,
           g_mem, g_final):
    batch, s_len, d = x.shape
    depth = w_in.shape[0]
    t = batch * s_len
    row = lambda a: a.reshape(1, -1).astype(F32)

    w_in_p, wqa_t, wva_t = _relayout_w_in(w_in)
    wq_p, wk_p, wv_p = _relayout_latent(w_uq, w_ukv)
    w_branch_b, w_out_b = w_branch.astype(BF16), w_out.astype(BF16)
    wgd, wud, wdd = w_gate_dense.astype(BF16), w_up_dense.astype(BF16), w_down_dense.astype(BF16)
    wge, wue, wde = w_gate_exp.astype(BF16), w_up_exp.astype(BF16), w_down_exp.astype(BF16)
    wr_t = jnp.swapaxes(w_router, 1, 2).astype(F32)

    cos_a, sin_a, cos_b, sin_b, cos_t, sin_t = _rope_tables(positions)
    mem_kv = _mem_kv(mem.reshape(batch * mem.shape[1], d).astype(F32), row(g_mem), w_mem_kv.astype(BF16))

    xt = x.reshape(t, d).astype(F32)
    for l in range(depth):
        p, qa_t, va_t = _in_proj(xt, row(g_mix[l]), w_in_p[l], wqa_t[l], wva_t[l], cos_a, sin_a, cos_t, sin_t)
        y_a = _win_attn(p, qa_t, va_t, sink[l].astype(F32), batch)
        qt, k, vt = _mla_prep(p, row(g_q_lat[l]), row(g_kv_lat[l]), wq_p[l], wk_p[l], wv_p[l],
                              cos_b, sin_b, cos_t, sin_t, batch)
        y_b = _mla_attn(qt, k, vt)
        y_m = _mem_attn(p, mem_kv[l], batch)
        xt = _mix(xt, y_a, y_b, y_m, p, w_branch_b[l], w_out_b[l])
        i = l // 2
        if l % 2 == 0:
            xt = _ffn(xt, row(g_ffn[l]), wgd[i], wud[i], wdd[i])
            if l == depth - 1:
                xt = _final_norm(xt, row(g_final))
        else:
            top_i, top_w, x_tiles = _router(xt, row(g_ffn[l]), wr_t[i])
            buf_rows = _moe_rows_per_step(wge.shape[3] // TF_FFN) * (wge.shape[3] // TF_FFN)
            blk_e, gather_tok, scatter_row = _slot_tables(top_i, t, buf_rows)
            y = _moe_ffn(x_tiles, row(g_ffn[l]), wge[i], wue[i], wde[i], blk_e, gather_tok, scatter_row)
            xt = _moe_combine(xt, top_w.T, y, row(g_final), final_norm=(l == depth - 1))
    return xt.reshape(batch, s_len, d).astype(x.dtype)
```

```python
import functools

import jax
import jax.numpy as jnp
from jax import lax
from jax.experimental import pallas as pl
from jax.experimental.pallas import tpu as pltpu

F32 = jnp.float32
BF16 = jnp.bfloat16

ROPE_THETA = 10000.0
EPS = 1e-6
NEG_INF = -1e30
LOG2E = 1.4426950408889634

A_HEADS, A_KV_HEADS, A_HEAD_DIM, A_WINDOW = 8, 2, 64, 128
B_HEADS, B_Q_RANK, B_KV_RANK, B_NOPE, B_ROPE, B_V = 8, 256, 128, 64, 32, 64
M_HEADS, M_HEAD_DIM = 4, 128
N_EXPERTS, TOP_K = 8, 2
LANES = 128
SUBLANES = 8
BLK = 128

P_GATES = 0
P_QM = 3072
P_KA = 3584
P_CQ = 3840
P_CKV = 4096
P_WIDTH = 4224

TM_PROJ = 512
TM_PREP = 512
TM_MIX = 512
TM_MEM = 512
TM_FFN = 1024
TF_FFN = 512
TQ_WIN = 256
TQ_MLA = 512
TK_MLA = 512
TM_ROUTE = 1024
TM_EXP = 1024
TM_COMB = 512


def _rms(x, g):
    return x * lax.rsqrt(jnp.mean(x * x, axis=-1, keepdims=True) + EPS) * g


def _rot_tile(x, cos, sin_signed, first, shift):
    up = pltpu.roll(x, LANES - shift, 1)
    dn = pltpu.roll(x, shift, 1)
    return x * cos + jnp.where(first, up, dn) * sin_signed


def _rope_table_kernel(pos_ref, posr_ref, par_ref, invc_ref, ca_ref, sa_ref, cb_ref, sb_ref, cbt_ref, sbt_ref):
    pos = pos_ref[...]
    ang_a = pos * par_ref[0:1, :]
    ca_ref[...] = jnp.cos(ang_a)
    sa_ref[...] = jnp.sin(ang_a) * par_ref[1:2, :]
    ang_b = pos * par_ref[2:3, :]
    cb_ref[...] = jnp.cos(ang_b)
    sb_ref[...] = jnp.sin(ang_b) * par_ref[3:4, :]
    ang_t = invc_ref[...] * posr_ref[...]
    cbt_ref[...] = jnp.cos(ang_t)
    sbt_ref[...] = jnp.sin(ang_t)


def _rope_tables(positions):
    t = positions.size
    tm = 2048
    lane = jnp.arange(LANES)
    half_a = A_HEAD_DIM // 2
    inv_a = jnp.power(F32(ROPE_THETA), -jnp.arange(half_a, dtype=F32) * (2.0 / A_HEAD_DIM))
    inv_a = inv_a[lane % half_a]
    sgn_a = jnp.where((lane % A_HEAD_DIM) < half_a, -1.0, 1.0).astype(F32)
    half_b = B_ROPE // 2
    inv_b16 = jnp.power(F32(ROPE_THETA), -jnp.arange(half_b, dtype=F32) * (2.0 / B_ROPE))
    in_rope = (lane >= B_NOPE) & (lane < B_NOPE + B_ROPE)
    inv_b = jnp.where(in_rope, inv_b16[(lane - B_NOPE) % half_b], 0.0).astype(F32)
    sgn_b = jnp.where(in_rope, jnp.where(lane < B_NOPE + half_b, -1.0, 1.0), 0.0).astype(F32)
    par = jnp.zeros((8, LANES), F32).at[0].set(inv_a).at[1].set(sgn_a).at[2].set(inv_b).at[3].set(sgn_b)
    pos = positions.reshape(t, 1).astype(F32)
    tab = jax.ShapeDtypeStruct((t, LANES), F32)
    n_freq = half_a + half_b
    inv_col = jnp.concatenate([inv_a[:half_a], inv_b16]).reshape(n_freq, 1)
    tab_t = jax.ShapeDtypeStruct((n_freq, t), F32)
    return pl.pallas_call(
        _rope_table_kernel,
        grid=(t // tm,),
        in_specs=[pl.BlockSpec((tm, 1), lambda i: (i, 0)),
                  pl.BlockSpec((1, tm), lambda i: (0, i)),
                  pl.BlockSpec((8, LANES), lambda i: (0, 0)),
                  pl.BlockSpec((n_freq, 1), lambda i: (0, 0))],
        out_specs=[pl.BlockSpec((tm, LANES), lambda i: (i, 0))] * 4
        + [pl.BlockSpec((n_freq, tm), lambda i: (0, i))] * 2,
        out_shape=[tab] * 4 + [tab_t] * 2,
        name="rope_tables",
    )(pos, pos.reshape(1, t), par, inv_col)


def _in_proj_kernel(x_ref, g_ref, w_ref, wqt_ref, wvt_ref, cos_ref, sin_ref, cost_ref, sint_ref,
                    o_ref, qt_ref, vt_ref):
    h = _rms(x_ref[...], g_ref[...]).astype(BF16)
    half = A_HEAD_DIM // 2

    def proj(c0, c1):
        return jnp.dot(h, w_ref[:, c0:c1], preferred_element_type=F32)

    for c0 in range(P_GATES, P_QM, 512):
        o_ref[:, c0:c0 + 512] = proj(c0, c0 + 512).astype(BF16)
    o_ref[:, P_QM:P_KA] = (proj(P_QM, P_KA) * (M_HEAD_DIM ** -0.5)).astype(BF16)
    cos, sin = cos_ref[...], sin_ref[...]
    lane = lax.broadcasted_iota(jnp.int32, cos.shape, 1)
    ka = proj(P_KA, P_CQ)
    o_ref[:, P_KA:P_KA + LANES] = _rot_tile(ka[:, :LANES], cos, sin, (lane % A_HEAD_DIM) < half, half).astype(BF16)
    o_ref[:, P_KA + LANES:P_CQ] = ka[:, LANES:].astype(BF16)
    o_ref[:, P_CQ:P_WIDTH] = proj(P_CQ, P_WIDTH).astype(BF16)

    nt = (((1,), (1,)), ((), ()))
    ct, st = cost_ref[...], sint_ref[...]
    qscale = A_HEAD_DIM ** -0.5 * LOG2E
    qt = lax.dot_general(wqt_ref[...], h, nt, preferred_element_type=F32)
    for hd in range(A_HEADS):
        r0 = hd * A_HEAD_DIM
        x1, x2 = qt[r0:r0 + half], qt[r0 + half:r0 + A_HEAD_DIM]
        qt_ref[r0:r0 + half, :] = ((x1 * ct - x2 * st) * qscale).astype(BF16)
        qt_ref[r0 + half:r0 + A_HEAD_DIM, :] = ((x2 * ct + x1 * st) * qscale).astype(BF16)
    vt = lax.dot_general(wvt_ref[...], h, nt, preferred_element_type=F32)
    row = lax.broadcasted_iota(jnp.int32, (LANES - A_HEAD_DIM, vt.shape[1]), 0)
    ones_pad = (row == 0).astype(BF16)
    for g in range(A_KV_HEADS):
        vt_ref[g * LANES:g * LANES + A_HEAD_DIM, :] = vt[g * A_HEAD_DIM:(g + 1) * A_HEAD_DIM].astype(BF16)
        vt_ref[g * LANES + A_HEAD_DIM:(g + 1) * LANES, :] = ones_pad


def _in_proj(x, g, w, wqt, wvt, cos_a, sin_a, cos_t, sin_t):
    t, d = x.shape
    tm = TM_PROJ
    half = A_HEAD_DIM // 2
    qrows, vrows = A_HEADS * A_HEAD_DIM, A_KV_HEADS * LANES
    const = lambda i: (0, 0)
    return pl.pallas_call(
        _in_proj_kernel,
        grid=(t // tm,),
        in_specs=[pl.BlockSpec((tm, d), lambda i: (i, 0)),
                  pl.BlockSpec((1, d), const),
                  pl.BlockSpec((d, P_WIDTH), const),
                  pl.BlockSpec((qrows, d), const),
                  pl.BlockSpec((A_KV_HEADS * A_HEAD_DIM, d), const),
                  pl.BlockSpec((tm, LANES), lambda i: (i, 0)),
                  pl.BlockSpec((tm, LANES), lambda i: (i, 0)),
                  pl.BlockSpec((half, tm), lambda i: (0, i)),
                  pl.BlockSpec((half, tm), lambda i: (0, i))],
        out_specs=[pl.BlockSpec((tm, P_WIDTH), lambda i: (i, 0)),
                   pl.BlockSpec((qrows, tm), lambda i: (0, i)),
                   pl.BlockSpec((vrows, tm), lambda i: (0, i))],
        out_shape=[jax.ShapeDtypeStruct((t, P_WIDTH), BF16),
                   jax.ShapeDtypeStruct((qrows, t), BF16),
                   jax.ShapeDtypeStruct((vrows, t), BF16)],
        name="in_proj",
    )(x, g, w, wqt, wvt, cos_a, sin_a, cos_t, sin_t)


def _mla_prep_kernel(cq_ref, ckv_ref, kr_ref, gq_ref, gkv_ref, wqt_ref, wk_ref, wvt_ref,
                     cos_ref, sin_ref, cost_ref, sint_ref, qt_ref, k_ref, vt_ref):
    half = B_ROPE // 2
    nt = (((1,), (1,)), ((), ()))
    cq = _rms(cq_ref[...].astype(F32), gq_ref[...]).astype(BF16)
    ckv = _rms(ckv_ref[...].astype(F32), gkv_ref[...]).astype(BF16)

    cos, sin = cos_ref[...], sin_ref[...]
    lane = lax.broadcasted_iota(jnp.int32, cos.shape, 1)
    first = (lane >= B_NOPE) & (lane < B_NOPE + half)
    kr = _rot_tile(pltpu.roll(kr_ref[:, LANES:].astype(F32), B_NOPE, 1), cos, sin, first, half)
    k = jnp.dot(ckv, wk_ref[...], preferred_element_type=F32)
    for hd in range(B_HEADS):
        sl = slice(hd * LANES, (hd + 1) * LANES)
        k_ref[:, sl] = (k[:, sl] + kr).astype(BF16)

    ct, st = cost_ref[...], sint_ref[...]
    qscale = (B_NOPE + B_ROPE) ** -0.5 * LOG2E
    qt = lax.dot_general(wqt_ref[...], cq, nt, preferred_element_type=F32)
    vt = lax.dot_general(wvt_ref[...], ckv, nt, preferred_element_type=F32)
    row = lax.broadcasted_iota(jnp.int32, (LANES, vt.shape[1]), 0)
    ones_row = (row == B_V).astype(F32)
    for hd in range(B_HEADS):
        r0 = hd * LANES
        x1 = qt[r0 + B_NOPE:r0 + B_NOPE + half]
        x2 = qt[r0 + B_NOPE + half:r0 + B_NOPE + B_ROPE]
        qh = jnp.concatenate([qt[r0:r0 + B_NOPE], x1 * ct - x2 * st, x2 * ct + x1 * st,
                              qt[r0 + B_NOPE + B_ROPE:r0 + LANES]], axis=0)
        qt_ref[r0:r0 + LANES, :] = (qh * qscale).astype(BF16)
        vt_ref[r0:r0 + LANES, :] = (vt[r0:r0 + LANES] + ones_row).astype(BF16)


def _mla_prep(p, gq, gkv, wqt, wk, wvt, cos_b, sin_b, cos_bt, sin_bt, batch):
    t = p.shape[0]
    s_len = t // batch
    tm = TM_PREP
    nt = s_len // tm
    hw = B_HEADS * LANES
    half = B_ROPE // 2
    const = lambda b, i: (0, 0)
    tok = lambda b, i: (b * nt + i, 0)
    return pl.pallas_call(
        _mla_prep_kernel,
        grid=(batch, nt),
        in_specs=[pl.BlockSpec((tm, B_Q_RANK), lambda b, i: (b * nt + i, P_CQ // B_Q_RANK)),
                  pl.BlockSpec((tm, B_KV_RANK), lambda b, i: (b * nt + i, P_CKV // B_KV_RANK)),
                  pl.BlockSpec((tm, 2 * LANES), lambda b, i: (b * nt + i, P_KA // (2 * LANES))),
                  pl.BlockSpec((1, B_Q_RANK), const),
                  pl.BlockSpec((1, B_KV_RANK), const),
                  pl.BlockSpec((hw, B_Q_RANK), const),
                  pl.BlockSpec((B_KV_RANK, hw), const),
                  pl.BlockSpec((hw, B_KV_RANK), const),
                  pl.BlockSpec((tm, LANES), tok),
                  pl.BlockSpec((tm, LANES), tok),
                  pl.BlockSpec((half, tm), lambda b, i: (A_HEAD_DIM // 2 // half, b * nt + i)),
                  pl.BlockSpec((half, tm), lambda b, i: (A_HEAD_DIM // 2 // half, b * nt + i))],
        out_specs=[pl.BlockSpec((None, hw, tm), lambda b, i: (b, 0, i)),
                   pl.BlockSpec((None, tm, hw), lambda b, i: (b, i, 0)),
                   pl.BlockSpec((None, hw, tm), lambda b, i: (b, 0, i))],
        out_shape=[jax.ShapeDtypeStruct((batch, hw, s_len), BF16),
                   jax.ShapeDtypeStruct((batch, s_len, hw), BF16),
                   jax.ShapeDtypeStruct((batch, hw, s_len), BF16)],
        name="mla_prep",
    )(p, p, p, gq, gkv, wqt, wk, wvt, cos_b, sin_b, cos_bt, sin_bt)


def _mla_attn_kernel(qt_ref, k_ref, vt_ref, o_ref, m_sc, acc_sc, s0_sc, s1_sc, *, tk):
    n_chunks = k_ref.shape[0] // tk
    m_sc[...] = jnp.full(m_sc.shape, -jnp.inf, F32)
    acc_sc[...] = jnp.zeros(acc_sc.shape, F32)
    heads = [slice(hd * LANES, (hd + 1) * LANES) for hd in range(2)]

    def keys(j):
        return pl.ds(pl.multiple_of(j * tk, tk), tk)

    def scores(j, dst):
        for hd, sl in enumerate(heads):
            dst[hd] = jnp.dot(k_ref[keys(j), sl], qt_ref[sl, :], preferred_element_type=F32)

    def accumulate(j, src):
        for hd, sl in enumerate(heads):
            st = src[hd]
            m_old = m_sc[hd]
            m_new = jnp.maximum(m_old, jnp.max(st, axis=0, keepdims=True))
            pt = jnp.exp2(st - m_new).astype(BF16)
            acc_sc[hd] = acc_sc[hd] * jnp.exp2(m_old - m_new) + jnp.dot(
                vt_ref[sl, keys(j)], pt, preferred_element_type=F32)
            m_sc[hd] = m_new

    scores(0, s0_sc)

    def body(jj, carry):
        j = 2 * jj
        scores(j + 1, s1_sc)
        accumulate(j, s0_sc)
        scores(j + 2, s0_sc)
        accumulate(j + 1, s1_sc)
        return carry

    lax.fori_loop(0, n_chunks // 2 - 1, body, 0)
    scores(n_chunks - 1, s1_sc)
    accumulate(n_chunks - 2, s0_sc)
    accumulate(n_chunks - 1, s1_sc)
    outs = []
    for hd in range(2):
        acc = acc_sc[hd]
        outs.append(acc[:B_V] * (1.0 / acc[B_V:B_V + 1]))
    o_ref[...] = jnp.concatenate(outs, axis=0).T.astype(BF16)


def _mla_attn(qt, k, vt):
    batch, s_len, hw = k.shape
    tq = TQ_MLA
    out = pl.pallas_call(
        functools.partial(_mla_attn_kernel, tk=TK_MLA),
        grid=(batch, B_HEADS // 2, s_len // tq),
        in_specs=[pl.BlockSpec((None, 2 * LANES, tq), lambda b, p, i: (b, p, i)),
                  pl.BlockSpec((None, s_len, 2 * LANES), lambda b, p, i: (b, 0, p)),
                  pl.BlockSpec((None, 2 * LANES, s_len), lambda b, p, i: (b, p, 0))],
        out_specs=pl.BlockSpec((None, tq, LANES), lambda b, p, i: (b, i, p)),
        out_shape=jax.ShapeDtypeStruct((batch, s_len, B_HEADS * B_V), BF16),
        scratch_shapes=[pltpu.VMEM((2, 1, tq), F32), pltpu.VMEM((2, LANES, tq), F32),
                        pltpu.VMEM((2, TK_MLA, tq), F32), pltpu.VMEM((2, TK_MLA, tq), F32)],
        name="mla_attn",
    )(qt, k, vt)
    return out.reshape(batch * s_len, B_HEADS * B_V)


def _win_attn_kernel(sink_ref, qt_ref, kp_ref, kc_ref, kn_ref, vp_ref, vc_ref, vn_ref, o_ref, *, s_len):
    n = pl.program_id(1)
    grp = A_HEADS // A_KV_HEADS
    dh = A_HEAD_DIM
    tq = kc_ref.shape[0]
    kj = lax.broadcasted_iota(jnp.int32, (tq + 2 * BLK, tq), 0)
    qi = lax.broadcasted_iota(jnp.int32, (tq + 2 * BLK, tq), 1)
    k_abs = n * tq - BLK + kj
    ok = (jnp.abs(kj - BLK - qi) <= A_WINDOW) & (k_abs >= 0) & (k_abs < s_len)
    ok = jnp.concatenate([ok] * grp, axis=1)
    lane_head = lax.broadcasted_iota(jnp.int32, (1, grp * tq), 1) // tq
    kband = jnp.concatenate([kp_ref[:, :LANES], kc_ref[:, :LANES], kn_ref[:, :LANES]], axis=0)
    zeros = jnp.zeros((dh, tq), BF16)
    heads = []
    for g in range(A_KV_HEADS):
        tiles = []
        for a in range(grp):
            qh = qt_ref[(g * grp + a) * dh:(g * grp + a + 1) * dh, :]
            tiles.append(jnp.concatenate([zeros] * g + [qh] + [zeros] * (A_KV_HEADS - 1 - g), axis=0))
        qg = jnp.concatenate(tiles, axis=1)
        st = jnp.where(ok, jnp.dot(kband, qg, preferred_element_type=F32), NEG_INF)
        sk = jnp.zeros((1, grp * tq), F32)
        for a in range(grp):
            sk = jnp.where(lane_head == a, sink_ref[g * grp + a] * LOG2E, sk)
        m = jnp.maximum(jnp.max(st, axis=0, keepdims=True), sk)
        pt = jnp.exp2(st - m).astype(BF16)
        rows = slice(g * LANES, (g + 1) * LANES)
        vband = jnp.concatenate([vp_ref[rows, :], vc_ref[rows, :], vn_ref[rows, :]], axis=1)
        ot = jnp.dot(vband, pt, preferred_element_type=F32)
        o = ot[:dh] * (1.0 / (ot[dh:dh + 1] + jnp.exp2(sk - m)))
        heads.extend(o[:, a * tq:(a + 1) * tq] for a in range(grp))
    for pair in range(A_HEADS // 2):
        o_ref[:, pair * LANES:(pair + 1) * LANES] = jnp.concatenate(
            [heads[2 * pair], heads[2 * pair + 1]], axis=0).T.astype(BF16)


def _win_attn(p, qt, vt, sink, batch):
    t = p.shape[0]
    s_len = t // batch
    tq = TQ_WIN
    nq, nb, r = s_len // tq, s_len // BLK, tq // BLK
    p3 = p.reshape(batch, s_len, P_WIDTH)
    kcol = P_KA // (2 * LANES)
    lo = lambda n: jnp.maximum(n * r - 1, 0)
    hi = lambda n: jnp.minimum((n + 1) * r, nb - 1)
    kedge = lambda f: pl.BlockSpec((None, BLK, 2 * LANES), lambda b, n, s: (b, f(n), kcol))
    vedge = lambda f: pl.BlockSpec((A_KV_HEADS * LANES, BLK), lambda b, n, s: (0, b * nb + f(n)))
    out = pl.pallas_call(
        functools.partial(_win_attn_kernel, s_len=s_len),
        grid_spec=pltpu.PrefetchScalarGridSpec(
            num_scalar_prefetch=1,
            grid=(batch, nq),
            in_specs=[pl.BlockSpec((A_HEADS * A_HEAD_DIM, tq), lambda b, n, s: (0, b * nq + n)),
                      kedge(lo), pl.BlockSpec((None, tq, 2 * LANES), lambda b, n, s: (b, n, kcol)), kedge(hi),
                      vedge(lo), pl.BlockSpec((A_KV_HEADS * LANES, tq), lambda b, n, s: (0, b * nq + n)), vedge(hi)],
            out_specs=pl.BlockSpec((None, tq, A_HEADS * A_HEAD_DIM), lambda b, n, s: (b, n, 0)),
        ),
        out_shape=jax.ShapeDtypeStruct((batch, s_len, A_HEADS * A_HEAD_DIM), BF16),
        name="win_attn",
    )(sink, qt, p3, p3, p3, vt, vt, vt)
    return out.reshape(t, A_HEADS * A_HEAD_DIM)


def _mem_kv_kernel(mem_ref, g_ref, w_ref, o_ref):
    mem_n = _rms(mem_ref[...], g_ref[...]).astype(BF16)
    o_ref[...] = jnp.dot(mem_n, w_ref[...], preferred_element_type=F32).astype(BF16)


def _mem_kv(mem2, g_mem, w_mem_kv):
    depth, d, width = w_mem_kv.shape
    rows = mem2.shape[0]
    return pl.pallas_call(
        _mem_kv_kernel,
        grid=(depth,),
        in_specs=[pl.BlockSpec((rows, d), lambda l: (0, 0)),
                  pl.BlockSpec((1, d), lambda l: (0, 0)),
                  pl.BlockSpec((None, d, width), lambda l: (l, 0, 0))],
        out_specs=pl.BlockSpec((None, rows, width), lambda l: (l, 0, 0)),
        out_shape=jax.ShapeDtypeStruct((depth, rows, width), BF16),
        name="mem_kv",
    )(mem2, g_mem, w_mem_kv)


def _mem_attn_kernel(q_ref, kv_ref, o_ref):
    dh = M_HEAD_DIM
    for hd in range(M_HEADS):
        q = q_ref[:, hd * dh:(hd + 1) * dh]
        k = kv_ref[:, hd * dh:(hd + 1) * dh]
        v = kv_ref[:, (M_HEADS + hd) * dh:(M_HEADS + hd + 1) * dh]
        sc = lax.dot_general(q, k, (((1,), (1,)), ((), ())), preferred_element_type=F32)
        e = jnp.exp(sc - jnp.max(sc, axis=-1, keepdims=True))
        pr = (e * (1.0 / jnp.sum(e, axis=-1, keepdims=True))).astype(BF16)
        o_ref[:, hd * dh:(hd + 1) * dh] = jnp.dot(pr, v, preferred_element_type=F32).astype(BF16)


def _mem_attn(p, mem_kv_l, batch):
    t = p.shape[0]
    s_len = t // batch
    tm = TM_MEM
    n_mem = mem_kv_l.shape[0] // batch
    width = M_HEADS * M_HEAD_DIM
    p3 = p.reshape(batch, s_len, P_WIDTH)
    kv3 = mem_kv_l.reshape(batch, n_mem, 2 * width)
    out = pl.pallas_call(
        _mem_attn_kernel,
        grid=(batch, s_len // tm),
        in_specs=[pl.BlockSpec((None, tm, width), lambda b, i: (b, i, P_QM // width)),
                  pl.BlockSpec((None, n_mem, 2 * width), lambda b, i: (b, 0, 0))],
        out_specs=pl.BlockSpec((None, tm, width), lambda b, i: (b, i, 0)),
        out_shape=jax.ShapeDtypeStruct((batch, s_len, width), BF16),
        name="mem_attn",
    )(p3, kv3)
    return out.reshape(t, width)


def _mix_kernel(x_ref, ya_ref, yb_ref, ym_ref, g_ref, wb_ref, wo_ref, o_ref):
    d = x_ref.shape[1]
    mixed = None
    for i, y_ref in enumerate((ya_ref, yb_ref, ym_ref)):
        gate = jax.nn.sigmoid(g_ref[:, i * d:(i + 1) * d].astype(F32))
        term = gate * jnp.dot(y_ref[...], wb_ref[i], preferred_element_type=F32)
        mixed = term if mixed is None else mixed + term
    o_ref[...] = x_ref[...] + jnp.dot(mixed.astype(BF16), wo_ref[...], preferred_element_type=F32)


def _mix(x, ya, yb, ym, p, wb, wo):
    t, d = x.shape
    tm = TM_MIX
    bw = ya.shape[1]
    row = lambda i: (i, 0)
    return pl.pallas_call(
        _mix_kernel,
        grid=(t // tm,),
        in_specs=[pl.BlockSpec((tm, d), row),
                  pl.BlockSpec((tm, bw), row), pl.BlockSpec((tm, bw), row), pl.BlockSpec((tm, bw), row),
                  pl.BlockSpec((tm, 3 * d), lambda i: (i, P_GATES)),
                  pl.BlockSpec((3, bw, d), lambda i: (0, 0, 0)),
                  pl.BlockSpec((d, d), lambda i: (0, 0))],
        out_specs=pl.BlockSpec((tm, d), row),
        out_shape=jax.ShapeDtypeStruct((t, d), F32),
        name="mix_out",
    )(x, ya, yb, ym, p, wb, wo)


def _swiglu_step(h, wg, wu, wd):
    gate = jnp.dot(h, wg, preferred_element_type=F32)
    up = jnp.dot(h, wu, preferred_element_type=F32)
    act = (gate * jax.nn.sigmoid(gate) * up).astype(BF16)
    return jnp.dot(act, wd, preferred_element_type=F32)


def _ffn_kernel(x_ref, g_ref, wg_ref, wu_ref, wd_ref, o_ref, h_sc, acc_sc):
    j = pl.program_id(1)

    @pl.when(j == 0)
    def _():
        h_sc[...] = _rms(x_ref[...], g_ref[...]).astype(BF16)
        acc_sc[...] = jnp.zeros(acc_sc.shape, F32)

    acc_sc[...] += _swiglu_step(h_sc[...], wg_ref[...], wu_ref[...], wd_ref[...])

    @pl.when(j == pl.num_programs(1) - 1)
    def _():
        o_ref[...] = x_ref[...] + acc_sc[...]


def _ffn(x, g, wg, wu, wd):
    t, d = x.shape
    f = wg.shape[1]
    tm, tf = TM_FFN, TF_FFN
    return pl.pallas_call(
        _ffn_kernel,
        grid=(t // tm, f // tf),
        in_specs=[pl.BlockSpec((tm, d), lambda i, j: (i, 0)),
                  pl.BlockSpec((1, d), lambda i, j: (0, 0)),
                  pl.BlockSpec((d, tf), lambda i, j: (0, j)),
                  pl.BlockSpec((d, tf), lambda i, j: (0, j)),
                  pl.BlockSpec((tf, d), lambda i, j: (j, 0))],
        out_specs=pl.BlockSpec((tm, d), lambda i, j: (i, 0)),
        out_shape=jax.ShapeDtypeStruct((t, d), F32),
        scratch_shapes=[pltpu.VMEM((tm, d), BF16), pltpu.VMEM((tm, d), F32)],
        name="ffn_dense",
    )(x, g, wg, wu, wd)


def _router_kernel(x_ref, g_ref, wr_ref, idx_ref, wgt_ref, xt_ref):
    x = x_ref[...]
    xt_ref[...] = x.reshape(xt_ref.shape)
    h = _rms(x, g_ref[...])
    logits = lax.dot_general(wr_ref[...], h, (((1,), (1,)), ((), ())),
                             precision=lax.Precision.HIGHEST, preferred_element_type=F32)
    e_id = lax.broadcasted_iota(jnp.int32, logits.shape, 0)
    m1 = jnp.max(logits, axis=0, keepdims=True)
    i1 = jnp.min(jnp.where(logits == m1, e_id, N_EXPERTS), axis=0, keepdims=True)
    rest = jnp.where(e_id == i1, -jnp.inf, logits)
    m2 = jnp.max(rest, axis=0, keepdims=True)
    i2 = jnp.min(jnp.where(rest == m2, e_id, N_EXPERTS), axis=0, keepdims=True)
    e2 = jnp.exp(m2 - m1)
    w1 = 1.0 / (1.0 + e2)
    idx_ref[...] = jnp.concatenate([i1, i2], axis=0)
    wgt_ref[...] = jnp.concatenate([w1, e2 * w1], axis=0)


def _router(x, g, wr_t):
    t, d = x.shape
    tm = TM_ROUTE
    return pl.pallas_call(
        _router_kernel,
        grid=(t // tm,),
        in_specs=[pl.BlockSpec((tm, d), lambda i: (i, 0)),
                  pl.BlockSpec((1, d), lambda i: (0, 0)),
                  pl.BlockSpec((N_EXPERTS, d), lambda i: (0, 0))],
        out_specs=[pl.BlockSpec((TOP_K, tm), lambda i: (0, i)),
                   pl.BlockSpec((TOP_K, tm), lambda i: (0, i)),
                   pl.BlockSpec((tm * SUBLANES, LANES), lambda i: (i, 0))],
        out_shape=[jax.ShapeDtypeStruct((TOP_K, t), jnp.int32),
                   jax.ShapeDtypeStruct((TOP_K, t), F32),
                   jax.ShapeDtypeStruct((t * SUBLANES, LANES), F32)],
        name="moe_router",
    )(x, g, wr_t)


def _moe_rows_per_step(n_steps):
    return -(-TM_EXP // (n_steps * SUBLANES)) * SUBLANES


def _slot_tables(top_i, n_tok, buf_rows):
    n_asg = n_tok * TOP_K
    flat_e = top_i.T.reshape(n_asg)
    order = jnp.argsort(flat_e, stable=True).astype(jnp.int32)
    counts = jnp.sum(flat_e[:, None] == jnp.arange(N_EXPERTS, dtype=jnp.int32)[None, :], axis=0, dtype=jnp.int32)
    padded = (counts + TM_EXP - 1) // TM_EXP * TM_EXP
    pad_end = jnp.cumsum(padded)
    start = jnp.cumsum(counts) - counts
    pads_before = jnp.cumsum(padded - counts) - (padded - counts)
    n_slots = n_asg + N_EXPERTS * TM_EXP
    n_blk = n_slots // TM_EXP
    extra = buf_rows - TM_EXP
    blk_e = jnp.minimum(jnp.searchsorted(pad_end, jnp.arange(n_blk, dtype=jnp.int32) * TM_EXP, side='right'),
                        N_EXPERTS - 1).astype(jnp.int32)
    slot_e = jnp.repeat(blk_e, TM_EXP)
    rank = jnp.arange(n_slots, dtype=jnp.int32) - (pad_end - padded)[slot_e]
    is_pad = rank >= counts[slot_e]
    asg = order[jnp.clip(start[slot_e] + rank, 0, n_asg - 1)]
    slot_tok = jnp.where(is_pad, 0, asg // TOP_K)
    slot_row = jnp.where(is_pad, n_asg + pads_before[slot_e] + rank - counts[slot_e],
                         (asg % TOP_K) * n_tok + slot_tok)
    gather_tok = jnp.concatenate([slot_tok.reshape(n_blk, TM_EXP), jnp.zeros((n_blk, extra), jnp.int32)], axis=1)
    spare = n_slots + jnp.arange((n_blk + 1) * buf_rows - n_slots, dtype=jnp.int32)
    lead, over = spare[:TM_EXP], spare[TM_EXP:].reshape(n_blk + 1, extra)
    scatter_row = jnp.concatenate(
        [jnp.concatenate([lead[None], slot_row.reshape(n_blk, TM_EXP)], axis=0), over], axis=1)
    return blk_e, gather_tok.reshape(-1).astype(jnp.int32), scatter_row.reshape(-1).astype(jnp.int32)


def _moe_ffn_kernel(blk_e_ref, gtok_ref, srow_ref, x_hbm, g_ref, wg_ref, wu_ref, wd_ref, o_hbm,
                    xg_sc, ys_sc, h_sc, acc_sc, gsem, ssem, *, rows_per_step):
    i, j = pl.program_id(0), pl.program_id(1)
    n_blk, n_steps = pl.num_programs(0), pl.num_programs(1)
    rows, d = h_sc.shape
    tr = d // LANES
    buf_rows = xg_sc.shape[1] // tr
    slot = i % 2
    other = 1 - slot

    def tile(ref, k):
        return ref.at[pl.ds(pl.multiple_of(k * tr, tr), tr), :]

    def gathered(s):
        return pltpu.make_async_copy(x_hbm.at[pl.ds(0, buf_rows * tr), :], xg_sc.at[s], gsem.at[s])

    def scattered(s):
        return pltpu.make_async_copy(ys_sc.at[s], o_hbm.at[pl.ds(0, buf_rows * tr), :], ssem.at[s])

    def get_row(s, k, entry):
        return pltpu.make_async_copy(tile(x_hbm, gtok_ref[entry]), tile(xg_sc.at[s], k), gsem.at[s])

    def put_row(s, k, entry):
        return pltpu.make_async_copy(tile(ys_sc.at[s], k), tile(o_hbm, srow_ref[entry]), ssem.at[s])

    @pl.when((i == 0) & (j == 0))
    def _():
        ys_sc[...] = jnp.zeros(ys_sc.shape, F32)

        def start(k, c):
            get_row(0, k, k).start()
            return c

        lax.fori_loop(0, buf_rows, start, 0)

    @pl.when(j == 0)
    def _():
        gathered(slot).wait()
        for s in range(2):
            @pl.when(slot == s)
            def _():
                xr = xg_sc[s, 0:rows * tr, :].reshape(rows, d)
                h_sc[...] = _rms(xr, g_ref[...]).astype(BF16)

        acc_sc[...] = jnp.zeros(acc_sc.shape, F32)

    nxt = jnp.minimum(i + 1, n_blk - 1)
    for r in range(rows_per_step):
        k = j * rows_per_step + r
        get_row(other, k, nxt * buf_rows + k).start()
        put_row(other, k, i * buf_rows + k).start()

    acc_sc[...] += _swiglu_step(h_sc[...], wg_ref[...], wu_ref[...], wd_ref[...])

    @pl.when(j == n_steps - 1)
    def _():
        @pl.when(i > 0)
        def _():
            scattered(slot).wait()

        for s in range(2):
            @pl.when(slot == s)
            def _():
                ys_sc[s, 0:rows * tr, :] = acc_sc[...].reshape(rows * tr, LANES)

    @pl.when((i == n_blk - 1) & (j == n_steps - 1))
    def _():
        def start(k, c):
            put_row(slot, k, n_blk * buf_rows + k).start()
            return c

        lax.fori_loop(0, buf_rows, start, 0)
        scattered(other).wait()
        scattered(slot).wait()
        gathered(other).wait()


def _moe_ffn(x_tiles, g, wg, wu, wd, blk_e, gather_tok, scatter_row):
    d, f = wg.shape[1], wg.shape[2]
    tr = d // LANES
    assert tr % SUBLANES == 0
    tm, tf = TM_EXP, TF_FFN
    n_steps = f // tf
    rows_per_step = _moe_rows_per_step(n_steps)
    buf_rows = rows_per_step * n_steps
    n_blk = gather_tok.shape[0] // buf_rows
    return pl.pallas_call(
        functools.partial(_moe_ffn_kernel, rows_per_step=rows_per_step),
        grid_spec=pltpu.PrefetchScalarGridSpec(
            num_scalar_prefetch=3,
            grid=(n_blk, n_steps),
            in_specs=[pl.BlockSpec(memory_space=pl.ANY),
                      pl.BlockSpec((1, d), lambda i, j, be, gt, sr: (0, 0)),
                      pl.BlockSpec((None, d, tf), lambda i, j, be, gt, sr: (be[i], 0, j)),
                      pl.BlockSpec((None, d, tf), lambda i, j, be, gt, sr: (be[i], 0, j)),
                      pl.BlockSpec((None, tf, d), lambda i, j, be, gt, sr: (be[i], j, 0))],
            out_specs=pl.BlockSpec(memory_space=pl.ANY),
            scratch_shapes=[pltpu.VMEM((2, buf_rows * tr, LANES), F32), pltpu.VMEM((2, buf_rows * tr, LANES), F32),
                            pltpu.VMEM((tm, d), BF16), pltpu.VMEM((tm, d), F32),
                            pltpu.SemaphoreType.DMA((2,)), pltpu.SemaphoreType.DMA((2,))],
        ),
        out_shape=jax.ShapeDtypeStruct((scatter_row.shape[0] * tr, LANES), F32),
        name="moe_ffn",
    )(blk_e, gather_tok, scatter_row, x_tiles, g, wg, wu, wd)


def _moe_combine_kernel(x_ref, w_ref, y0_ref, y1_ref, gf_ref, o_ref, *, final_norm):
    tm, d = x_ref.shape
    y0, y1 = y0_ref[...].reshape(tm, d), y1_ref[...].reshape(tm, d)
    out = x_ref[...] + w_ref[:, 0:1] * y0 + w_ref[:, 1:2] * y1
    if final_norm:
        out = _rms(out, gf_ref[...])
    o_ref[...] = out


def _moe_combine(x, top_w, y, g_final, final_norm):
    t, d = x.shape
    tr = d // LANES
    tm = TM_COMB
    nt = t // tm
    return pl.pallas_call(
        functools.partial(_moe_combine_kernel, final_norm=final_norm),
        grid=(nt,),
        in_specs=[pl.BlockSpec((tm, d), lambda i: (i, 0)),
                  pl.BlockSpec((tm, TOP_K), lambda i: (i, 0)),
                  pl.BlockSpec((tm * tr, LANES), lambda i: (i, 0)),
                  pl.BlockSpec((tm * tr, LANES), lambda i: (nt + i, 0)),
                  pl.BlockSpec((1, d), lambda i: (0, 0))],
        out_specs=pl.BlockSpec((tm, d), lambda i: (i, 0)),
        out_shape=jax.ShapeDtypeStruct((t, d), F32),
        name="moe_combine",
    )(x, top_w, y, y, g_final)


def _final_norm_kernel(x_ref, g_ref, o_ref):
    o_ref[...] = _rms(x_ref[...], g_ref[...])


def _final_norm(x, g):
    t, d = x.shape
    tm = 1024
    return pl.pallas_call(
        _final_norm_kernel,
        grid=(t // tm,),
        in_specs=[pl.BlockSpec((tm, d), lambda i: (i, 0)), pl.BlockSpec((1, d), lambda i: (0, 0))],
        out_specs=pl.BlockSpec((tm, d), lambda i: (i, 0)),
        out_shape=jax.ShapeDtypeStruct((t, d), F32),
        name="final_norm",
    )(x, g)


def _relayout_w_in(w_in):
    qa, ka, va, cq, ckv, kr, qm, gates = jnp.split(
        w_in, [512, 640, 768, 1024, 1152, 1184, 1696], axis=-1)
    pad = jnp.zeros(w_in.shape[:2] + (LANES - B_ROPE,), w_in.dtype)
    w_tok = jnp.concatenate([gates, qm, ka, kr, pad, cq, ckv], axis=-1).astype(BF16)
    return w_tok, jnp.swapaxes(qa, 1, 2).astype(BF16), jnp.swapaxes(va, 1, 2).astype(BF16)


def _relayout_latent(w_uq, w_ukv):
    depth = w_uq.shape[0]
    q = w_uq.reshape(depth, B_Q_RANK, B_HEADS, B_NOPE + B_ROPE)
    q = jnp.pad(q, ((0, 0), (0, 0), (0, 0), (0, LANES - B_NOPE - B_ROPE)))
    kv = w_ukv.reshape(depth, B_KV_RANK, B_HEADS, B_NOPE + B_V)
    k = jnp.pad(kv[..., :B_NOPE], ((0, 0), (0, 0), (0, 0), (0, LANES - B_NOPE)))
    v = jnp.pad(kv[..., B_NOPE:], ((0, 0), (0, 0), (0, 0), (0, LANES - B_V)))
    flat = lambda a, r: a.reshape(depth, r, B_HEADS * LANES).astype(BF16)
    return (jnp.swapaxes(flat(q, B_Q_RANK), 1, 2), flat(k, B_KV_RANK),
            jnp.swapaxes(flat(v, B_KV_RANK), 1, 2))


def kernel(x, mem, positions, g_mix, w_in, g_q_lat, w_uq, g_kv_lat, w_ukv, sink, w_mem_kv, w_branch, w_out,
           g_ffn, w_gate_dense, w_up_dense, w_down_dense, w_router, w_gate_exp, w_up_exp, w_down_exp,
           g_mem, g_final):
    batch, s_len, d = x.shape
    depth = w_in.shape[0]
    t = batch * s_len
    row = lambda a: a.reshape(1, -1).astype(F32)

    w_in_p, wqa_t, wva_t = _relayout_w_in(w_in)
    wq_p, wk_p, wv_p = _relayout_latent(w_uq, w_ukv)
    w_branch_b, w_out_b = w_branch.astype(BF16), w_out.astype(BF16)
    wgd, wud, wdd = w_gate_dense.astype(BF16), w_up_dense.astype(BF16), w_down_dense.astype(BF16)
    wge, wue, wde = w_gate_exp.astype(BF16), w_up_exp.astype(BF16), w_down_exp.astype(BF16)
    wr_t = jnp.swapaxes(w_router, 1, 2).astype(F32)

    cos_a, sin_a, cos_b, sin_b, cos_t, sin_t = _rope_tables(positions)
    mem_kv = _mem_kv(mem.reshape(batch * mem.shape[1], d).astype(F32), row(g_mem), w_mem_kv.astype(BF16))

    xt = x.reshape(t, d).astype(F32)
    for l in range(depth):
        p, qa_t, va_t = _in_proj(xt, row(g_mix[l]), w_in_p[l], wqa_t[l], wva_t[l], cos_a, sin_a, cos_t, sin_t)
        y_a = _win_attn(p, qa_t, va_t, sink[l].astype(F32), batch)
        qt, k, vt = _mla_prep(p, row(g_q_lat[l]), row(g_kv_lat[l]), wq_p[l], wk_p[l], wv_p[l],
                              cos_b, sin_b, cos_t, sin_t, batch)
        y_b = _mla_attn(qt, k, vt)
        y_m = _mem_attn(p, mem_kv[l], batch)
        xt = _mix(xt, y_a, y_b, y_m, p, w_branch_b[l], w_out_b[l])
        i = l // 2
        if l % 2 == 0:
            xt = _ffn(xt, row(g_ffn[l]), wgd[i], wud[i], wdd[i])
            if l == depth - 1:
                xt = _final_norm(xt, row(g_final))
        else:
            top_i, top_w, x_tiles = _router(xt, row(g_ffn[l]), wr_t[i])
            buf_rows = _moe_rows_per_step(wge.shape[3] // TF_FFN) * (wge.shape[3] // TF_FFN)
            blk_e, gather_tok, scatter_row = _slot_tables(top_i, t, buf_rows)
            y = _moe_ffn(x_tiles, row(g_ffn[l]), wge[i], wue[i], wde[i], blk_e, gather_tok, scatter_row)
            xt = _moe_combine(xt, top_w.T, y, row(g_final), final_norm=(l == depth - 1))
    return xt.reshape(batch, s_len, d).astype(x.dtype)
```

```python
import functools

import jax
import jax.numpy as jnp
from jax import lax
from jax.experimental import pallas as pl
from jax.experimental.pallas import tpu as pltpu

F32 = jnp.float32
BF16 = jnp.bfloat16

ROPE_THETA = 10000.0
EPS = 1e-6
NEG_INF = -1e30
LOG2E = 1.4426950408889634

A_HEADS, A_KV_HEADS, A_HEAD_DIM, A_WINDOW = 8, 2, 64, 128
B_HEADS, B_Q_RANK, B_KV_RANK, B_NOPE, B_ROPE, B_V = 8, 256, 128, 64, 32, 64
M_HEADS, M_HEAD_DIM = 4, 128
N_EXPERTS, TOP_K = 8, 2
LANES = 128
SUBLANES = 8
BLK = 128

P_GATES = 0
P_QM = 3072
P_KA = 3584
P_CQ = 3840
P_CKV = 4096
P_WIDTH = 4224

TM_PROJ = 512
TM_PREP = 512
TM_MIX = 512
TM_MEM = 512
TM_FFN = 1024
TF_FFN = 512
TQ_WIN = 256
TQ_MLA = 1024
TK_MLA = 512
TM_ROUTE = 1024
TM_EXP = 1024
TM_COMB = 512
ROW_DMA_PRIORITY = 1


def _rms(x, g):
    return x * lax.rsqrt(jnp.mean(x * x, axis=-1, keepdims=True) + EPS) * g


def _rot_tile(x, cos, sin_signed, first, shift):
    up = pltpu.roll(x, LANES - shift, 1)
    dn = pltpu.roll(x, shift, 1)
    return x * cos + jnp.where(first, up, dn) * sin_signed


def _rope_table_kernel(pos_ref, posr_ref, par_ref, invc_ref, ca_ref, sa_ref, cb_ref, sb_ref, cbt_ref, sbt_ref):
    pos = pos_ref[...]
    ang_a = pos * par_ref[0:1, :]
    ca_ref[...] = jnp.cos(ang_a)
    sa_ref[...] = jnp.sin(ang_a) * par_ref[1:2, :]
    ang_b = pos * par_ref[2:3, :]
    cb_ref[...] = jnp.cos(ang_b)
    sb_ref[...] = jnp.sin(ang_b) * par_ref[3:4, :]
    ang_t = invc_ref[...] * posr_ref[...]
    cbt_ref[...] = jnp.cos(ang_t)
    sbt_ref[...] = jnp.sin(ang_t)


def _rope_tables(positions):
    t = positions.size
    tm = 2048
    lane = jnp.arange(LANES)
    half_a = A_HEAD_DIM // 2
    inv_a = jnp.power(F32(ROPE_THETA), -jnp.arange(half_a, dtype=F32) * (2.0 / A_HEAD_DIM))
    inv_a = inv_a[lane % half_a]
    sgn_a = jnp.where((lane % A_HEAD_DIM) < half_a, -1.0, 1.0).astype(F32)
    half_b = B_ROPE // 2
    inv_b16 = jnp.power(F32(ROPE_THETA), -jnp.arange(half_b, dtype=F32) * (2.0 / B_ROPE))
    in_rope = (lane >= B_NOPE) & (lane < B_NOPE + B_ROPE)
    inv_b = jnp.where(in_rope, inv_b16[(lane - B_NOPE) % half_b], 0.0).astype(F32)
    sgn_b = jnp.where(in_rope, jnp.where(lane < B_NOPE + half_b, -1.0, 1.0), 0.0).astype(F32)
    par = jnp.zeros((8, LANES), F32).at[0].set(inv_a).at[1].set(sgn_a).at[2].set(inv_b).at[3].set(sgn_b)
    pos = positions.reshape(t, 1).astype(F32)
    tab = jax.ShapeDtypeStruct((t, LANES), F32)
    n_freq = half_a + half_b
    inv_col = jnp.concatenate([inv_a[:half_a], inv_b16]).reshape(n_freq, 1)
    tab_t = jax.ShapeDtypeStruct((n_freq, t), F32)
    return pl.pallas_call(
        _rope_table_kernel,
        grid=(t // tm,),
        in_specs=[pl.BlockSpec((tm, 1), lambda i: (i, 0)),
                  pl.BlockSpec((1, tm), lambda i: (0, i)),
                  pl.BlockSpec((8, LANES), lambda i: (0, 0)),
                  pl.BlockSpec((n_freq, 1), lambda i: (0, 0))],
        out_specs=[pl.BlockSpec((tm, LANES), lambda i: (i, 0))] * 4
        + [pl.BlockSpec((n_freq, tm), lambda i: (0, i))] * 2,
        out_shape=[tab] * 4 + [tab_t] * 2,
        name="rope_tables",
    )(pos, pos.reshape(1, t), par, inv_col)


def _in_proj_kernel(x_ref, g_ref, w_ref, wqt_ref, wvt_ref, cos_ref, sin_ref, cost_ref, sint_ref,
                    o_ref, qt_ref, vt_ref):
    h = _rms(x_ref[...], g_ref[...]).astype(BF16)
    half = A_HEAD_DIM // 2

    def proj(c0, c1):
        return jnp.dot(h, w_ref[:, c0:c1], preferred_element_type=F32)

    for c0 in range(P_GATES, P_QM, 512):
        o_ref[:, c0:c0 + 512] = proj(c0, c0 + 512).astype(BF16)
    o_ref[:, P_QM:P_KA] = (proj(P_QM, P_KA) * (M_HEAD_DIM ** -0.5)).astype(BF16)
    cos, sin = cos_ref[...], sin_ref[...]
    lane = lax.broadcasted_iota(jnp.int32, cos.shape, 1)
    ka = proj(P_KA, P_CQ)
    o_ref[:, P_KA:P_KA + LANES] = _rot_tile(ka[:, :LANES], cos, sin, (lane % A_HEAD_DIM) < half, half).astype(BF16)
    o_ref[:, P_KA + LANES:P_CQ] = ka[:, LANES:].astype(BF16)
    o_ref[:, P_CQ:P_WIDTH] = proj(P_CQ, P_WIDTH).astype(BF16)

    nt = (((1,), (1,)), ((), ()))
    ct, st = cost_ref[...], sint_ref[...]
    qscale = A_HEAD_DIM ** -0.5 * LOG2E
    qt = lax.dot_general(wqt_ref[...], h, nt, preferred_element_type=F32)
    for hd in range(A_HEADS):
        r0 = hd * A_HEAD_DIM
        x1, x2 = qt[r0:r0 + half], qt[r0 + half:r0 + A_HEAD_DIM]
        qt_ref[r0:r0 + half, :] = ((x1 * ct - x2 * st) * qscale).astype(BF16)
        qt_ref[r0 + half:r0 + A_HEAD_DIM, :] = ((x2 * ct + x1 * st) * qscale).astype(BF16)
    vt = lax.dot_general(wvt_ref[...], h, nt, preferred_element_type=F32)
    row = lax.broadcasted_iota(jnp.int32, (LANES - A_HEAD_DIM, vt.shape[1]), 0)
    ones_pad = (row == 0).astype(BF16)
    for g in range(A_KV_HEADS):
        vt_ref[g * LANES:g * LANES + A_HEAD_DIM, :] = vt[g * A_HEAD_DIM:(g + 1) * A_HEAD_DIM].astype(BF16)
        vt_ref[g * LANES + A_HEAD_DIM:(g + 1) * LANES, :] = ones_pad


def _in_proj(x, g, w, wqt, wvt, cos_a, sin_a, cos_t, sin_t):
    t, d = x.shape
    tm = TM_PROJ
    half = A_HEAD_DIM // 2
    qrows, vrows = A_HEADS * A_HEAD_DIM, A_KV_HEADS * LANES
    const = lambda i: (0, 0)
    return pl.pallas_call(
        _in_proj_kernel,
        grid=(t // tm,),
        in_specs=[pl.BlockSpec((tm, d), lambda i: (i, 0)),
                  pl.BlockSpec((1, d), const),
                  pl.BlockSpec((d, P_WIDTH), const),
                  pl.BlockSpec((qrows, d), const),
                  pl.BlockSpec((A_KV_HEADS * A_HEAD_DIM, d), const),
                  pl.BlockSpec((tm, LANES), lambda i: (i, 0)),
                  pl.BlockSpec((tm, LANES), lambda i: (i, 0)),
                  pl.BlockSpec((half, tm), lambda i: (0, i)),
                  pl.BlockSpec((half, tm), lambda i: (0, i))],
        out_specs=[pl.BlockSpec((tm, P_WIDTH), lambda i: (i, 0)),
                   pl.BlockSpec((qrows, tm), lambda i: (0, i)),
                   pl.BlockSpec((vrows, tm), lambda i: (0, i))],
        out_shape=[jax.ShapeDtypeStruct((t, P_WIDTH), BF16),
                   jax.ShapeDtypeStruct((qrows, t), BF16),
                   jax.ShapeDtypeStruct((vrows, t), BF16)],
        name="in_proj",
    )(x, g, w, wqt, wvt, cos_a, sin_a, cos_t, sin_t)


def _mla_prep_kernel(cq_ref, ckv_ref, kr_ref, gq_ref, gkv_ref, wqt_ref, wk_ref, wvt_ref,
                     cos_ref, sin_ref, cost_ref, sint_ref, qt_ref, k_ref, vt_ref):
    half = B_ROPE // 2
    nt = (((1,), (1,)), ((), ()))
    cq = _rms(cq_ref[...].astype(F32), gq_ref[...]).astype(BF16)
    ckv = _rms(ckv_ref[...].astype(F32), gkv_ref[...]).astype(BF16)

    cos, sin = cos_ref[...], sin_ref[...]
    lane = lax.broadcasted_iota(jnp.int32, cos.shape, 1)
    first = (lane >= B_NOPE) & (lane < B_NOPE + half)
    kr = _rot_tile(pltpu.roll(kr_ref[:, LANES:].astype(F32), B_NOPE, 1), cos, sin, first, half)
    k = jnp.dot(ckv, wk_ref[...], preferred_element_type=F32)
    for hd in range(B_HEADS):
        sl = slice(hd * LANES, (hd + 1) * LANES)
        k_ref[:, sl] = (k[:, sl] + kr).astype(BF16)

    ct, st = cost_ref[...], sint_ref[...]
    qscale = (B_NOPE + B_ROPE) ** -0.5 * LOG2E
    qt = lax.dot_general(wqt_ref[...], cq, nt, preferred_element_type=F32)
    vt = lax.dot_general(wvt_ref[...], ckv, nt, preferred_element_type=F32)
    row = lax.broadcasted_iota(jnp.int32, (LANES, vt.shape[1]), 0)
    ones_row = (row == B_V).astype(F32)
    for hd in range(B_HEADS):
        r0 = hd * LANES
        x1 = qt[r0 + B_NOPE:r0 + B_NOPE + half]
        x2 = qt[r0 + B_NOPE + half:r0 + B_NOPE + B_ROPE]
        qh = jnp.concatenate([qt[r0:r0 + B_NOPE], x1 * ct - x2 * st, x2 * ct + x1 * st,
                              qt[r0 + B_NOPE + B_ROPE:r0 + LANES]], axis=0)
        qt_ref[r0:r0 + LANES, :] = (qh * qscale).astype(BF16)
        vt_ref[r0:r0 + LANES, :] = (vt[r0:r0 + LANES] + ones_row).astype(BF16)


def _mla_prep(p, gq, gkv, wqt, wk, wvt, cos_b, sin_b, cos_bt, sin_bt, batch):
    t = p.shape[0]
    s_len = t // batch
    tm = TM_PREP
    nt = s_len // tm
    hw = B_HEADS * LANES
    half = B_ROPE // 2
    const = lambda b, i: (0, 0)
    tok = lambda b, i: (b * nt + i, 0)
    return pl.pallas_call(
        _mla_prep_kernel,
        grid=(batch, nt),
        in_specs=[pl.BlockSpec((tm, B_Q_RANK), lambda b, i: (b * nt + i, P_CQ // B_Q_RANK)),
                  pl.BlockSpec((tm, B_KV_RANK), lambda b, i: (b * nt + i, P_CKV // B_KV_RANK)),
                  pl.BlockSpec((tm, 2 * LANES), lambda b, i: (b * nt + i, P_KA // (2 * LANES))),
                  pl.BlockSpec((1, B_Q_RANK), const),
                  pl.BlockSpec((1, B_KV_RANK), const),
                  pl.BlockSpec((hw, B_Q_RANK), const),
                  pl.BlockSpec((B_KV_RANK, hw), const),
                  pl.BlockSpec((hw, B_KV_RANK), const),
                  pl.BlockSpec((tm, LANES), tok),
                  pl.BlockSpec((tm, LANES), tok),
                  pl.BlockSpec((half, tm), lambda b, i: (A_HEAD_DIM // 2 // half, b * nt + i)),
                  pl.BlockSpec((half, tm), lambda b, i: (A_HEAD_DIM // 2 // half, b * nt + i))],
        out_specs=[pl.BlockSpec((None, hw, tm), lambda b, i: (b, 0, i)),
                   pl.BlockSpec((None, tm, hw), lambda b, i: (b, i, 0)),
                   pl.BlockSpec((None, hw, tm), lambda b, i: (b, 0, i))],
        out_shape=[jax.ShapeDtypeStruct((batch, hw, s_len), BF16),
                   jax.ShapeDtypeStruct((batch, s_len, hw), BF16),
                   jax.ShapeDtypeStruct((batch, hw, s_len), BF16)],
        name="mla_prep",
    )(p, p, p, gq, gkv, wqt, wk, wvt, cos_b, sin_b, cos_bt, sin_bt)


def _mla_attn_kernel(qt_ref, k_ref, vt_ref, o_ref, m_sc, acc_sc, s0_sc, s1_sc, *, tk):
    n_chunks = k_ref.shape[0] // tk
    m_sc[...] = jnp.full(m_sc.shape, -jnp.inf, F32)
    acc_sc[...] = jnp.zeros(acc_sc.shape, F32)
    heads = [slice(hd * LANES, (hd + 1) * LANES) for hd in range(2)]

    def keys(j):
        return pl.ds(pl.multiple_of(j * tk, tk), tk)

    def scores(j, dst):
        for hd, sl in enumerate(heads):
            dst[hd] = jnp.dot(k_ref[keys(j), sl], qt_ref[sl, :], preferred_element_type=F32)

    def accumulate(j, src):
        for hd, sl in enumerate(heads):
            st = src[hd]
            m_old = m_sc[hd]
            m_new = jnp.maximum(m_old, jnp.max(st, axis=0, keepdims=True))
            pt = jnp.exp2(st - m_new).astype(BF16)
            acc_sc[hd] = acc_sc[hd] * jnp.exp2(m_old - m_new) + jnp.dot(
                vt_ref[sl, keys(j)], pt, preferred_element_type=F32)
            m_sc[hd] = m_new

    scores(0, s0_sc)

    def body(jj, carry):
        j = 2 * jj
        scores(j + 1, s1_sc)
        accumulate(j, s0_sc)
        scores(j + 2, s0_sc)
        accumulate(j + 1, s1_sc)
        return carry

    lax.fori_loop(0, n_chunks // 2 - 1, body, 0)
    scores(n_chunks - 1, s1_sc)
    accumulate(n_chunks - 2, s0_sc)
    accumulate(n_chunks - 1, s1_sc)
    outs = []
    for hd in range(2):
        acc = acc_sc[hd]
        outs.append(acc[:B_V] * (1.0 / acc[B_V:B_V + 1]))
    o_ref[...] = jnp.concatenate(outs, axis=0).T.astype(BF16)


def _mla_attn(qt, k, vt):
    batch, s_len, hw = k.shape
    tq = TQ_MLA
    out = pl.pallas_call(
        functools.partial(_mla_attn_kernel, tk=TK_MLA),
        grid=(batch, B_HEADS // 2, s_len // tq),
        in_specs=[pl.BlockSpec((None, 2 * LANES, tq), lambda b, p, i: (b, p, i)),
                  pl.BlockSpec((None, s_len, 2 * LANES), lambda b, p, i: (b, 0, p)),
                  pl.BlockSpec((None, 2 * LANES, s_len), lambda b, p, i: (b, p, 0))],
        out_specs=pl.BlockSpec((None, tq, LANES), lambda b, p, i: (b, i, p)),
        out_shape=jax.ShapeDtypeStruct((batch, s_len, B_HEADS * B_V), BF16),
        scratch_shapes=[pltpu.VMEM((2, 1, tq), F32), pltpu.VMEM((2, LANES, tq), F32),
                        pltpu.VMEM((2, TK_MLA, tq), F32), pltpu.VMEM((2, TK_MLA, tq), F32)],
        name="mla_attn",
    )(qt, k, vt)
    return out.reshape(batch * s_len, B_HEADS * B_V)


def _win_attn_kernel(sink_ref, qt_ref, kp_ref, kc_ref, kn_ref, vp_ref, vc_ref, vn_ref, o_ref, *, s_len):
    n = pl.program_id(1)
    grp = A_HEADS // A_KV_HEADS
    dh = A_HEAD_DIM
    tq = kc_ref.shape[0]
    kj = lax.broadcasted_iota(jnp.int32, (tq + 2 * BLK, tq), 0)
    qi = lax.broadcasted_iota(jnp.int32, (tq + 2 * BLK, tq), 1)
    k_abs = n * tq - BLK + kj
    ok = (jnp.abs(kj - BLK - qi) <= A_WINDOW) & (k_abs >= 0) & (k_abs < s_len)
    ok = jnp.concatenate([ok] * grp, axis=1)
    lane_head = lax.broadcasted_iota(jnp.int32, (1, grp * tq), 1) // tq
    kband = jnp.concatenate([kp_ref[:, :LANES], kc_ref[:, :LANES], kn_ref[:, :LANES]], axis=0)
    zeros = jnp.zeros((dh, tq), BF16)
    heads = []
    for g in range(A_KV_HEADS):
        tiles = []
        for a in range(grp):
            qh = qt_ref[(g * grp + a) * dh:(g * grp + a + 1) * dh, :]
            tiles.append(jnp.concatenate([zeros] * g + [qh] + [zeros] * (A_KV_HEADS - 1 - g), axis=0))
        qg = jnp.concatenate(tiles, axis=1)
        st = jnp.where(ok, jnp.dot(kband, qg, preferred_element_type=F32), NEG_INF)
        sk = jnp.zeros((1, grp * tq), F32)
        for a in range(grp):
            sk = jnp.where(lane_head == a, sink_ref[g * grp + a] * LOG2E, sk)
        m = jnp.maximum(jnp.max(st, axis=0, keepdims=True), sk)
        pt = jnp.exp2(st - m).astype(BF16)
        rows = slice(g * LANES, (g + 1) * LANES)
        vband = jnp.concatenate([vp_ref[rows, :], vc_ref[rows, :], vn_ref[rows, :]], axis=1)
        ot = jnp.dot(vband, pt, preferred_element_type=F32)
        o = ot[:dh] * (1.0 / (ot[dh:dh + 1] + jnp.exp2(sk - m)))
        heads.extend(o[:, a * tq:(a + 1) * tq] for a in range(grp))
    for pair in range(A_HEADS // 2):
        o_ref[:, pair * LANES:(pair + 1) * LANES] = jnp.concatenate(
            [heads[2 * pair], heads[2 * pair + 1]], axis=0).T.astype(BF16)


def _win_attn(p, qt, vt, sink, batch):
    t = p.shape[0]
    s_len = t // batch
    tq = TQ_WIN
    nq, nb, r = s_len // tq, s_len // BLK, tq // BLK
    p3 = p.reshape(batch, s_len, P_WIDTH)
    kcol = P_KA // (2 * LANES)
    lo = lambda n: jnp.maximum(n * r - 1, 0)
    hi = lambda n: jnp.minimum((n + 1) * r, nb - 1)
    kedge = lambda f: pl.BlockSpec((None, BLK, 2 * LANES), lambda b, n, s: (b, f(n), kcol))
    vedge = lambda f: pl.BlockSpec((A_KV_HEADS * LANES, BLK), lambda b, n, s: (0, b * nb + f(n)))
    out = pl.pallas_call(
        functools.partial(_win_attn_kernel, s_len=s_len),
        grid_spec=pltpu.PrefetchScalarGridSpec(
            num_scalar_prefetch=1,
            grid=(batch, nq),
            in_specs=[pl.BlockSpec((A_HEADS * A_HEAD_DIM, tq), lambda b, n, s: (0, b * nq + n)),
                      kedge(lo), pl.BlockSpec((None, tq, 2 * LANES), lambda b, n, s: (b, n, kcol)), kedge(hi),
                      vedge(lo), pl.BlockSpec((A_KV_HEADS * LANES, tq), lambda b, n, s: (0, b * nq + n)), vedge(hi)],
            out_specs=pl.BlockSpec((None, tq, A_HEADS * A_HEAD_DIM), lambda b, n, s: (b, n, 0)),
        ),
        out_shape=jax.ShapeDtypeStruct((batch, s_len, A_HEADS * A_HEAD_DIM), BF16),
        name="win_attn",
    )(sink, qt, p3, p3, p3, vt, vt, vt)
    return out.reshape(t, A_HEADS * A_HEAD_DIM)


def _mem_kv_kernel(mem_ref, g_ref, w_ref, o_ref):
    mem_n = _rms(mem_ref[...], g_ref[...]).astype(BF16)
    o_ref[...] = jnp.dot(mem_n, w_ref[...], preferred_element_type=F32).astype(BF16)


def _mem_kv(mem2, g_mem, w_mem_kv):
    depth, d, width = w_mem_kv.shape
    rows = mem2.shape[0]
    return pl.pallas_call(
        _mem_kv_kernel,
        grid=(depth,),
        in_specs=[pl.BlockSpec((rows, d), lambda l: (0, 0)),
                  pl.BlockSpec((1, d), lambda l: (0, 0)),
                  pl.BlockSpec((None, d, width), lambda l: (l, 0, 0))],
        out_specs=pl.BlockSpec((None, rows, width), lambda l: (l, 0, 0)),
        out_shape=jax.ShapeDtypeStruct((depth, rows, width), BF16),
        name="mem_kv",
    )(mem2, g_mem, w_mem_kv)


def _mem_attn_kernel(q_ref, kv_ref, o_ref):
    dh = M_HEAD_DIM
    for hd in range(M_HEADS):
        q = q_ref[:, hd * dh:(hd + 1) * dh]
        k = kv_ref[:, hd * dh:(hd + 1) * dh]
        v = kv_ref[:, (M_HEADS + hd) * dh:(M_HEADS + hd + 1) * dh]
        sc = lax.dot_general(q, k, (((1,), (1,)), ((), ())), preferred_element_type=F32)
        e = jnp.exp(sc - jnp.max(sc, axis=-1, keepdims=True))
        pr = (e * (1.0 / jnp.sum(e, axis=-1, keepdims=True))).astype(BF16)
        o_ref[:, hd * dh:(hd + 1) * dh] = jnp.dot(pr, v, preferred_element_type=F32).astype(BF16)


def _mem_attn(p, mem_kv_l, batch):
    t = p.shape[0]
    s_len = t // batch
    tm = TM_MEM
    n_mem = mem_kv_l.shape[0] // batch
    width = M_HEADS * M_HEAD_DIM
    p3 = p.reshape(batch, s_len, P_WIDTH)
    kv3 = mem_kv_l.reshape(batch, n_mem, 2 * width)
    out = pl.pallas_call(
        _mem_attn_kernel,
        grid=(batch, s_len // tm),
        in_specs=[pl.BlockSpec((None, tm, width), lambda b, i: (b, i, P_QM // width)),
                  pl.BlockSpec((None, n_mem, 2 * width), lambda b, i: (b, 0, 0))],
        out_specs=pl.BlockSpec((None, tm, width), lambda b, i: (b, i, 0)),
        out_shape=jax.ShapeDtypeStruct((batch, s_len, width), BF16),
        name="mem_attn",
    )(p3, kv3)
    return out.reshape(t, width)


def _mix_kernel(x_ref, ya_ref, yb_ref, ym_ref, g_ref, wb_ref, wo_ref, o_ref):
    d = x_ref.shape[1]
    mixed = None
    for i, y_ref in enumerate((ya_ref, yb_ref, ym_ref)):
        gate = jax.nn.sigmoid(g_ref[:, i * d:(i + 1) * d].astype(F32))
        term = gate * jnp.dot(y_ref[...], wb_ref[i], preferred_element_type=F32)
        mixed = term if mixed is None else mixed + term
    o_ref[...] = x_ref[...] + jnp.dot(mixed.astype(BF16), wo_ref[...], preferred_element_type=F32)


def _mix(x, ya, yb, ym, p, wb, wo):
    t, d = x.shape
    tm = TM_MIX
    bw = ya.shape[1]
    row = lambda i: (i, 0)
    return pl.pallas_call(
        _mix_kernel,
        grid=(t // tm,),
        in_specs=[pl.BlockSpec((tm, d), row),
                  pl.BlockSpec((tm, bw), row), pl.BlockSpec((tm, bw), row), pl.BlockSpec((tm, bw), row),
                  pl.BlockSpec((tm, 3 * d), lambda i: (i, P_GATES)),
                  pl.BlockSpec((3, bw, d), lambda i: (0, 0, 0)),
                  pl.BlockSpec((d, d), lambda i: (0, 0))],
        out_specs=pl.BlockSpec((tm, d), row),
        out_shape=jax.ShapeDtypeStruct((t, d), F32),
        name="mix_out",
    )(x, ya, yb, ym, p, wb, wo)


def _swiglu_step(h, wg, wu, wd):
    gate = jnp.dot(h, wg, preferred_element_type=F32)
    up = jnp.dot(h, wu, preferred_element_type=F32)
    act = (gate * jax.nn.sigmoid(gate) * up).astype(BF16)
    return jnp.dot(act, wd, preferred_element_type=F32)


def _ffn_kernel(x_ref, g_ref, wg_ref, wu_ref, wd_ref, o_ref, h_sc, acc_sc):
    j = pl.program_id(1)

    @pl.when(j == 0)
    def _():
        h_sc[...] = _rms(x_ref[...], g_ref[...]).astype(BF16)
        acc_sc[...] = jnp.zeros(acc_sc.shape, F32)

    acc_sc[...] += _swiglu_step(h_sc[...], wg_ref[...], wu_ref[...], wd_ref[...])

    @pl.when(j == pl.num_programs(1) - 1)
    def _():
        o_ref[...] = x_ref[...] + acc_sc[...]


def _ffn(x, g, wg, wu, wd):
    t, d = x.shape
    f = wg.shape[1]
    tm, tf = TM_FFN, TF_FFN
    return pl.pallas_call(
        _ffn_kernel,
        grid=(t // tm, f // tf),
        in_specs=[pl.BlockSpec((tm, d), lambda i, j: (i, 0)),
                  pl.BlockSpec((1, d), lambda i, j: (0, 0)),
                  pl.BlockSpec((d, tf), lambda i, j: (0, j)),
                  pl.BlockSpec((d, tf), lambda i, j: (0, j)),
                  pl.BlockSpec((tf, d), lambda i, j: (j, 0))],
        out_specs=pl.BlockSpec((tm, d), lambda i, j: (i, 0)),
        out_shape=jax.ShapeDtypeStruct((t, d), F32),
        scratch_shapes=[pltpu.VMEM((tm, d), BF16), pltpu.VMEM((tm, d), F32)],
        name="ffn_dense",
    )(x, g, wg, wu, wd)


def _router_kernel(x_ref, g_ref, wr_ref, idx_ref, wgt_ref, xt_ref):
    x = x_ref[...]
    xt_ref[...] = x.reshape(xt_ref.shape)
    h = _rms(x, g_ref[...])
    logits = lax.dot_general(wr_ref[...], h, (((1,), (1,)), ((), ())),
                             precision=lax.Precision.HIGHEST, preferred_element_type=F32)
    e_id = lax.broadcasted_iota(jnp.int32, logits.shape, 0)
    m1 = jnp.max(logits, axis=0, keepdims=True)
    i1 = jnp.min(jnp.where(logits == m1, e_id, N_EXPERTS), axis=0, keepdims=True)
    rest = jnp.where(e_id == i1, -jnp.inf, logits)
    m2 = jnp.max(rest, axis=0, keepdims=True)
    i2 = jnp.min(jnp.where(rest == m2, e_id, N_EXPERTS), axis=0, keepdims=True)
    e2 = jnp.exp(m2 - m1)
    w1 = 1.0 / (1.0 + e2)
    idx_ref[...] = jnp.concatenate([i1, i2], axis=0)
    wgt_ref[...] = jnp.concatenate([w1, e2 * w1], axis=0)


def _router(x, g, wr_t):
    t, d = x.shape
    tm = TM_ROUTE
    return pl.pallas_call(
        _router_kernel,
        grid=(t // tm,),
        in_specs=[pl.BlockSpec((tm, d), lambda i: (i, 0)),
                  pl.BlockSpec((1, d), lambda i: (0, 0)),
                  pl.BlockSpec((N_EXPERTS, d), lambda i: (0, 0))],
        out_specs=[pl.BlockSpec((TOP_K, tm), lambda i: (0, i)),
                   pl.BlockSpec((TOP_K, tm), lambda i: (0, i)),
                   pl.BlockSpec((tm * SUBLANES, LANES), lambda i: (i, 0))],
        out_shape=[jax.ShapeDtypeStruct((TOP_K, t), jnp.int32),
                   jax.ShapeDtypeStruct((TOP_K, t), F32),
                   jax.ShapeDtypeStruct((t * SUBLANES, LANES), F32)],
        name="moe_router",
    )(x, g, wr_t)


def _moe_rows_per_step(n_steps):
    return -(-TM_EXP // (n_steps * SUBLANES)) * SUBLANES


def _slot_tables(top_i, n_tok, buf_rows):
    n_asg = n_tok * TOP_K
    flat_e = top_i.T.reshape(n_asg)
    order = jnp.argsort(flat_e, stable=True).astype(jnp.int32)
    counts = jnp.sum(flat_e[:, None] == jnp.arange(N_EXPERTS, dtype=jnp.int32)[None, :], axis=0, dtype=jnp.int32)
    padded = (counts + TM_EXP - 1) // TM_EXP * TM_EXP
    pad_end = jnp.cumsum(padded)
    start = jnp.cumsum(counts) - counts
    pads_before = jnp.cumsum(padded - counts) - (padded - counts)
    n_slots = n_asg + N_EXPERTS * TM_EXP
    n_blk = n_slots // TM_EXP
    extra = buf_rows - TM_EXP
    blk_e = jnp.minimum(jnp.searchsorted(pad_end, jnp.arange(n_blk, dtype=jnp.int32) * TM_EXP, side='right'),
                        N_EXPERTS - 1).astype(jnp.int32)
    slot_e = jnp.repeat(blk_e, TM_EXP)
    rank = jnp.arange(n_slots, dtype=jnp.int32) - (pad_end - padded)[slot_e]
    is_pad = rank >= counts[slot_e]
    asg = order[jnp.clip(start[slot_e] + rank, 0, n_asg - 1)]
    slot_tok = jnp.where(is_pad, 0, asg // TOP_K)
    slot_row = jnp.where(is_pad, n_asg + pads_before[slot_e] + rank - counts[slot_e],
                         (asg % TOP_K) * n_tok + slot_tok)
    gather_tok = jnp.concatenate([slot_tok.reshape(n_blk, TM_EXP), jnp.zeros((n_blk, extra), jnp.int32)], axis=1)
    spare = n_slots + jnp.arange((n_blk + 1) * buf_rows - n_slots, dtype=jnp.int32)
    lead, over = spare[:TM_EXP], spare[TM_EXP:].reshape(n_blk + 1, extra)
    scatter_row = jnp.concatenate(
        [jnp.concatenate([lead[None], slot_row.reshape(n_blk, TM_EXP)], axis=0), over], axis=1)
    return blk_e, gather_tok.reshape(-1).astype(jnp.int32), scatter_row.reshape(-1).astype(jnp.int32)


def _moe_ffn_kernel(blk_e_ref, gtok_ref, srow_ref, x_hbm, g_ref, wg_ref, wu_ref, wd_ref, o_hbm,
                    xg_sc, ys_sc, h_sc, acc_sc, gsem, ssem, *, rows_per_step):
    i, j = pl.program_id(0), pl.program_id(1)
    n_blk, n_steps = pl.num_programs(0), pl.num_programs(1)
    rows, d = h_sc.shape
    tr = d // LANES
    buf_rows = xg_sc.shape[1] // tr
    slot = i % 2
    other = 1 - slot

    def tile(ref, k):
        return ref.at[pl.ds(pl.multiple_of(k * tr, tr), tr), :]

    def gathered(s):
        return pltpu.make_async_copy(x_hbm.at[pl.ds(0, buf_rows * tr), :], xg_sc.at[s], gsem.at[s])

    def scattered(s):
        return pltpu.make_async_copy(ys_sc.at[s], o_hbm.at[pl.ds(0, buf_rows * tr), :], ssem.at[s])

    def get_row(s, k, entry):
        return pltpu.make_async_copy(tile(x_hbm, gtok_ref[entry]), tile(xg_sc.at[s], k), gsem.at[s])

    def put_row(s, k, entry):
        return pltpu.make_async_copy(tile(ys_sc.at[s], k), tile(o_hbm, srow_ref[entry]), ssem.at[s])

    @pl.when((i == 0) & (j == 0))
    def _():
        ys_sc[...] = jnp.zeros(ys_sc.shape, F32)

        def start(k, c):
            get_row(0, k, k).start()
            return c

        lax.fori_loop(0, buf_rows, start, 0)

    @pl.when(j == 0)
    def _():
        gathered(slot).wait()
        for s in range(2):
            @pl.when(slot == s)
            def _():
                xs = [xg_sc[s, pl.ds(c, rows, stride=tr), :] for c in range(tr)]
                ms = sum(jnp.sum(x * x, axis=-1, keepdims=True) for x in xs) * (1.0 / d)
                inv = lax.rsqrt(ms + EPS)
                for c, x in enumerate(xs):
                    cols = slice(c * LANES, (c + 1) * LANES)
                    h_sc[:, cols] = (x * inv * g_ref[:, cols]).astype(BF16)

        acc_sc[...] = jnp.zeros(acc_sc.shape, F32)

    nxt = jnp.minimum(i + 1, n_blk - 1)
    for r in range(rows_per_step):
        k = j * rows_per_step + r
        get_row(other, k, nxt * buf_rows + k).start(priority=ROW_DMA_PRIORITY)
        put_row(other, k, i * buf_rows + k).start(priority=ROW_DMA_PRIORITY)

    acc_sc[...] += _swiglu_step(h_sc[...], wg_ref[...], wu_ref[...], wd_ref[...])

    @pl.when(j == n_steps - 1)
    def _():
        @pl.when(i > 0)
        def _():
            scattered(slot).wait()

        for s in range(2):
            @pl.when(slot == s)
            def _():
                for c in range(tr):
                    ys_sc[s, pl.ds(c, rows, stride=tr), :] = acc_sc[:, c * LANES:(c + 1) * LANES]

    @pl.when((i == n_blk - 1) & (j == n_steps - 1))
    def _():
        def start(k, c):
            put_row(slot, k, n_blk * buf_rows + k).start()
            return c

        lax.fori_loop(0, buf_rows, start, 0)
        scattered(other).wait()
        scattered(slot).wait()
        gathered(other).wait()


def _moe_ffn(x_tiles, g, wg, wu, wd, blk_e, gather_tok, scatter_row):
    d, f = wg.shape[1], wg.shape[2]
    tr = d // LANES
    assert tr % SUBLANES == 0
    tm, tf = TM_EXP, TF_FFN
    n_steps = f // tf
    rows_per_step = _moe_rows_per_step(n_steps)
    buf_rows = rows_per_step * n_steps
    n_blk = gather_tok.shape[0] // buf_rows
    return pl.pallas_call(
        functools.partial(_moe_ffn_kernel, rows_per_step=rows_per_step),
        grid_spec=pltpu.PrefetchScalarGridSpec(
            num_scalar_prefetch=3,
            grid=(n_blk, n_steps),
            in_specs=[pl.BlockSpec(memory_space=pl.ANY),
                      pl.BlockSpec((1, d), lambda i, j, be, gt, sr: (0, 0)),
                      pl.BlockSpec((None, d, tf), lambda i, j, be, gt, sr: (be[i], 0, j)),
                      pl.BlockSpec((None, d, tf), lambda i, j, be, gt, sr: (be[i], 0, j)),
                      pl.BlockSpec((None, tf, d), lambda i, j, be, gt, sr: (be[i], j, 0))],
            out_specs=pl.BlockSpec(memory_space=pl.ANY),
            scratch_shapes=[pltpu.VMEM((2, buf_rows * tr, LANES), F32), pltpu.VMEM((2, buf_rows * tr, LANES), F32),
                            pltpu.VMEM((tm, d), BF16), pltpu.VMEM((tm, d), F32),
                            pltpu.SemaphoreType.DMA((2,)), pltpu.SemaphoreType.DMA((2,))],
        ),
        out_shape=jax.ShapeDtypeStruct((scatter_row.shape[0] * tr, LANES), F32),
        name="moe_ffn",
    )(blk_e, gather_tok, scatter_row, x_tiles, g, wg, wu, wd)


def _moe_combine_kernel(x_ref, w_ref, y0_ref, y1_ref, gf_ref, o_ref, *, final_norm):
    tm, d = x_ref.shape
    y0, y1 = y0_ref[...].reshape(tm, d), y1_ref[...].reshape(tm, d)
    out = x_ref[...] + w_ref[:, 0:1] * y0 + w_ref[:, 1:2] * y1
    if final_norm:
        out = _rms(out, gf_ref[...])
    o_ref[...] = out


def _moe_combine(x, top_w, y, g_final, final_norm):
    t, d = x.shape
    tr = d // LANES
    tm = TM_COMB
    nt = t // tm
    return pl.pallas_call(
        functools.partial(_moe_combine_kernel, final_norm=final_norm),
        grid=(nt,),
        in_specs=[pl.BlockSpec((tm, d), lambda i: (i, 0)),
                  pl.BlockSpec((tm, TOP_K), lambda i: (i, 0)),
                  pl.BlockSpec((tm * tr, LANES), lambda i: (i, 0)),
                  pl.BlockSpec((tm * tr, LANES), lambda i: (nt + i, 0)),
                  pl.BlockSpec((1, d), lambda i: (0, 0))],
        out_specs=pl.BlockSpec((tm, d), lambda i: (i, 0)),
        out_shape=jax.ShapeDtypeStruct((t, d), F32),
        name="moe_combine",
    )(x, top_w, y, y, g_final)


def _final_norm_kernel(x_ref, g_ref, o_ref):
    o_ref[...] = _rms(x_ref[...], g_ref[...])


def _final_norm(x, g):
    t, d = x.shape
    tm = 1024
    return pl.pallas_call(
        _final_norm_kernel,
        grid=(t // tm,),
        in_specs=[pl.BlockSpec((tm, d), lambda i: (i, 0)), pl.BlockSpec((1, d), lambda i: (0, 0))],
        out_specs=pl.BlockSpec((tm, d), lambda i: (i, 0)),
        out_shape=jax.ShapeDtypeStruct((t, d), F32),
        name="final_norm",
    )(x, g)


def _relayout_w_in(w_in):
    qa, ka, va, cq, ckv, kr, qm, gates = jnp.split(
        w_in, [512, 640, 768, 1024, 1152, 1184, 1696], axis=-1)
    pad = jnp.zeros(w_in.shape[:2] + (LANES - B_ROPE,), w_in.dtype)
    w_tok = jnp.concatenate([gates, qm, ka, kr, pad, cq, ckv], axis=-1).astype(BF16)
    return w_tok, jnp.swapaxes(qa, 1, 2).astype(BF16), jnp.swapaxes(va, 1, 2).astype(BF16)


def _relayout_latent(w_uq, w_ukv):
    depth = w_uq.shape[0]
    q = w_uq.reshape(depth, B_Q_RANK, B_HEADS, B_NOPE + B_ROPE)
    q = jnp.pad(q, ((0, 0), (0, 0), (0, 0), (0, LANES - B_NOPE - B_ROPE)))
    kv = w_ukv.reshape(depth, B_KV_RANK, B_HEADS, B_NOPE + B_V)
    k = jnp.pad(kv[..., :B_NOPE], ((0, 0), (0, 0), (0, 0), (0, LANES - B_NOPE)))
    v = jnp.pad(kv[..., B_NOPE:], ((0, 0), (0, 0), (0, 0), (0, LANES - B_V)))
    flat = lambda a, r: a.reshape(depth, r, B_HEADS * LANES).astype(BF16)
    return (jnp.swapaxes(flat(q, B_Q_RANK), 1, 2), flat(k, B_KV_RANK),
            jnp.swapaxes(flat(v, B_KV_RANK), 1, 2))


def kernel(x, mem, positions, g_mix, w_in, g_q_lat, w_uq, g_kv_lat, w_ukv, sink, w_mem_kv, w_branch, w_out,
           g_ffn, w_gate_dense, w_up_dense, w_down_dense, w_router, w_gate_exp, w_up_exp, w_down_exp,
           g_mem, g_final):
    batch, s_len, d = x.shape
    depth = w_in.shape[0]
    t = batch * s_len
    row = lambda a: a.reshape(1, -1).astype(F32)

    w_in_p, wqa_t, wva_t = _relayout_w_in(w_in)
    wq_p, wk_p, wv_p = _relayout_latent(w_uq, w_ukv)
    w_branch_b, w_out_b = w_branch.astype(BF16), w_out.astype(BF16)
    wgd, wud, wdd = w_gate_dense.astype(BF16), w_up_dense.astype(BF16), w_down_dense.astype(BF16)
    wge, wue, wde = w_gate_exp.astype(BF16), w_up_exp.astype(BF16), w_down_exp.astype(BF16)
    wr_t = jnp.swapaxes(w_router, 1, 2).astype(F32)

    cos_a, sin_a, cos_b, sin_b, cos_t, sin_t = _rope_tables(positions)
    mem_kv = _mem_kv(mem.reshape(batch * mem.shape[1], d).astype(F32), row(g_mem), w_mem_kv.astype(BF16))

    xt = x.reshape(t, d).astype(F32)
    for l in range(depth):
        p, qa_t, va_t = _in_proj(xt, row(g_mix[l]), w_in_p[l], wqa_t[l], wva_t[l], cos_a, sin_a, cos_t, sin_t)
        y_a = _win_attn(p, qa_t, va_t, sink[l].astype(F32), batch)
        qt, k, vt = _mla_prep(p, row(g_q_lat[l]), row(g_kv_lat[l]), wq_p[l], wk_p[l], wv_p[l],
                              cos_b, sin_b, cos_t, sin_t, batch)
        y_b = _mla_attn(qt, k, vt)
        y_m = _mem_attn(p, mem_kv[l], batch)
        xt = _mix(xt, y_a, y_b, y_m, p, w_branch_b[l], w_out_b[l])
        i = l // 2
        if l % 2 == 0:
            xt = _ffn(xt, row(g_ffn[l]), wgd[i], wud[i], wdd[i])
            if l == depth - 1:
                xt = _final_norm(xt, row(g_final))
        else:
            top_i, top_w, x_tiles = _router(xt, row(g_ffn[l]), wr_t[i])
            buf_rows = _moe_rows_per_step(wge.shape[3] // TF_FFN) * (wge.shape[3] // TF_FFN)
            blk_e, gather_tok, scatter_row = _slot_tables(top_i, t, buf_rows)
            y = _moe_ffn(x_tiles, row(g_ffn[l]), wge[i], wue[i], wde[i], blk_e, gather_tok, scatter_row)
            xt = _moe_combine(xt, top_w.T, y, row(g_final), final_norm=(l == depth - 1))
    return xt.reshape(batch, s_len, d).astype(x.dtype)
```

```python
import functools

import jax
import jax.numpy as jnp
from jax import lax
from jax.experimental import pallas as pl
from jax.experimental.pallas import tpu as pltpu

F32 = jnp.float32
BF16 = jnp.bfloat16

ROPE_THETA = 10000.0
EPS = 1e-6
NEG_INF = -1e30
LOG2E = 1.4426950408889634

A_HEADS, A_KV_HEADS, A_HEAD_DIM, A_WINDOW = 8, 2, 64, 128
B_HEADS, B_Q_RANK, B_KV_RANK, B_NOPE, B_ROPE, B_V = 8, 256, 128, 64, 32, 64
M_HEADS, M_HEAD_DIM = 4, 128
N_EXPERTS, TOP_K = 8, 2
LANES = 128
SUBLANES = 8
BLK = 128

P_GATES = 0
P_QM = 3072
P_KA = 3584
P_CQ = 3840
P_CKV = 4096
P_WIDTH = 4224

TM_PROJ = 512
TM_PREP = 512
TM_MIX = 512
TM_MEM = 512
TM_FFN = 1024
TF_FFN = 512
TQ_WIN = 256
TQ_MLA = 2048
TK_MLA = 512
TM_ROUTE = 1024
TM_EXP = 1024
TM_COMB = 512
DMA_QUEUES = 2


def _rms(x, g):
    return x * lax.rsqrt(jnp.mean(x * x, axis=-1, keepdims=True) + EPS) * g


def _rot_tile(x, cos, sin_signed, first, shift):
    up = pltpu.roll(x, LANES - shift, 1)
    dn = pltpu.roll(x, shift, 1)
    return x * cos + jnp.where(first, up, dn) * sin_signed


def _rope_table_kernel(pos_ref, posr_ref, par_ref, invc_ref, ca_ref, sa_ref, cb_ref, sb_ref, cbt_ref, sbt_ref):
    pos = pos_ref[...]
    ang_a = pos * par_ref[0:1, :]
    ca_ref[...] = jnp.cos(ang_a)
    sa_ref[...] = jnp.sin(ang_a) * par_ref[1:2, :]
    ang_b = pos * par_ref[2:3, :]
    cb_ref[...] = jnp.cos(ang_b)
    sb_ref[...] = jnp.sin(ang_b) * par_ref[3:4, :]
    ang_t = invc_ref[...] * posr_ref[...]
    cbt_ref[...] = jnp.cos(ang_t)
    sbt_ref[...] = jnp.sin(ang_t)


def _rope_tables(positions):
    t = positions.size
    tm = 2048
    lane = jnp.arange(LANES)
    half_a = A_HEAD_DIM // 2
    inv_a = jnp.power(F32(ROPE_THETA), -jnp.arange(half_a, dtype=F32) * (2.0 / A_HEAD_DIM))
    inv_a = inv_a[lane % half_a]
    sgn_a = jnp.where((lane % A_HEAD_DIM) < half_a, -1.0, 1.0).astype(F32)
    half_b = B_ROPE // 2
    inv_b16 = jnp.power(F32(ROPE_THETA), -jnp.arange(half_b, dtype=F32) * (2.0 / B_ROPE))
    in_rope = (lane >= B_NOPE) & (lane < B_NOPE + B_ROPE)
    inv_b = jnp.where(in_rope, inv_b16[(lane - B_NOPE) % half_b], 0.0).astype(F32)
    sgn_b = jnp.where(in_rope, jnp.where(lane < B_NOPE + half_b, -1.0, 1.0), 0.0).astype(F32)
    par = jnp.zeros((8, LANES), F32).at[0].set(inv_a).at[1].set(sgn_a).at[2].set(inv_b).at[3].set(sgn_b)
    pos = positions.reshape(t, 1).astype(F32)
    tab = jax.ShapeDtypeStruct((t, LANES), F32)
    n_freq = half_a + half_b
    inv_col = jnp.concatenate([inv_a[:half_a], inv_b16]).reshape(n_freq, 1)
    tab_t = jax.ShapeDtypeStruct((n_freq, t), F32)
    return pl.pallas_call(
        _rope_table_kernel,
        grid=(t // tm,),
        in_specs=[pl.BlockSpec((tm, 1), lambda i: (i, 0)),
                  pl.BlockSpec((1, tm), lambda i: (0, i)),
                  pl.BlockSpec((8, LANES), lambda i: (0, 0)),
                  pl.BlockSpec((n_freq, 1), lambda i: (0, 0))],
        out_specs=[pl.BlockSpec((tm, LANES), lambda i: (i, 0))] * 4
        + [pl.BlockSpec((n_freq, tm), lambda i: (0, i))] * 2,
        out_shape=[tab] * 4 + [tab_t] * 2,
        name="rope_tables",
    )(pos, pos.reshape(1, t), par, inv_col)


def _in_proj_kernel(x_ref, g_ref, w_ref, wqt_ref, wvt_ref, cos_ref, sin_ref, cost_ref, sint_ref,
                    o_ref, qt_ref, vt_ref):
    h = _rms(x_ref[...], g_ref[...]).astype(BF16)
    half = A_HEAD_DIM // 2

    def proj(c0, c1):
        return jnp.dot(h, w_ref[:, c0:c1], preferred_element_type=F32)

    for c0 in range(P_GATES, P_QM, 512):
        o_ref[:, c0:c0 + 512] = proj(c0, c0 + 512).astype(BF16)
    o_ref[:, P_QM:P_KA] = (proj(P_QM, P_KA) * (M_HEAD_DIM ** -0.5)).astype(BF16)
    cos, sin = cos_ref[...], sin_ref[...]
    lane = lax.broadcasted_iota(jnp.int32, cos.shape, 1)
    ka = proj(P_KA, P_CQ)
    o_ref[:, P_KA:P_KA + LANES] = _rot_tile(ka[:, :LANES], cos, sin, (lane % A_HEAD_DIM) < half, half).astype(BF16)
    o_ref[:, P_KA + LANES:P_CQ] = ka[:, LANES:].astype(BF16)
    o_ref[:, P_CQ:P_WIDTH] = proj(P_CQ, P_WIDTH).astype(BF16)

    nt = (((1,), (1,)), ((), ()))
    ct, st = cost_ref[...], sint_ref[...]
    qscale = A_HEAD_DIM ** -0.5 * LOG2E
    qt = lax.dot_general(wqt_ref[...], h, nt, preferred_element_type=F32)
    for hd in range(A_HEADS):
        r0 = hd * A_HEAD_DIM
        x1, x2 = qt[r0:r0 + half], qt[r0 + half:r0 + A_HEAD_DIM]
        qt_ref[r0:r0 + half, :] = ((x1 * ct - x2 * st) * qscale).astype(BF16)
        qt_ref[r0 + half:r0 + A_HEAD_DIM, :] = ((x2 * ct + x1 * st) * qscale).astype(BF16)
    vt = lax.dot_general(wvt_ref[...], h, nt, preferred_element_type=F32)
    row = lax.broadcasted_iota(jnp.int32, (LANES - A_HEAD_DIM, vt.shape[1]), 0)
    ones_pad = (row == 0).astype(BF16)
    for g in range(A_KV_HEADS):
        vt_ref[g * LANES:g * LANES + A_HEAD_DIM, :] = vt[g * A_HEAD_DIM:(g + 1) * A_HEAD_DIM].astype(BF16)
        vt_ref[g * LANES + A_HEAD_DIM:(g + 1) * LANES, :] = ones_pad


def _in_proj(x, g, w, wqt, wvt, cos_a, sin_a, cos_t, sin_t):
    t, d = x.shape
    tm = TM_PROJ
    half = A_HEAD_DIM // 2
    qrows, vrows = A_HEADS * A_HEAD_DIM, A_KV_HEADS * LANES
    const = lambda i: (0, 0)
    return pl.pallas_call(
        _in_proj_kernel,
        grid=(t // tm,),
        in_specs=[pl.BlockSpec((tm, d), lambda i: (i, 0)),
                  pl.BlockSpec((1, d), const),
                  pl.BlockSpec((d, P_WIDTH), const),
                  pl.BlockSpec((qrows, d), const),
                  pl.BlockSpec((A_KV_HEADS * A_HEAD_DIM, d), const),
                  pl.BlockSpec((tm, LANES), lambda i: (i, 0)),
                  pl.BlockSpec((tm, LANES), lambda i: (i, 0)),
                  pl.BlockSpec((half, tm), lambda i: (0, i)),
                  pl.BlockSpec((half, tm), lambda i: (0, i))],
        out_specs=[pl.BlockSpec((tm, P_WIDTH), lambda i: (i, 0)),
                   pl.BlockSpec((qrows, tm), lambda i: (0, i)),
                   pl.BlockSpec((vrows, tm), lambda i: (0, i))],
        out_shape=[jax.ShapeDtypeStruct((t, P_WIDTH), BF16),
                   jax.ShapeDtypeStruct((qrows, t), BF16),
                   jax.ShapeDtypeStruct((vrows, t), BF16)],
        name="in_proj",
    )(x, g, w, wqt, wvt, cos_a, sin_a, cos_t, sin_t)


def _mla_prep_kernel(cq_ref, ckv_ref, kr_ref, gq_ref, gkv_ref, wqt_ref, wk_ref, wvt_ref,
                     cos_ref, sin_ref, cost_ref, sint_ref, qt_ref, k_ref, vt_ref):
    half = B_ROPE // 2
    nt = (((1,), (1,)), ((), ()))
    cq = _rms(cq_ref[...].astype(F32), gq_ref[...]).astype(BF16)
    ckv = _rms(ckv_ref[...].astype(F32), gkv_ref[...]).astype(BF16)

    cos, sin = cos_ref[...], sin_ref[...]
    lane = lax.broadcasted_iota(jnp.int32, cos.shape, 1)
    first = (lane >= B_NOPE) & (lane < B_NOPE + half)
    kr = _rot_tile(pltpu.roll(kr_ref[:, LANES:].astype(F32), B_NOPE, 1), cos, sin, first, half)
    k = jnp.dot(ckv, wk_ref[...], preferred_element_type=F32)
    for hd in range(B_HEADS):
        sl = slice(hd * LANES, (hd + 1) * LANES)
        k_ref[:, sl] = (k[:, sl] + kr).astype(BF16)

    ct, st = cost_ref[...], sint_ref[...]
    qscale = (B_NOPE + B_ROPE) ** -0.5 * LOG2E
    qt = lax.dot_general(wqt_ref[...], cq, nt, preferred_element_type=F32)
    vt = lax.dot_general(wvt_ref[...], ckv, nt, preferred_element_type=F32)
    row = lax.broadcasted_iota(jnp.int32, (LANES, vt.shape[1]), 0)
    ones_row = (row == B_V).astype(F32)
    for hd in range(B_HEADS):
        r0 = hd * LANES
        x1 = qt[r0 + B_NOPE:r0 + B_NOPE + half]
        x2 = qt[r0 + B_NOPE + half:r0 + B_NOPE + B_ROPE]
        qh = jnp.concatenate([qt[r0:r0 + B_NOPE], x1 * ct - x2 * st, x2 * ct + x1 * st,
                              qt[r0 + B_NOPE + B_ROPE:r0 + LANES]], axis=0)
        qt_ref[r0:r0 + LANES, :] = (qh * qscale).astype(BF16)
        vt_ref[r0:r0 + LANES, :] = (vt[r0:r0 + LANES] + ones_row).astype(BF16)


def _mla_prep(p, gq, gkv, wqt, wk, wvt, cos_b, sin_b, cos_bt, sin_bt, batch):
    t = p.shape[0]
    s_len = t // batch
    tm = TM_PREP
    nt = s_len // tm
    hw = B_HEADS * LANES
    half = B_ROPE // 2
    const = lambda b, i: (0, 0)
    tok = lambda b, i: (b * nt + i, 0)
    return pl.pallas_call(
        _mla_prep_kernel,
        grid=(batch, nt),
        in_specs=[pl.BlockSpec((tm, B_Q_RANK), lambda b, i: (b * nt + i, P_CQ // B_Q_RANK)),
                  pl.BlockSpec((tm, B_KV_RANK), lambda b, i: (b * nt + i, P_CKV // B_KV_RANK)),
                  pl.BlockSpec((tm, 2 * LANES), lambda b, i: (b * nt + i, P_KA // (2 * LANES))),
                  pl.BlockSpec((1, B_Q_RANK), const),
                  pl.BlockSpec((1, B_KV_RANK), const),
                  pl.BlockSpec((hw, B_Q_RANK), const),
                  pl.BlockSpec((B_KV_RANK, hw), const),
                  pl.BlockSpec((hw, B_KV_RANK), const),
                  pl.BlockSpec((tm, LANES), tok),
                  pl.BlockSpec((tm, LANES), tok),
                  pl.BlockSpec((half, tm), lambda b, i: (A_HEAD_DIM // 2 // half, b * nt + i)),
                  pl.BlockSpec((half, tm), lambda b, i: (A_HEAD_DIM // 2 // half, b * nt + i))],
        out_specs=[pl.BlockSpec((None, hw, tm), lambda b, i: (b, 0, i)),
                   pl.BlockSpec((None, tm, hw), lambda b, i: (b, i, 0)),
                   pl.BlockSpec((None, hw, tm), lambda b, i: (b, 0, i))],
        out_shape=[jax.ShapeDtypeStruct((batch, hw, s_len), BF16),
                   jax.ShapeDtypeStruct((batch, s_len, hw), BF16),
                   jax.ShapeDtypeStruct((batch, hw, s_len), BF16)],
        name="mla_prep",
    )(p, p, p, gq, gkv, wqt, wk, wvt, cos_b, sin_b, cos_bt, sin_bt)


def _mla_attn_kernel(qt_ref, k_ref, vt_ref, o_ref, m_sc, acc_sc, s0_sc, s1_sc, *, tk):
    n_chunks = k_ref.shape[0] // tk
    m_sc[...] = jnp.full(m_sc.shape, -jnp.inf, F32)
    acc_sc[...] = jnp.zeros(acc_sc.shape, F32)
    heads = [slice(hd * LANES, (hd + 1) * LANES) for hd in range(2)]

    def keys(j):
        return pl.ds(pl.multiple_of(j * tk, tk), tk)

    def scores(j, dst):
        for hd, sl in enumerate(heads):
            dst[hd] = jnp.dot(k_ref[keys(j), sl], qt_ref[sl, :], preferred_element_type=F32)

    def accumulate(j, src):
        for hd, sl in enumerate(heads):
            st = src[hd]
            m_old = m_sc[hd]
            m_new = jnp.maximum(m_old, jnp.max(st, axis=0, keepdims=True))
            pt = jnp.exp2(st - m_new).astype(BF16)
            acc_sc[hd] = acc_sc[hd] * jnp.exp2(m_old - m_new) + jnp.dot(
                vt_ref[sl, keys(j)], pt, preferred_element_type=F32)
            m_sc[hd] = m_new

    scores(0, s0_sc)

    def body(jj, carry):
        j = 2 * jj
        scores(j + 1, s1_sc)
        accumulate(j, s0_sc)
        scores(j + 2, s0_sc)
        accumulate(j + 1, s1_sc)
        return carry

    lax.fori_loop(0, n_chunks // 2 - 1, body, 0)
    scores(n_chunks - 1, s1_sc)
    accumulate(n_chunks - 2, s0_sc)
    accumulate(n_chunks - 1, s1_sc)
    outs = []
    for hd in range(2):
        acc = acc_sc[hd]
        outs.append(acc[:B_V] * (1.0 / acc[B_V:B_V + 1]))
    o_ref[...] = jnp.concatenate(outs, axis=0).T.astype(BF16)


def _mla_attn(qt, k, vt):
    batch, s_len, hw = k.shape
    tq = TQ_MLA
    out = pl.pallas_call(
        functools.partial(_mla_attn_kernel, tk=TK_MLA),
        grid=(batch, B_HEADS // 2, s_len // tq),
        in_specs=[pl.BlockSpec((None, 2 * LANES, tq), lambda b, p, i: (b, p, i)),
                  pl.BlockSpec((None, s_len, 2 * LANES), lambda b, p, i: (b, 0, p)),
                  pl.BlockSpec((None, 2 * LANES, s_len), lambda b, p, i: (b, p, 0))],
        out_specs=pl.BlockSpec((None, tq, LANES), lambda b, p, i: (b, i, p)),
        out_shape=jax.ShapeDtypeStruct((batch, s_len, B_HEADS * B_V), BF16),
        scratch_shapes=[pltpu.VMEM((2, 1, tq), F32), pltpu.VMEM((2, LANES, tq), F32),
                        pltpu.VMEM((2, TK_MLA, tq), F32), pltpu.VMEM((2, TK_MLA, tq), F32)],
        name="mla_attn",
    )(qt, k, vt)
    return out.reshape(batch * s_len, B_HEADS * B_V)


def _win_attn_kernel(sink_ref, qt_ref, kp_ref, kc_ref, kn_ref, vp_ref, vc_ref, vn_ref, o_ref, *, s_len):
    n = pl.program_id(1)
    grp = A_HEADS // A_KV_HEADS
    dh = A_HEAD_DIM
    tq = kc_ref.shape[0]
    kj = lax.broadcasted_iota(jnp.int32, (tq + 2 * BLK, tq), 0)
    qi = lax.broadcasted_iota(jnp.int32, (tq + 2 * BLK, tq), 1)
    k_abs = n * tq - BLK + kj
    ok = (jnp.abs(kj - BLK - qi) <= A_WINDOW) & (k_abs >= 0) & (k_abs < s_len)
    ok = jnp.concatenate([ok] * grp, axis=1)
    lane_head = lax.broadcasted_iota(jnp.int32, (1, grp * tq), 1) // tq
    kband = jnp.concatenate([kp_ref[:, :LANES], kc_ref[:, :LANES], kn_ref[:, :LANES]], axis=0)
    zeros = jnp.zeros((dh, tq), BF16)
    heads = []
    for g in range(A_KV_HEADS):
        tiles = []
        for a in range(grp):
            qh = qt_ref[(g * grp + a) * dh:(g * grp + a + 1) * dh, :]
            tiles.append(jnp.concatenate([zeros] * g + [qh] + [zeros] * (A_KV_HEADS - 1 - g), axis=0))
        qg = jnp.concatenate(tiles, axis=1)
        st = jnp.where(ok, jnp.dot(kband, qg, preferred_element_type=F32), NEG_INF)
        sk = jnp.zeros((1, grp * tq), F32)
        for a in range(grp):
            sk = jnp.where(lane_head == a, sink_ref[g * grp + a] * LOG2E, sk)
        m = jnp.maximum(jnp.max(st, axis=0, keepdims=True), sk)
        pt = jnp.exp2(st - m).astype(BF16)
        rows = slice(g * LANES, (g + 1) * LANES)
        vband = jnp.concatenate([vp_ref[rows, :], vc_ref[rows, :], vn_ref[rows, :]], axis=1)
        ot = jnp.dot(vband, pt, preferred_element_type=F32)
        o = ot[:dh] * (1.0 / (ot[dh:dh + 1] + jnp.exp2(sk - m)))
        heads.extend(o[:, a * tq:(a + 1) * tq] for a in range(grp))
    for pair in range(A_HEADS // 2):
        o_ref[:, pair * LANES:(pair + 1) * LANES] = jnp.concatenate(
            [heads[2 * pair], heads[2 * pair + 1]], axis=0).T.astype(BF16)


def _win_attn(p, qt, vt, sink, batch):
    t = p.shape[0]
    s_len = t // batch
    tq = TQ_WIN
    nq, nb, r = s_len // tq, s_len // BLK, tq // BLK
    p3 = p.reshape(batch, s_len, P_WIDTH)
    kcol = P_KA // (2 * LANES)
    lo = lambda n: jnp.maximum(n * r - 1, 0)
    hi = lambda n: jnp.minimum((n + 1) * r, nb - 1)
    kedge = lambda f: pl.BlockSpec((None, BLK, 2 * LANES), lambda b, n, s: (b, f(n), kcol))
    vedge = lambda f: pl.BlockSpec((A_KV_HEADS * LANES, BLK), lambda b, n, s: (0, b * nb + f(n)))
    out = pl.pallas_call(
        functools.partial(_win_attn_kernel, s_len=s_len),
        grid_spec=pltpu.PrefetchScalarGridSpec(
            num_scalar_prefetch=1,
            grid=(batch, nq),
            in_specs=[pl.BlockSpec((A_HEADS * A_HEAD_DIM, tq), lambda b, n, s: (0, b * nq + n)),
                      kedge(lo), pl.BlockSpec((None, tq, 2 * LANES), lambda b, n, s: (b, n, kcol)), kedge(hi),
                      vedge(lo), pl.BlockSpec((A_KV_HEADS * LANES, tq), lambda b, n, s: (0, b * nq + n)), vedge(hi)],
            out_specs=pl.BlockSpec((None, tq, A_HEADS * A_HEAD_DIM), lambda b, n, s: (b, n, 0)),
        ),
        out_shape=jax.ShapeDtypeStruct((batch, s_len, A_HEADS * A_HEAD_DIM), BF16),
        name="win_attn",
    )(sink, qt, p3, p3, p3, vt, vt, vt)
    return out.reshape(t, A_HEADS * A_HEAD_DIM)


def _mem_kv_kernel(mem_ref, g_ref, w_ref, o_ref):
    mem_n = _rms(mem_ref[...], g_ref[...]).astype(BF16)
    o_ref[...] = jnp.dot(mem_n, w_ref[...], preferred_element_type=F32).astype(BF16)


def _mem_kv(mem2, g_mem, w_mem_kv):
    depth, d, width = w_mem_kv.shape
    rows = mem2.shape[0]
    return pl.pallas_call(
        _mem_kv_kernel,
        grid=(depth,),
        in_specs=[pl.BlockSpec((rows, d), lambda l: (0, 0)),
                  pl.BlockSpec((1, d), lambda l: (0, 0)),
                  pl.BlockSpec((None, d, width), lambda l: (l, 0, 0))],
        out_specs=pl.BlockSpec((None, rows, width), lambda l: (l, 0, 0)),
        out_shape=jax.ShapeDtypeStruct((depth, rows, width), BF16),
        name="mem_kv",
    )(mem2, g_mem, w_mem_kv)


def _mem_attn_kernel(q_ref, kv_ref, o_ref):
    dh = M_HEAD_DIM
    for hd in range(M_HEADS):
        q = q_ref[:, hd * dh:(hd + 1) * dh]
        k = kv_ref[:, hd * dh:(hd + 1) * dh]
        v = kv_ref[:, (M_HEADS + hd) * dh:(M_HEADS + hd + 1) * dh]
        sc = lax.dot_general(q, k, (((1,), (1,)), ((), ())), preferred_element_type=F32)
        e = jnp.exp(sc - jnp.max(sc, axis=-1, keepdims=True))
        pr = (e * (1.0 / jnp.sum(e, axis=-1, keepdims=True))).astype(BF16)
        o_ref[:, hd * dh:(hd + 1) * dh] = jnp.dot(pr, v, preferred_element_type=F32).astype(BF16)


def _mem_attn(p, mem_kv_l, batch):
    t = p.shape[0]
    s_len = t // batch
    tm = TM_MEM
    n_mem = mem_kv_l.shape[0] // batch
    width = M_HEADS * M_HEAD_DIM
    p3 = p.reshape(batch, s_len, P_WIDTH)
    kv3 = mem_kv_l.reshape(batch, n_mem, 2 * width)
    out = pl.pallas_call(
        _mem_attn_kernel,
        grid=(batch, s_len // tm),
        in_specs=[pl.BlockSpec((None, tm, width), lambda b, i: (b, i, P_QM // width)),
                  pl.BlockSpec((None, n_mem, 2 * width), lambda b, i: (b, 0, 0))],
        out_specs=pl.BlockSpec((None, tm, width), lambda b, i: (b, i, 0)),
        out_shape=jax.ShapeDtypeStruct((batch, s_len, width), BF16),
        name="mem_attn",
    )(p3, kv3)
    return out.reshape(t, width)


def _mix_kernel(x_ref, ya_ref, yb_ref, ym_ref, g_ref, wb_ref, wo_ref, o_ref):
    d = x_ref.shape[1]
    mixed = None
    for i, y_ref in enumerate((ya_ref, yb_ref, ym_ref)):
        gate = jax.nn.sigmoid(g_ref[:, i * d:(i + 1) * d].astype(F32))
        term = gate * jnp.dot(y_ref[...], wb_ref[i], preferred_element_type=F32)
        mixed = term if mixed is None else mixed + term
    o_ref[...] = x_ref[...] + jnp.dot(mixed.astype(BF16), wo_ref[...], preferred_element_type=F32)


def _mix(x, ya, yb, ym, p, wb, wo):
    t, d = x.shape
    tm = TM_MIX
    bw = ya.shape[1]
    row = lambda i: (i, 0)
    return pl.pallas_call(
        _mix_kernel,
        grid=(t // tm,),
        in_specs=[pl.BlockSpec((tm, d), row),
                  pl.BlockSpec((tm, bw), row), pl.BlockSpec((tm, bw), row), pl.BlockSpec((tm, bw), row),
                  pl.BlockSpec((tm, 3 * d), lambda i: (i, P_GATES)),
                  pl.BlockSpec((3, bw, d), lambda i: (0, 0, 0)),
                  pl.BlockSpec((d, d), lambda i: (0, 0))],
        out_specs=pl.BlockSpec((tm, d), row),
        out_shape=jax.ShapeDtypeStruct((t, d), F32),
        name="mix_out",
    )(x, ya, yb, ym, p, wb, wo)


def _swiglu_step(h, wg, wu, wd):
    gate = jnp.dot(h, wg, preferred_element_type=F32)
    up = jnp.dot(h, wu, preferred_element_type=F32)
    act = (gate * jax.nn.sigmoid(gate) * up).astype(BF16)
    return jnp.dot(act, wd, preferred_element_type=F32)


def _ffn_kernel(x_ref, g_ref, wg_ref, wu_ref, wd_ref, o_ref, h_sc, acc_sc):
    j = pl.program_id(1)

    @pl.when(j == 0)
    def _():
        h_sc[...] = _rms(x_ref[...], g_ref[...]).astype(BF16)
        acc_sc[...] = jnp.zeros(acc_sc.shape, F32)

    acc_sc[...] += _swiglu_step(h_sc[...], wg_ref[...], wu_ref[...], wd_ref[...])

    @pl.when(j == pl.num_programs(1) - 1)
    def _():
        o_ref[...] = x_ref[...] + acc_sc[...]


def _ffn(x, g, wg, wu, wd):
    t, d = x.shape
    f = wg.shape[1]
    tm, tf = TM_FFN, TF_FFN
    return pl.pallas_call(
        _ffn_kernel,
        grid=(t // tm, f // tf),
        in_specs=[pl.BlockSpec((tm, d), lambda i, j: (i, 0)),
                  pl.BlockSpec((1, d), lambda i, j: (0, 0)),
                  pl.BlockSpec((d, tf), lambda i, j: (0, j)),
                  pl.BlockSpec((d, tf), lambda i, j: (0, j)),
                  pl.BlockSpec((tf, d), lambda i, j: (j, 0))],
        out_specs=pl.BlockSpec((tm, d), lambda i, j: (i, 0)),
        out_shape=jax.ShapeDtypeStruct((t, d), F32),
        scratch_shapes=[pltpu.VMEM((tm, d), BF16), pltpu.VMEM((tm, d), F32)],
        name="ffn_dense",
    )(x, g, wg, wu, wd)


def _router_kernel(x_ref, g_ref, wr_ref, idx_ref, wgt_ref, xt_ref):
    x = x_ref[...]
    xt_ref[...] = x.reshape(xt_ref.shape)
    h = _rms(x, g_ref[...])
    logits = lax.dot_general(wr_ref[...], h, (((1,), (1,)), ((), ())),
                             precision=lax.Precision.HIGHEST, preferred_element_type=F32)
    e_id = lax.broadcasted_iota(jnp.int32, logits.shape, 0)
    m1 = jnp.max(logits, axis=0, keepdims=True)
    i1 = jnp.min(jnp.where(logits == m1, e_id, N_EXPERTS), axis=0, keepdims=True)
    rest = jnp.where(e_id == i1, -jnp.inf, logits)
    m2 = jnp.max(rest, axis=0, keepdims=True)
    i2 = jnp.min(jnp.where(rest == m2, e_id, N_EXPERTS), axis=0, keepdims=True)
    e2 = jnp.exp(m2 - m1)
    w1 = 1.0 / (1.0 + e2)
    idx_ref[...] = jnp.concatenate([i1, i2], axis=0)
    wgt_ref[...] = jnp.concatenate([w1, e2 * w1], axis=0)


def _router(x, g, wr_t):
    t, d = x.shape
    tm = TM_ROUTE
    return pl.pallas_call(
        _router_kernel,
        grid=(t // tm,),
        in_specs=[pl.BlockSpec((tm, d), lambda i: (i, 0)),
                  pl.BlockSpec((1, d), lambda i: (0, 0)),
                  pl.BlockSpec((N_EXPERTS, d), lambda i: (0, 0))],
        out_specs=[pl.BlockSpec((TOP_K, tm), lambda i: (0, i)),
                   pl.BlockSpec((TOP_K, tm), lambda i: (0, i)),
                   pl.BlockSpec((tm * SUBLANES, LANES), lambda i: (i, 0))],
        out_shape=[jax.ShapeDtypeStruct((TOP_K, t), jnp.int32),
                   jax.ShapeDtypeStruct((TOP_K, t), F32),
                   jax.ShapeDtypeStruct((t * SUBLANES, LANES), F32)],
        name="moe_router",
    )(x, g, wr_t)


def _moe_rows_per_step(n_steps):
    return -(-TM_EXP // (n_steps * SUBLANES)) * SUBLANES


def _slot_tables(top_i, n_tok, buf_rows):
    n_asg = n_tok * TOP_K
    flat_e = top_i.T.reshape(n_asg)
    order = jnp.argsort(flat_e, stable=True).astype(jnp.int32)
    counts = jnp.sum(flat_e[:, None] == jnp.arange(N_EXPERTS, dtype=jnp.int32)[None, :], axis=0, dtype=jnp.int32)
    padded = (counts + TM_EXP - 1) // TM_EXP * TM_EXP
    pad_end = jnp.cumsum(padded)
    start = jnp.cumsum(counts) - counts
    pads_before = jnp.cumsum(padded - counts) - (padded - counts)
    n_slots = n_asg + N_EXPERTS * TM_EXP
    n_blk = n_slots // TM_EXP
    extra = buf_rows - TM_EXP
    blk_e = jnp.minimum(jnp.searchsorted(pad_end, jnp.arange(n_blk, dtype=jnp.int32) * TM_EXP, side='right'),
                        N_EXPERTS - 1).astype(jnp.int32)
    slot_e = jnp.repeat(blk_e, TM_EXP)
    rank = jnp.arange(n_slots, dtype=jnp.int32) - (pad_end - padded)[slot_e]
    is_pad = rank >= counts[slot_e]
    asg = order[jnp.clip(start[slot_e] + rank, 0, n_asg - 1)]
    slot_tok = jnp.where(is_pad, 0, asg // TOP_K)
    slot_row = jnp.where(is_pad, n_asg + pads_before[slot_e] + rank - counts[slot_e],
                         (asg % TOP_K) * n_tok + slot_tok)
    gather_tok = jnp.concatenate([slot_tok.reshape(n_blk, TM_EXP), jnp.zeros((n_blk, extra), jnp.int32)], axis=1)
    spare = n_slots + jnp.arange((n_blk + 1) * buf_rows - n_slots, dtype=jnp.int32)
    lead, over = spare[:TM_EXP], spare[TM_EXP:].reshape(n_blk + 1, extra)
    scatter_row = jnp.concatenate(
        [jnp.concatenate([lead[None], slot_row.reshape(n_blk, TM_EXP)], axis=0), over], axis=1)
    return blk_e, gather_tok.reshape(-1).astype(jnp.int32), scatter_row.reshape(-1).astype(jnp.int32)


def _moe_ffn_kernel(blk_e_ref, gtok_ref, srow_ref, x_hbm, g_ref, wg_ref, wu_ref, wd_ref, o_hbm,
                    xg_sc, ys_sc, h_sc, acc_sc, gsem, ssem, *, rows_per_step):
    i, j = pl.program_id(0), pl.program_id(1)
    n_blk, n_steps = pl.num_programs(0), pl.num_programs(1)
    rows, d = h_sc.shape
    tr = d // LANES
    buf_rows = xg_sc.shape[1] // tr
    slot = i % 2
    other = 1 - slot

    def tile(ref, k):
        return ref.at[pl.ds(pl.multiple_of(k * tr, tr), tr), :]

    def gathered(s):
        return pltpu.make_async_copy(x_hbm.at[pl.ds(0, buf_rows * tr), :], xg_sc.at[s], gsem.at[s])

    def scattered(s):
        return pltpu.make_async_copy(ys_sc.at[s], o_hbm.at[pl.ds(0, buf_rows * tr), :], ssem.at[s])

    def get_row(s, k, entry):
        return pltpu.make_async_copy(tile(x_hbm, gtok_ref[entry]), tile(xg_sc.at[s], k), gsem.at[s])

    def put_row(s, k, entry):
        return pltpu.make_async_copy(tile(ys_sc.at[s], k), tile(o_hbm, srow_ref[entry]), ssem.at[s])

    @pl.when((i == 0) & (j == 0))
    def _():
        ys_sc[...] = jnp.zeros(ys_sc.shape, F32)

        def start(k, c):
            get_row(0, k, k).start()
            return c

        lax.fori_loop(0, buf_rows, start, 0)

    @pl.when(j == 0)
    def _():
        gathered(slot).wait()
        for s in range(2):
            @pl.when(slot == s)
            def _():
                xs = [xg_sc[s, pl.ds(c, rows, stride=tr), :] for c in range(tr)]
                ms = sum(jnp.sum(x * x, axis=-1, keepdims=True) for x in xs) * (1.0 / d)
                inv = lax.rsqrt(ms + EPS)
                for c, x in enumerate(xs):
                    cols = slice(c * LANES, (c + 1) * LANES)
                    h_sc[:, cols] = (x * inv * g_ref[:, cols]).astype(BF16)

        acc_sc[...] = jnp.zeros(acc_sc.shape, F32)

    nxt = jnp.minimum(i + 1, n_blk - 1)
    for r in range(rows_per_step):
        k = j * rows_per_step + r
        get_row(other, k, nxt * buf_rows + k).start(priority=r % DMA_QUEUES)
        put_row(other, k, i * buf_rows + k).start(priority=(r + 1) % DMA_QUEUES)

    acc_sc[...] += _swiglu_step(h_sc[...], wg_ref[...], wu_ref[...], wd_ref[...])

    @pl.when(j == n_steps - 1)
    def _():
        @pl.when(i > 0)
        def _():
            scattered(slot).wait()

        for s in range(2):
            @pl.when(slot == s)
            def _():
                for c in range(tr):
                    ys_sc[s, pl.ds(c, rows, stride=tr), :] = acc_sc[:, c * LANES:(c + 1) * LANES]

    @pl.when((i == n_blk - 1) & (j == n_steps - 1))
    def _():
        def start(k, c):
            put_row(slot, k, n_blk * buf_rows + k).start()
            return c

        lax.fori_loop(0, buf_rows, start, 0)
        scattered(other).wait()
        scattered(slot).wait()
        gathered(other).wait()


def _moe_ffn(x_tiles, g, wg, wu, wd, blk_e, gather_tok, scatter_row):
    d, f = wg.shape[1], wg.shape[2]
    tr = d // LANES
    assert tr % SUBLANES == 0
    tm, tf = TM_EXP, TF_FFN
    n_steps = f // tf
    rows_per_step = _moe_rows_per_step(n_steps)
    buf_rows = rows_per_step * n_steps
    n_blk = gather_tok.shape[0] // buf_rows
    return pl.pallas_call(
        functools.partial(_moe_ffn_kernel, rows_per_step=rows_per_step),
        grid_spec=pltpu.PrefetchScalarGridSpec(
            num_scalar_prefetch=3,
            grid=(n_blk, n_steps),
            in_specs=[pl.BlockSpec(memory_space=pl.ANY),
                      pl.BlockSpec((1, d), lambda i, j, be, gt, sr: (0, 0)),
                      pl.BlockSpec((None, d, tf), lambda i, j, be, gt, sr: (be[i], 0, j)),
                      pl.BlockSpec((None, d, tf), lambda i, j, be, gt, sr: (be[i], 0, j)),
                      pl.BlockSpec((None, tf, d), lambda i, j, be, gt, sr: (be[i], j, 0))],
            out_specs=pl.BlockSpec(memory_space=pl.ANY),
            scratch_shapes=[pltpu.VMEM((2, buf_rows * tr, LANES), F32), pltpu.VMEM((2, buf_rows * tr, LANES), F32),
                            pltpu.VMEM((tm, d), BF16), pltpu.VMEM((tm, d), F32),
                            pltpu.SemaphoreType.DMA((2,)), pltpu.SemaphoreType.DMA((2,))],
        ),
        out_shape=jax.ShapeDtypeStruct((scatter_row.shape[0] * tr, LANES), F32),
        name="moe_ffn",
    )(blk_e, gather_tok, scatter_row, x_tiles, g, wg, wu, wd)


def _moe_combine_kernel(x_ref, w_ref, y0_ref, y1_ref, gf_ref, o_ref, *, final_norm):
    tm, d = x_ref.shape
    y0, y1 = y0_ref[...].reshape(tm, d), y1_ref[...].reshape(tm, d)
    out = x_ref[...] + w_ref[:, 0:1] * y0 + w_ref[:, 1:2] * y1
    if final_norm:
        out = _rms(out, gf_ref[...])
    o_ref[...] = out


def _moe_combine(x, top_w, y, g_final, final_norm):
    t, d = x.shape
    tr = d // LANES
    tm = TM_COMB
    nt = t // tm
    return pl.pallas_call(
        functools.partial(_moe_combine_kernel, final_norm=final_norm),
        grid=(nt,),
        in_specs=[pl.BlockSpec((tm, d), lambda i: (i, 0)),
                  pl.BlockSpec((tm, TOP_K), lambda i: (i, 0)),
                  pl.BlockSpec((tm * tr, LANES), lambda i: (i, 0)),
                  pl.BlockSpec((tm * tr, LANES), lambda i: (nt + i, 0)),
                  pl.BlockSpec((1, d), lambda i: (0, 0))],
        out_specs=pl.BlockSpec((tm, d), lambda i: (i, 0)),
        out_shape=jax.ShapeDtypeStruct((t, d), F32),
        name="moe_combine",
    )(x, top_w, y, y, g_final)


def _final_norm_kernel(x_ref, g_ref, o_ref):
    o_ref[...] = _rms(x_ref[...], g_ref[...])


def _final_norm(x, g):
    t, d = x.shape
    tm = 1024
    return pl.pallas_call(
        _final_norm_kernel,
        grid=(t // tm,),
        in_specs=[pl.BlockSpec((tm, d), lambda i: (i, 0)), pl.BlockSpec((1, d), lambda i: (0, 0))],
        out_specs=pl.BlockSpec((tm, d), lambda i: (i, 0)),
        out_shape=jax.ShapeDtypeStruct((t, d), F32),
        name="final_norm",
    )(x, g)


def _relayout_w_in(w_in):
    qa, ka, va, cq, ckv, kr, qm, gates = jnp.split(
        w_in, [512, 640, 768, 1024, 1152, 1184, 1696], axis=-1)
    pad = jnp.zeros(w_in.shape[:2] + (LANES - B_ROPE,), w_in.dtype)
    w_tok = jnp.concatenate([gates, qm, ka, kr, pad, cq, ckv], axis=-1).astype(BF16)
    return w_tok, jnp.swapaxes(qa, 1, 2).astype(BF16), jnp.swapaxes(va, 1, 2).astype(BF16)


def _relayout_latent(w_uq, w_ukv):
    depth = w_uq.shape[0]
    q = w_uq.reshape(depth, B_Q_RANK, B_HEADS, B_NOPE + B_ROPE)
    q = jnp.pad(q, ((0, 0), (0, 0), (0, 0), (0, LANES - B_NOPE - B_ROPE)))
    kv = w_ukv.reshape(depth, B_KV_RANK, B_HEADS, B_NOPE + B_V)
    k = jnp.pad(kv[..., :B_NOPE], ((0, 0), (0, 0), (0, 0), (0, LANES - B_NOPE)))
    v = jnp.pad(kv[..., B_NOPE:], ((0, 0), (0, 0), (0, 0), (0, LANES - B_V)))
    flat = lambda a, r: a.reshape(depth, r, B_HEADS * LANES).astype(BF16)
    return (jnp.swapaxes(flat(q, B_Q_RANK), 1, 2), flat(k, B_KV_RANK),
            jnp.swapaxes(flat(v, B_KV_RANK), 1, 2))


def kernel(x, mem, positions, g_mix, w_in, g_q_lat, w_uq, g_kv_lat, w_ukv, sink, w_mem_kv, w_branch, w_out,
           g_ffn, w_gate_dense, w_up_dense, w_down_dense, w_router, w_gate_exp, w_up_exp, w_down_exp,
           g_mem, g_final):
    batch, s_len, d = x.shape
    depth = w_in.shape[0]
    t = batch * s_len
    row = lambda a: a.reshape(1, -1).astype(F32)

    w_in_p, wqa_t, wva_t = _relayout_w_in(w_in)
    wq_p, wk_p, wv_p = _relayout_latent(w_uq, w_ukv)
    w_branch_b, w_out_b = w_branch.astype(BF16), w_out.astype(BF16)
    wgd, wud, wdd = w_gate_dense.astype(BF16), w_up_dense.astype(BF16), w_down_dense.astype(BF16)
    wge, wue, wde = w_gate_exp.astype(BF16), w_up_exp.astype(BF16), w_down_exp.astype(BF16)
    wr_t = jnp.swapaxes(w_router, 1, 2).astype(F32)

    cos_a, sin_a, cos_b, sin_b, cos_t, sin_t = _rope_tables(positions)
    mem_kv = _mem_kv(mem.reshape(batch * mem.shape[1], d).astype(F32), row(g_mem), w_mem_kv.astype(BF16))

    xt = x.reshape(t, d).astype(F32)
    for l in range(depth):
        p, qa_t, va_t = _in_proj(xt, row(g_mix[l]), w_in_p[l], wqa_t[l], wva_t[l], cos_a, sin_a, cos_t, sin_t)
        y_a = _win_attn(p, qa_t, va_t, sink[l].astype(F32), batch)
        qt, k, vt = _mla_prep(p, row(g_q_lat[l]), row(g_kv_lat[l]), wq_p[l], wk_p[l], wv_p[l],
                              cos_b, sin_b, cos_t, sin_t, batch)
        y_b = _mla_attn(qt, k, vt)
        y_m = _mem_attn(p, mem_kv[l], batch)
        xt = _mix(xt, y_a, y_b, y_m, p, w_branch_b[l], w_out_b[l])
        i = l // 2
        if l % 2 == 0:
            xt = _ffn(xt, row(g_ffn[l]), wgd[i], wud[i], wdd[i])
            if l == depth - 1:
                xt = _final_norm(xt, row(g_final))
        else:
            top_i, top_w, x_tiles = _router(xt, row(g_ffn[l]), wr_t[i])
            buf_rows = _moe_rows_per_step(wge.shape[3] // TF_FFN) * (wge.shape[3] // TF_FFN)
            blk_e, gather_tok, scatter_row = _slot_tables(top_i, t, buf_rows)
            y = _moe_ffn(x_tiles, row(g_ffn[l]), wge[i], wue[i], wde[i], blk_e, gather_tok, scatter_row)
            xt = _moe_combine(xt, top_w.T, y, row(g_final), final_norm=(l == depth - 1))
    return xt.reshape(batch, s_len, d).astype(x.dtype)
```

```python
import functools

import jax
import jax.numpy as jnp
from jax import lax
from jax.experimental import pallas as pl
from jax.experimental.pallas import tpu as pltpu

F32 = jnp.float32
BF16 = jnp.bfloat16

ROPE_THETA = 10000.0
EPS = 1e-6
NEG_INF = -1e30
LOG2E = 1.4426950408889634

A_HEADS, A_KV_HEADS, A_HEAD_DIM, A_WINDOW = 8, 2, 64, 128
B_HEADS, B_Q_RANK, B_KV_RANK, B_NOPE, B_ROPE, B_V = 8, 256, 128, 64, 32, 64
M_HEADS, M_HEAD_DIM = 4, 128
N_EXPERTS, TOP_K = 8, 2
LANES = 128
SUBLANES = 8
BLK = 128

P_GATES = 0
P_QM = 3072
P_KA = 3584
P_CQ = 3840
P_CKV = 4096
P_WIDTH = 4224

TM_PROJ = 512
TM_PREP = 512
TM_MIX = 512
TM_MEM = 512
TM_FFN = 1024
TF_FFN = 512
TQ_WIN = 256
TQ_MLA = 2048
TK_MLA = 512
TM_ROUTE = 1024
TM_EXP = 1024
TM_COMB = 512
DMA_QUEUES = 2


def _rms(x, g):
    return x * lax.rsqrt(jnp.mean(x * x, axis=-1, keepdims=True) + EPS) * g


def _rot_tile(x, cos, sin_signed, first, shift):
    up = pltpu.roll(x, LANES - shift, 1)
    dn = pltpu.roll(x, shift, 1)
    return x * cos + jnp.where(first, up, dn) * sin_signed


def _rope_table_kernel(posr_ref, par_ref, invc_ref, ca_ref, sa_ref, cb_ref, sb_ref, cbt_ref, sbt_ref):
    ang_t = invc_ref[...] * posr_ref[...]
    ct, st = jnp.cos(ang_t), jnp.sin(ang_t)
    cbt_ref[...] = ct
    sbt_ref[...] = st
    half_a, half_b = A_HEAD_DIM // 2, B_ROPE // 2
    tm = ct.shape[1]
    ca, sa = ct[:half_a].T, st[:half_a].T
    ca_ref[...] = jnp.concatenate([ca] * (LANES // half_a), axis=1)
    sa_ref[...] = jnp.concatenate([sa] * (LANES // half_a), axis=1) * par_ref[1:2, :]
    cb, sb = ct[half_a:].T, st[half_a:].T
    one = jnp.ones((tm, B_NOPE), F32)
    pad = jnp.ones((tm, LANES - B_NOPE - B_ROPE), F32)
    cb_ref[...] = jnp.concatenate([one, cb, cb, pad], axis=1)
    sb_ref[...] = jnp.concatenate([one, sb, sb, pad], axis=1) * par_ref[3:4, :]


def _rope_tables(positions):
    t = positions.size
    tm = 2048
    lane = jnp.arange(LANES)
    half_a = A_HEAD_DIM // 2
    inv_a = jnp.power(F32(ROPE_THETA), -jnp.arange(half_a, dtype=F32) * (2.0 / A_HEAD_DIM))
    sgn_a = jnp.where((lane % A_HEAD_DIM) < half_a, -1.0, 1.0).astype(F32)
    half_b = B_ROPE // 2
    inv_b = jnp.power(F32(ROPE_THETA), -jnp.arange(half_b, dtype=F32) * (2.0 / B_ROPE))
    in_rope = (lane >= B_NOPE) & (lane < B_NOPE + B_ROPE)
    sgn_b = jnp.where(in_rope, jnp.where(lane < B_NOPE + half_b, -1.0, 1.0), 0.0).astype(F32)
    par = jnp.zeros((8, LANES), F32).at[1].set(sgn_a).at[3].set(sgn_b)
    tab = jax.ShapeDtypeStruct((t, LANES), F32)
    n_freq = half_a + half_b
    inv_col = jnp.concatenate([inv_a, inv_b]).reshape(n_freq, 1)
    tab_t = jax.ShapeDtypeStruct((n_freq, t), F32)
    return pl.pallas_call(
        _rope_table_kernel,
        grid=(t // tm,),
        in_specs=[pl.BlockSpec((1, tm), lambda i: (0, i)),
                  pl.BlockSpec((8, LANES), lambda i: (0, 0)),
                  pl.BlockSpec((n_freq, 1), lambda i: (0, 0))],
        out_specs=[pl.BlockSpec((tm, LANES), lambda i: (i, 0))] * 4
        + [pl.BlockSpec((n_freq, tm), lambda i: (0, i))] * 2,
        out_shape=[tab] * 4 + [tab_t] * 2,
        name="rope_tables",
    )(positions.reshape(1, t).astype(F32), par, inv_col)


def _in_proj_kernel(x_ref, g_ref, w_ref, wqt_ref, wvt_ref, cos_ref, sin_ref, cost_ref, sint_ref,
                    o_ref, qt_ref, vt_ref):
    h = _rms(x_ref[...], g_ref[...]).astype(BF16)
    half = A_HEAD_DIM // 2

    def proj(c0, c1):
        return jnp.dot(h, w_ref[:, c0:c1], preferred_element_type=F32)

    for c0 in range(P_GATES, P_QM, 512):
        o_ref[:, c0:c0 + 512] = proj(c0, c0 + 512).astype(BF16)
    o_ref[:, P_QM:P_KA] = (proj(P_QM, P_KA) * (M_HEAD_DIM ** -0.5)).astype(BF16)
    cos, sin = cos_ref[...], sin_ref[...]
    lane = lax.broadcasted_iota(jnp.int32, cos.shape, 1)
    ka = proj(P_KA, P_CQ)
    o_ref[:, P_KA:P_KA + LANES] = _rot_tile(ka[:, :LANES], cos, sin, (lane % A_HEAD_DIM) < half, half).astype(BF16)
    o_ref[:, P_KA + LANES:P_CQ] = ka[:, LANES:].astype(BF16)
    o_ref[:, P_CQ:P_WIDTH] = proj(P_CQ, P_WIDTH).astype(BF16)

    nt = (((1,), (1,)), ((), ()))
    ct, st = cost_ref[...], sint_ref[...]
    qscale = A_HEAD_DIM ** -0.5 * LOG2E
    qt = lax.dot_general(wqt_ref[...], h, nt, preferred_element_type=F32)
    for hd in range(A_HEADS):
        r0 = hd * A_HEAD_DIM
        x1, x2 = qt[r0:r0 + half], qt[r0 + half:r0 + A_HEAD_DIM]
        qt_ref[r0:r0 + half, :] = ((x1 * ct - x2 * st) * qscale).astype(BF16)
        qt_ref[r0 + half:r0 + A_HEAD_DIM, :] = ((x2 * ct + x1 * st) * qscale).astype(BF16)
    vt = lax.dot_general(wvt_ref[...], h, nt, preferred_element_type=F32)
    row = lax.broadcasted_iota(jnp.int32, (LANES - A_HEAD_DIM, vt.shape[1]), 0)
    ones_pad = (row == 0).astype(BF16)
    for g in range(A_KV_HEADS):
        vt_ref[g * LANES:g * LANES + A_HEAD_DIM, :] = vt[g * A_HEAD_DIM:(g + 1) * A_HEAD_DIM].astype(BF16)
        vt_ref[g * LANES + A_HEAD_DIM:(g + 1) * LANES, :] = ones_pad


def _in_proj(x, g, w, wqt, wvt, cos_a, sin_a, cos_t, sin_t):
    t, d = x.shape
    tm = TM_PROJ
    half = A_HEAD_DIM // 2
    qrows, vrows = A_HEADS * A_HEAD_DIM, A_KV_HEADS * LANES
    const = lambda i: (0, 0)
    return pl.pallas_call(
        _in_proj_kernel,
        grid=(t // tm,),
        in_specs=[pl.BlockSpec((tm, d), lambda i: (i, 0)),
                  pl.BlockSpec((1, d), const),
                  pl.BlockSpec((d, P_WIDTH), const),
                  pl.BlockSpec((qrows, d), const),
                  pl.BlockSpec((A_KV_HEADS * A_HEAD_DIM, d), const),
                  pl.BlockSpec((tm, LANES), lambda i: (i, 0)),
                  pl.BlockSpec((tm, LANES), lambda i: (i, 0)),
                  pl.BlockSpec((half, tm), lambda i: (0, i)),
                  pl.BlockSpec((half, tm), lambda i: (0, i))],
        out_specs=[pl.BlockSpec((tm, P_WIDTH), lambda i: (i, 0)),
                   pl.BlockSpec((qrows, tm), lambda i: (0, i)),
                   pl.BlockSpec((vrows, tm), lambda i: (0, i))],
        out_shape=[jax.ShapeDtypeStruct((t, P_WIDTH), BF16),
                   jax.ShapeDtypeStruct((qrows, t), BF16),
                   jax.ShapeDtypeStruct((vrows, t), BF16)],
        name="in_proj",
    )(x, g, w, wqt, wvt, cos_a, sin_a, cos_t, sin_t)


def _mla_prep_kernel(cq_ref, ckv_ref, kr_ref, gq_ref, gkv_ref, wqt_ref, wk_ref, wvt_ref,
                     cos_ref, sin_ref, cost_ref, sint_ref, qt_ref, k_ref, vt_ref):
    half = B_ROPE // 2
    nt = (((1,), (1,)), ((), ()))
    cq = _rms(cq_ref[...].astype(F32), gq_ref[...]).astype(BF16)
    ckv = _rms(ckv_ref[...].astype(F32), gkv_ref[...]).astype(BF16)

    cos, sin = cos_ref[...], sin_ref[...]
    lane = lax.broadcasted_iota(jnp.int32, cos.shape, 1)
    first = (lane >= B_NOPE) & (lane < B_NOPE + half)
    kr = _rot_tile(pltpu.roll(kr_ref[:, LANES:].astype(F32), B_NOPE, 1), cos, sin, first, half)
    k = jnp.dot(ckv, wk_ref[...], preferred_element_type=F32)
    for hd in range(B_HEADS):
        sl = slice(hd * LANES, (hd + 1) * LANES)
        k_ref[:, sl] = (k[:, sl] + kr).astype(BF16)

    ct, st = cost_ref[...], sint_ref[...]
    qscale = (B_NOPE + B_ROPE) ** -0.5 * LOG2E
    qt = lax.dot_general(wqt_ref[...], cq, nt, preferred_element_type=F32)
    vt = lax.dot_general(wvt_ref[...], ckv, nt, preferred_element_type=F32)
    row = lax.broadcasted_iota(jnp.int32, (LANES, vt.shape[1]), 0)
    ones_row = (row == B_V).astype(F32)
    for hd in range(B_HEADS):
        r0 = hd * LANES
        x1 = qt[r0 + B_NOPE:r0 + B_NOPE + half]
        x2 = qt[r0 + B_NOPE + half:r0 + B_NOPE + B_ROPE]
        qh = jnp.concatenate([qt[r0:r0 + B_NOPE], x1 * ct - x2 * st, x2 * ct + x1 * st,
                              qt[r0 + B_NOPE + B_ROPE:r0 + LANES]], axis=0)
        qt_ref[r0:r0 + LANES, :] = (qh * qscale).astype(BF16)
        vt_ref[r0:r0 + LANES, :] = (vt[r0:r0 + LANES] + ones_row).astype(BF16)


def _mla_prep(p, gq, gkv, wqt, wk, wvt, cos_b, sin_b, cos_bt, sin_bt, batch):
    t = p.shape[0]
    s_len = t // batch
    tm = TM_PREP
    nt = s_len // tm
    hw = B_HEADS * LANES
    half = B_ROPE // 2
    const = lambda b, i: (0, 0)
    tok = lambda b, i: (b * nt + i, 0)
    return pl.pallas_call(
        _mla_prep_kernel,
        grid=(batch, nt),
        in_specs=[pl.BlockSpec((tm, B_Q_RANK), lambda b, i: (b * nt + i, P_CQ // B_Q_RANK)),
                  pl.BlockSpec((tm, B_KV_RANK), lambda b, i: (b * nt + i, P_CKV // B_KV_RANK)),
                  pl.BlockSpec((tm, 2 * LANES), lambda b, i: (b * nt + i, P_KA // (2 * LANES))),
                  pl.BlockSpec((1, B_Q_RANK), const),
                  pl.BlockSpec((1, B_KV_RANK), const),
                  pl.BlockSpec((hw, B_Q_RANK), const),
                  pl.BlockSpec((B_KV_RANK, hw), const),
                  pl.BlockSpec((hw, B_KV_RANK), const),
                  pl.BlockSpec((tm, LANES), tok),
                  pl.BlockSpec((tm, LANES), tok),
                  pl.BlockSpec((half, tm), lambda b, i: (A_HEAD_DIM // 2 // half, b * nt + i)),
                  pl.BlockSpec((half, tm), lambda b, i: (A_HEAD_DIM // 2 // half, b * nt + i))],
        out_specs=[pl.BlockSpec((None, hw, tm), lambda b, i: (b, 0, i)),
                   pl.BlockSpec((None, tm, hw), lambda b, i: (b, i, 0)),
                   pl.BlockSpec((None, hw, tm), lambda b, i: (b, 0, i))],
        out_shape=[jax.ShapeDtypeStruct((batch, hw, s_len), BF16),
                   jax.ShapeDtypeStruct((batch, s_len, hw), BF16),
                   jax.ShapeDtypeStruct((batch, hw, s_len), BF16)],
        name="mla_prep",
    )(p, p, p, gq, gkv, wqt, wk, wvt, cos_b, sin_b, cos_bt, sin_bt)


def _mla_attn_kernel(qt_ref, k_ref, vt_ref, o_ref, m_sc, acc_sc, s0_sc, s1_sc, *, tk):
    n_chunks = k_ref.shape[0] // tk
    m_sc[...] = jnp.full(m_sc.shape, -jnp.inf, F32)
    acc_sc[...] = jnp.zeros(acc_sc.shape, F32)
    heads = [slice(hd * LANES, (hd + 1) * LANES) for hd in range(2)]

    def keys(j):
        return pl.ds(pl.multiple_of(j * tk, tk), tk)

    def scores(j, dst):
        for hd, sl in enumerate(heads):
            dst[hd] = jnp.dot(k_ref[keys(j), sl], qt_ref[sl, :], preferred_element_type=F32)

    def accumulate(j, src):
        for hd, sl in enumerate(heads):
            st = src[hd]
            m_old = m_sc[hd]
            m_new = jnp.maximum(m_old, jnp.max(st, axis=0, keepdims=True))
            pt = jnp.exp2(st - m_new).astype(BF16)
            acc_sc[hd] = acc_sc[hd] * jnp.exp2(m_old - m_new) + jnp.dot(
                vt_ref[sl, keys(j)], pt, preferred_element_type=F32)
            m_sc[hd] = m_new

    scores(0, s0_sc)

    def body(jj, carry):
        j = 2 * jj
        scores(j + 1, s1_sc)
        accumulate(j, s0_sc)
        scores(j + 2, s0_sc)
        accumulate(j + 1, s1_sc)
        return carry

    lax.fori_loop(0, n_chunks // 2 - 1, body, 0)
    scores(n_chunks - 1, s1_sc)
    accumulate(n_chunks - 2, s0_sc)
    accumulate(n_chunks - 1, s1_sc)
    outs = []
    for hd in range(2):
        acc = acc_sc[hd]
        outs.append(acc[:B_V] * (1.0 / acc[B_V:B_V + 1]))
    o_ref[...] = jnp.concatenate(outs, axis=0).T.astype(BF16)


def _mla_attn(qt, k, vt):
    batch, s_len, hw = k.shape
    tq = TQ_MLA
    out = pl.pallas_call(
        functools.partial(_mla_attn_kernel, tk=TK_MLA),
        grid=(batch, B_HEADS // 2, s_len // tq),
        in_specs=[pl.BlockSpec((None, 2 * LANES, tq), lambda b, p, i: (b, p, i)),
                  pl.BlockSpec((None, s_len, 2 * LANES), lambda b, p, i: (b, 0, p)),
                  pl.BlockSpec((None, 2 * LANES, s_len), lambda b, p, i: (b, p, 0))],
        out_specs=pl.BlockSpec((None, tq, LANES), lambda b, p, i: (b, i, p)),
        out_shape=jax.ShapeDtypeStruct((batch, s_len, B_HEADS * B_V), BF16),
        scratch_shapes=[pltpu.VMEM((2, 1, tq), F32), pltpu.VMEM((2, LANES, tq), F32),
                        pltpu.VMEM((2, TK_MLA, tq), F32), pltpu.VMEM((2, TK_MLA, tq), F32)],
        name="mla_attn",
    )(qt, k, vt)
    return out.reshape(batch * s_len, B_HEADS * B_V)


def _win_attn_kernel(sink_ref, qt_ref, kp_ref, kc_ref, kn_ref, vp_ref, vc_ref, vn_ref, o_ref, *, s_len):
    n = pl.program_id(1)
    grp = A_HEADS // A_KV_HEADS
    dh = A_HEAD_DIM
    tq = kc_ref.shape[0]
    kj = lax.broadcasted_iota(jnp.int32, (tq + 2 * BLK, tq), 0)
    qi = lax.broadcasted_iota(jnp.int32, (tq + 2 * BLK, tq), 1)
    k_abs = n * tq - BLK + kj
    ok = (jnp.abs(kj - BLK - qi) <= A_WINDOW) & (k_abs >= 0) & (k_abs < s_len)
    ok = jnp.concatenate([ok] * grp, axis=1)
    lane_head = lax.broadcasted_iota(jnp.int32, (1, grp * tq), 1) // tq
    kband = jnp.concatenate([kp_ref[:, :LANES], kc_ref[:, :LANES], kn_ref[:, :LANES]], axis=0)
    zeros = jnp.zeros((dh, tq), BF16)
    heads = []
    for g in range(A_KV_HEADS):
        tiles = []
        for a in range(grp):
            qh = qt_ref[(g * grp + a) * dh:(g * grp + a + 1) * dh, :]
            tiles.append(jnp.concatenate([zeros] * g + [qh] + [zeros] * (A_KV_HEADS - 1 - g), axis=0))
        qg = jnp.concatenate(tiles, axis=1)
        st = jnp.where(ok, jnp.dot(kband, qg, preferred_element_type=F32), NEG_INF)
        sk = jnp.zeros((1, grp * tq), F32)
        for a in range(grp):
            sk = jnp.where(lane_head == a, sink_ref[g * grp + a] * LOG2E, sk)
        m = jnp.maximum(jnp.max(st, axis=0, keepdims=True), sk)
        pt = jnp.exp2(st - m).astype(BF16)
        rows = slice(g * LANES, (g + 1) * LANES)
        vband = jnp.concatenate([vp_ref[rows, :], vc_ref[rows, :], vn_ref[rows, :]], axis=1)
        ot = jnp.dot(vband, pt, preferred_element_type=F32)
        o = ot[:dh] * (1.0 / (ot[dh:dh + 1] + jnp.exp2(sk - m)))
        heads.extend(o[:, a * tq:(a + 1) * tq] for a in range(grp))
    for pair in range(A_HEADS // 2):
        o_ref[:, pair * LANES:(pair + 1) * LANES] = jnp.concatenate(
            [heads[2 * pair], heads[2 * pair + 1]], axis=0).T.astype(BF16)


def _win_attn(p, qt, vt, sink, batch):
    t = p.shape[0]
    s_len = t // batch
    tq = TQ_WIN
    nq, nb, r = s_len // tq, s_len // BLK, tq // BLK
    p3 = p.reshape(batch, s_len, P_WIDTH)
    kcol = P_KA // (2 * LANES)
    lo = lambda n: jnp.maximum(n * r - 1, 0)
    hi = lambda n: jnp.minimum((n + 1) * r, nb - 1)
    kedge = lambda f: pl.BlockSpec((None, BLK, 2 * LANES), lambda b, n, s: (b, f(n), kcol))
    vedge = lambda f: pl.BlockSpec((A_KV_HEADS * LANES, BLK), lambda b, n, s: (0, b * nb + f(n)))
    out = pl.pallas_call(
        functools.partial(_win_attn_kernel, s_len=s_len),
        grid_spec=pltpu.PrefetchScalarGridSpec(
            num_scalar_prefetch=1,
            grid=(batch, nq),
            in_specs=[pl.BlockSpec((A_HEADS * A_HEAD_DIM, tq), lambda b, n, s: (0, b * nq + n)),
                      kedge(lo), pl.BlockSpec((None, tq, 2 * LANES), lambda b, n, s: (b, n, kcol)), kedge(hi),
                      vedge(lo), pl.BlockSpec((A_KV_HEADS * LANES, tq), lambda b, n, s: (0, b * nq + n)), vedge(hi)],
            out_specs=pl.BlockSpec((None, tq, A_HEADS * A_HEAD_DIM), lambda b, n, s: (b, n, 0)),
        ),
        out_shape=jax.ShapeDtypeStruct((batch, s_len, A_HEADS * A_HEAD_DIM), BF16),
        name="win_attn",
    )(sink, qt, p3, p3, p3, vt, vt, vt)
    return out.reshape(t, A_HEADS * A_HEAD_DIM)


def _mem_kv_kernel(mem_ref, g_ref, w_ref, o_ref):
    mem_n = _rms(mem_ref[...], g_ref[...]).astype(BF16)
    o_ref[...] = jnp.dot(mem_n, w_ref[...], preferred_element_type=F32).astype(BF16)


def _mem_kv(mem2, g_mem, w_mem_kv):
    depth, d, width = w_mem_kv.shape
    rows = mem2.shape[0]
    return pl.pallas_call(
        _mem_kv_kernel,
        grid=(depth,),
        in_specs=[pl.BlockSpec((rows, d), lambda l: (0, 0)),
                  pl.BlockSpec((1, d), lambda l: (0, 0)),
                  pl.BlockSpec((None, d, width), lambda l: (l, 0, 0))],
        out_specs=pl.BlockSpec((None, rows, width), lambda l: (l, 0, 0)),
        out_shape=jax.ShapeDtypeStruct((depth, rows, width), BF16),
        name="mem_kv",
    )(mem2, g_mem, w_mem_kv)


def _mem_attn_kernel(q_ref, kv_ref, o_ref):
    dh = M_HEAD_DIM
    for hd in range(M_HEADS):
        q = q_ref[:, hd * dh:(hd + 1) * dh]
        k = kv_ref[:, hd * dh:(hd + 1) * dh]
        v = kv_ref[:, (M_HEADS + hd) * dh:(M_HEADS + hd + 1) * dh]
        sc = lax.dot_general(q, k, (((1,), (1,)), ((), ())), preferred_element_type=F32)
        e = jnp.exp(sc - jnp.max(sc, axis=-1, keepdims=True))
        pr = (e * (1.0 / jnp.sum(e, axis=-1, keepdims=True))).astype(BF16)
        o_ref[:, hd * dh:(hd + 1) * dh] = jnp.dot(pr, v, preferred_element_type=F32).astype(BF16)


def _mem_attn(p, mem_kv_l, batch):
    t = p.shape[0]
    s_len = t // batch
    tm = TM_MEM
    n_mem = mem_kv_l.shape[0] // batch
    width = M_HEADS * M_HEAD_DIM
    p3 = p.reshape(batch, s_len, P_WIDTH)
    kv3 = mem_kv_l.reshape(batch, n_mem, 2 * width)
    out = pl.pallas_call(
        _mem_attn_kernel,
        grid=(batch, s_len // tm),
        in_specs=[pl.BlockSpec((None, tm, width), lambda b, i: (b, i, P_QM // width)),
                  pl.BlockSpec((None, n_mem, 2 * width), lambda b, i: (b, 0, 0))],
        out_specs=pl.BlockSpec((None, tm, width), lambda b, i: (b, i, 0)),
        out_shape=jax.ShapeDtypeStruct((batch, s_len, width), BF16),
        name="mem_attn",
    )(p3, kv3)
    return out.reshape(t, width)


def _mix_kernel(x_ref, ya_ref, yb_ref, ym_ref, g_ref, wb_ref, wo_ref, o_ref):
    d = x_ref.shape[1]
    mixed = None
    for i, y_ref in enumerate((ya_ref, yb_ref, ym_ref)):
        gate = jax.nn.sigmoid(g_ref[:, i * d:(i + 1) * d].astype(F32))
        term = gate * jnp.dot(y_ref[...], wb_ref[i], preferred_element_type=F32)
        mixed = term if mixed is None else mixed + term
    o_ref[...] = x_ref[...] + jnp.dot(mixed.astype(BF16), wo_ref[...], preferred_element_type=F32)


def _mix(x, ya, yb, ym, p, wb, wo):
    t, d = x.shape
    tm = TM_MIX
    bw = ya.shape[1]
    row = lambda i: (i, 0)
    return pl.pallas_call(
        _mix_kernel,
        grid=(t // tm,),
        in_specs=[pl.BlockSpec((tm, d), row),
                  pl.BlockSpec((tm, bw), row), pl.BlockSpec((tm, bw), row), pl.BlockSpec((tm, bw), row),
                  pl.BlockSpec((tm, 3 * d), lambda i: (i, P_GATES)),
                  pl.BlockSpec((3, bw, d), lambda i: (0, 0, 0)),
                  pl.BlockSpec((d, d), lambda i: (0, 0))],
        out_specs=pl.BlockSpec((tm, d), row),
        out_shape=jax.ShapeDtypeStruct((t, d), F32),
        name="mix_out",
    )(x, ya, yb, ym, p, wb, wo)


def _swiglu_step(h, wg, wu, wd):
    gate = jnp.dot(h, wg, preferred_element_type=F32)
    up = jnp.dot(h, wu, preferred_element_type=F32)
    act = (gate * jax.nn.sigmoid(gate) * up).astype(BF16)
    return jnp.dot(act, wd, preferred_element_type=F32)


def _ffn_kernel(x_ref, g_ref, wg_ref, wu_ref, wd_ref, o_ref, h_sc, acc_sc):
    j = pl.program_id(1)

    @pl.when(j == 0)
    def _():
        h_sc[...] = _rms(x_ref[...], g_ref[...]).astype(BF16)
        acc_sc[...] = jnp.zeros(acc_sc.shape, F32)

    acc_sc[...] += _swiglu_step(h_sc[...], wg_ref[...], wu_ref[...], wd_ref[...])

    @pl.when(j == pl.num_programs(1) - 1)
    def _():
        o_ref[...] = x_ref[...] + acc_sc[...]


def _ffn(x, g, wg, wu, wd):
    t, d = x.shape
    f = wg.shape[1]
    tm, tf = TM_FFN, TF_FFN
    return pl.pallas_call(
        _ffn_kernel,
        grid=(t // tm, f // tf),
        in_specs=[pl.BlockSpec((tm, d), lambda i, j: (i, 0)),
                  pl.BlockSpec((1, d), lambda i, j: (0, 0)),
                  pl.BlockSpec((d, tf), lambda i, j: (0, j)),
                  pl.BlockSpec((d, tf), lambda i, j: (0, j)),
                  pl.BlockSpec((tf, d), lambda i, j: (j, 0))],
        out_specs=pl.BlockSpec((tm, d), lambda i, j: (i, 0)),
        out_shape=jax.ShapeDtypeStruct((t, d), F32),
        scratch_shapes=[pltpu.VMEM((tm, d), BF16), pltpu.VMEM((tm, d), F32)],
        name="ffn_dense",
    )(x, g, wg, wu, wd)


def _router_kernel(x_ref, g_ref, wr_ref, idx_ref, wgt_ref, xt_ref):
    x = x_ref[...]
    xt_ref[...] = x.reshape(xt_ref.shape)
    h = _rms(x, g_ref[...])
    logits = lax.dot_general(wr_ref[...], h, (((1,), (1,)), ((), ())),
                             precision=lax.Precision.HIGHEST, preferred_element_type=F32)
    e_id = lax.broadcasted_iota(jnp.int32, logits.shape, 0)
    m1 = jnp.max(logits, axis=0, keepdims=True)
    i1 = jnp.min(jnp.where(logits == m1, e_id, N_EXPERTS), axis=0, keepdims=True)
    rest = jnp.where(e_id == i1, -jnp.inf, logits)
    m2 = jnp.max(rest, axis=0, keepdims=True)
    i2 = jnp.min(jnp.where(rest == m2, e_id, N_EXPERTS), axis=0, keepdims=True)
    e2 = jnp.exp(m2 - m1)
    w1 = 1.0 / (1.0 + e2)
    idx_ref[...] = jnp.concatenate([i1, i2], axis=0)
    wgt_ref[...] = jnp.concatenate([w1, e2 * w1], axis=0)


def _router(x, g, wr_t):
    t, d = x.shape
    tm = TM_ROUTE
    return pl.pallas_call(
        _router_kernel,
        grid=(t // tm,),
        in_specs=[pl.BlockSpec((tm, d), lambda i: (i, 0)),
                  pl.BlockSpec((1, d), lambda i: (0, 0)),
                  pl.BlockSpec((N_EXPERTS, d), lambda i: (0, 0))],
        out_specs=[pl.BlockSpec((TOP_K, tm), lambda i: (0, i)),
                   pl.BlockSpec((TOP_K, tm), lambda i: (0, i)),
                   pl.BlockSpec((tm * SUBLANES, LANES), lambda i: (i, 0))],
        out_shape=[jax.ShapeDtypeStruct((TOP_K, t), jnp.int32),
                   jax.ShapeDtypeStruct((TOP_K, t), F32),
                   jax.ShapeDtypeStruct((t * SUBLANES, LANES), F32)],
        name="moe_router",
    )(x, g, wr_t)


def _moe_rows_per_step(n_steps):
    return -(-TM_EXP // (n_steps * SUBLANES)) * SUBLANES


def _slot_tables(top_i, n_tok, buf_rows):
    n_asg = n_tok * TOP_K
    flat_e = top_i.T.reshape(n_asg)
    order = jnp.argsort(flat_e, stable=True).astype(jnp.int32)
    counts = jnp.sum(flat_e[:, None] == jnp.arange(N_EXPERTS, dtype=jnp.int32)[None, :], axis=0, dtype=jnp.int32)
    padded = (counts + TM_EXP - 1) // TM_EXP * TM_EXP
    pad_end = jnp.cumsum(padded)
    start = jnp.cumsum(counts) - counts
    pads_before = jnp.cumsum(padded - counts) - (padded - counts)
    n_slots = n_asg + N_EXPERTS * TM_EXP
    n_blk = n_slots // TM_EXP
    extra = buf_rows - TM_EXP
    blk_e = jnp.minimum(jnp.searchsorted(pad_end, jnp.arange(n_blk, dtype=jnp.int32) * TM_EXP, side='right'),
                        N_EXPERTS - 1).astype(jnp.int32)
    slot_e = jnp.repeat(blk_e, TM_EXP)
    rank = jnp.arange(n_slots, dtype=jnp.int32) - (pad_end - padded)[slot_e]
    is_pad = rank >= counts[slot_e]
    asg = order[jnp.clip(start[slot_e] + rank, 0, n_asg - 1)]
    slot_tok = jnp.where(is_pad, 0, asg // TOP_K)
    slot_row = jnp.where(is_pad, n_asg + pads_before[slot_e] + rank - counts[slot_e],
                         (asg % TOP_K) * n_tok + slot_tok)
    gather_tok = jnp.concatenate([slot_tok.reshape(n_blk, TM_EXP), jnp.zeros((n_blk, extra), jnp.int32)], axis=1)
    spare = n_slots + jnp.arange((n_blk + 1) * buf_rows - n_slots, dtype=jnp.int32)
    lead, over = spare[:TM_EXP], spare[TM_EXP:].reshape(n_blk + 1, extra)
    scatter_row = jnp.concatenate(
        [jnp.concatenate([lead[None], slot_row.reshape(n_blk, TM_EXP)], axis=0), over], axis=1)
    return blk_e, gather_tok.reshape(-1).astype(jnp.int32), scatter_row.reshape(-1).astype(jnp.int32)


def _moe_ffn_kernel(blk_e_ref, gtok_ref, srow_ref, x_hbm, g_ref, wg_ref, wu_ref, wd_ref, o_hbm,
                    xg_sc, ys_sc, h_sc, acc_sc, gsem, ssem, *, rows_per_step):
    i, j = pl.program_id(0), pl.program_id(1)
    n_blk, n_steps = pl.num_programs(0), pl.num_programs(1)
    rows, d = h_sc.shape
    tr = d // LANES
    buf_rows = xg_sc.shape[1] // tr
    slot = i % 2
    other = 1 - slot

    def tile(ref, k):
        return ref.at[pl.ds(pl.multiple_of(k * tr, tr), tr), :]

    def gathered(s):
        return pltpu.make_async_copy(x_hbm.at[pl.ds(0, buf_rows * tr), :], xg_sc.at[s], gsem.at[s])

    def scattered(s):
        return pltpu.make_async_copy(ys_sc.at[s], o_hbm.at[pl.ds(0, buf_rows * tr), :], ssem.at[s])

    def get_row(s, k, entry):
        return pltpu.make_async_copy(tile(x_hbm, gtok_ref[entry]), tile(xg_sc.at[s], k), gsem.at[s])

    def put_row(s, k, entry):
        return pltpu.make_async_copy(tile(ys_sc.at[s], k), tile(o_hbm, srow_ref[entry]), ssem.at[s])

    @pl.when((i == 0) & (j == 0))
    def _():
        ys_sc[...] = jnp.zeros(ys_sc.shape, F32)

        def start(k, c):
            get_row(0, k, k).start()
            return c

        lax.fori_loop(0, buf_rows, start, 0)

    @pl.when(j == 0)
    def _():
        gathered(slot).wait()
        for s in range(2):
            @pl.when(slot == s)
            def _():
                xs = [xg_sc[s, pl.ds(c, rows, stride=tr), :] for c in range(tr)]
                ms = sum(jnp.sum(x * x, axis=-1, keepdims=True) for x in xs) * (1.0 / d)
                inv = lax.rsqrt(ms + EPS)
                for c, x in enumerate(xs):
                    cols = slice(c * LANES, (c + 1) * LANES)
                    h_sc[:, cols] = (x * inv * g_ref[:, cols]).astype(BF16)

        acc_sc[...] = jnp.zeros(acc_sc.shape, F32)

    nxt = jnp.minimum(i + 1, n_blk - 1)
    for r in range(rows_per_step):
        k = j * rows_per_step + r
        get_row(other, k, nxt * buf_rows + k).start(priority=r % DMA_QUEUES)
        put_row(other, k, i * buf_rows + k).start(priority=(r + 1) % DMA_QUEUES)

    acc_sc[...] += _swiglu_step(h_sc[...], wg_ref[...], wu_ref[...], wd_ref[...])

    @pl.when(j == n_steps - 1)
    def _():
        @pl.when(i > 0)
        def _():
            scattered(slot).wait()

        for s in range(2):
            @pl.when(slot == s)
            def _():
                for c in range(tr):
                    ys_sc[s, pl.ds(c, rows, stride=tr), :] = acc_sc[:, c * LANES:(c + 1) * LANES]

    @pl.when((i == n_blk - 1) & (j == n_steps - 1))
    def _():
        def start(k, c):
            put_row(slot, k, n_blk * buf_rows + k).start()
            return c

        lax.fori_loop(0, buf_rows, start, 0)
        scattered(other).wait()
        scattered(slot).wait()
        gathered(other).wait()


def _moe_ffn(x_tiles, g, wg, wu, wd, blk_e, gather_tok, scatter_row):
    d, f = wg.shape[1], wg.shape[2]
    tr = d // LANES
    assert tr % SUBLANES == 0
    tm, tf = TM_EXP, TF_FFN
    n_steps = f // tf
    rows_per_step = _moe_rows_per_step(n_steps)
    buf_rows = rows_per_step * n_steps
    n_blk = gather_tok.shape[0] // buf_rows
    return pl.pallas_call(
        functools.partial(_moe_ffn_kernel, rows_per_step=rows_per_step),
        grid_spec=pltpu.PrefetchScalarGridSpec(
            num_scalar_prefetch=3,
            grid=(n_blk, n_steps),
            in_specs=[pl.BlockSpec(memory_space=pl.ANY),
                      pl.BlockSpec((1, d), lambda i, j, be, gt, sr: (0, 0)),
                      pl.BlockSpec((None, d, tf), lambda i, j, be, gt, sr: (be[i], 0, j)),
                      pl.BlockSpec((None, d, tf), lambda i, j, be, gt, sr: (be[i], 0, j)),
                      pl.BlockSpec((None, tf, d), lambda i, j, be, gt, sr: (be[i], j, 0))],
            out_specs=pl.BlockSpec(memory_space=pl.ANY),
            scratch_shapes=[pltpu.VMEM((2, buf_rows * tr, LANES), F32), pltpu.VMEM((2, buf_rows * tr, LANES), F32),
                            pltpu.VMEM((tm, d), BF16), pltpu.VMEM((tm, d), F32),
                            pltpu.SemaphoreType.DMA((2,)), pltpu.SemaphoreType.DMA((2,))],
        ),
        out_shape=jax.ShapeDtypeStruct((scatter_row.shape[0] * tr, LANES), F32),
        name="moe_ffn",
    )(blk_e, gather_tok, scatter_row, x_tiles, g, wg, wu, wd)


def _moe_combine_kernel(x_ref, w_ref, y0_ref, y1_ref, gf_ref, o_ref, *, final_norm):
    tm, d = x_ref.shape
    y0, y1 = y0_ref[...].reshape(tm, d), y1_ref[...].reshape(tm, d)
    out = x_ref[...] + w_ref[:, 0:1] * y0 + w_ref[:, 1:2] * y1
    if final_norm:
        out = _rms(out, gf_ref[...])
    o_ref[...] = out


def _moe_combine(x, top_w, y, g_final, final_norm):
    t, d = x.shape
    tr = d // LANES
    tm = TM_COMB
    nt = t // tm
    return pl.pallas_call(
        functools.partial(_moe_combine_kernel, final_norm=final_norm),
        grid=(nt,),
        in_specs=[pl.BlockSpec((tm, d), lambda i: (i, 0)),
                  pl.BlockSpec((tm, TOP_K), lambda i: (i, 0)),
                  pl.BlockSpec((tm * tr, LANES), lambda i: (i, 0)),
                  pl.BlockSpec((tm * tr, LANES), lambda i: (nt + i, 0)),
                  pl.BlockSpec((1, d), lambda i: (0, 0))],
        out_specs=pl.BlockSpec((tm, d), lambda i: (i, 0)),
        out_shape=jax.ShapeDtypeStruct((t, d), F32),
        name="moe_combine",
    )(x, top_w, y, y, g_final)


def _final_norm_kernel(x_ref, g_ref, o_ref):
    o_ref[...] = _rms(x_ref[...], g_ref[...])


def _final_norm(x, g):
    t, d = x.shape
    tm = 1024
    return pl.pallas_call(
        _final_norm_kernel,
        grid=(t // tm,),
        in_specs=[pl.BlockSpec((tm, d), lambda i: (i, 0)), pl.BlockSpec((1, d), lambda i: (0, 0))],
        out_specs=pl.BlockSpec((tm, d), lambda i: (i, 0)),
        out_shape=jax.ShapeDtypeStruct((t, d), F32),
        name="final_norm",
    )(x, g)


def _w_in_relayout_kernel(w_ref, tok_ref, qt_ref, vt_ref):
    w = w_ref[...]
    rows = w.shape[0]
    qa, ka, va = w[:, 0:512], w[:, 512:640], w[:, 640:768]
    cq, ckv, kr = w[:, 768:1024], w[:, 1024:1152], w[:, 1152:1184]
    qm, gates = w[:, 1184:1696], w[:, 1696:4768]
    pad = jnp.zeros((rows, LANES - B_ROPE), w.dtype)
    tok_ref[...] = jnp.concatenate([gates, qm, ka, kr, pad, cq, ckv], axis=1).astype(BF16)
    qt_ref[...] = qa.T.astype(BF16)
    vt_ref[...] = va.T.astype(BF16)


def _relayout_w_in(w_in):
    depth, d, width = w_in.shape
    tr = 256
    qrows, vrows = A_HEADS * A_HEAD_DIM, A_KV_HEADS * A_HEAD_DIM
    return pl.pallas_call(
        _w_in_relayout_kernel,
        grid=(depth, d // tr),
        in_specs=[pl.BlockSpec((None, tr, width), lambda l, i: (l, i, 0))],
        out_specs=[pl.BlockSpec((None, tr, P_WIDTH), lambda l, i: (l, i, 0)),
                   pl.BlockSpec((None, qrows, tr), lambda l, i: (l, 0, i)),
                   pl.BlockSpec((None, vrows, tr), lambda l, i: (l, 0, i))],
        out_shape=[jax.ShapeDtypeStruct((depth, d, P_WIDTH), BF16),
                   jax.ShapeDtypeStruct((depth, qrows, d), BF16),
                   jax.ShapeDtypeStruct((depth, vrows, d), BF16)],
        name="w_in_relayout",
    )(w_in)


def _relayout_latent(w_uq, w_ukv):
    depth = w_uq.shape[0]
    q = w_uq.reshape(depth, B_Q_RANK, B_HEADS, B_NOPE + B_ROPE)
    q = jnp.pad(q, ((0, 0), (0, 0), (0, 0), (0, LANES - B_NOPE - B_ROPE)))
    kv = w_ukv.reshape(depth, B_KV_RANK, B_HEADS, B_NOPE + B_V)
    k = jnp.pad(kv[..., :B_NOPE], ((0, 0), (0, 0), (0, 0), (0, LANES - B_NOPE)))
    v = jnp.pad(kv[..., B_NOPE:], ((0, 0), (0, 0), (0, 0), (0, LANES - B_V)))
    flat = lambda a, r: a.reshape(depth, r, B_HEADS * LANES).astype(BF16)
    return (jnp.swapaxes(flat(q, B_Q_RANK), 1, 2), flat(k, B_KV_RANK),
            jnp.swapaxes(flat(v, B_KV_RANK), 1, 2))


def kernel(x, mem, positions, g_mix, w_in, g_q_lat, w_uq, g_kv_lat, w_ukv, sink, w_mem_kv, w_branch, w_out,
           g_ffn, w_gate_dense, w_up_dense, w_down_dense, w_router, w_gate_exp, w_up_exp, w_down_exp,
           g_mem, g_final):
    batch, s_len, d = x.shape
    depth = w_in.shape[0]
    t = batch * s_len
    row = lambda a: a.reshape(1, -1).astype(F32)

    w_in_p, wqa_t, wva_t = _relayout_w_in(w_in)
    wq_p, wk_p, wv_p = _relayout_latent(w_uq, w_ukv)
    w_branch_b, w_out_b = w_branch.astype(BF16), w_out.astype(BF16)
    wgd, wud, wdd = w_gate_dense.astype(BF16), w_up_dense.astype(BF16), w_down_dense.astype(BF16)
    wge, wue, wde = w_gate_exp.astype(BF16), w_up_exp.astype(BF16), w_down_exp.astype(BF16)
    wr_t = jnp.swapaxes(w_router, 1, 2).astype(F32)

    cos_a, sin_a, cos_b, sin_b, cos_t, sin_t = _rope_tables(positions)
    mem_kv = _mem_kv(mem.reshape(batch * mem.shape[1], d).astype(F32), row(g_mem), w_mem_kv.astype(BF16))

    xt = x.reshape(t, d).astype(F32)
    for l in range(depth):
        p, qa_t, va_t = _in_proj(xt, row(g_mix[l]), w_in_p[l], wqa_t[l], wva_t[l], cos_a, sin_a, cos_t, sin_t)
        y_a = _win_attn(p, qa_t, va_t, sink[l].astype(F32), batch)
        qt, k, vt = _mla_prep(p, row(g_q_lat[l]), row(g_kv_lat[l]), wq_p[l], wk_p[l], wv_p[l],
                              cos_b, sin_b, cos_t, sin_t, batch)
        y_b = _mla_attn(qt, k, vt)
        y_m = _mem_attn(p, mem_kv[l], batch)
        xt = _mix(xt, y_a, y_b, y_m, p, w_branch_b[l], w_out_b[l])
        i = l // 2
        if l % 2 == 0:
            xt = _ffn(xt, row(g_ffn[l]), wgd[i], wud[i], wdd[i])
            if l == depth - 1:
                xt = _final_norm(xt, row(g_final))
        else:
            top_i, top_w, x_tiles = _router(xt, row(g_ffn[l]), wr_t[i])
            buf_rows = _moe_rows_per_step(wge.shape[3] // TF_FFN) * (wge.shape[3] // TF_FFN)
            blk_e, gather_tok, scatter_row = _slot_tables(top_i, t, buf_rows)
            y = _moe_ffn(x_tiles, row(g_ffn[l]), wge[i], wue[i], wde[i], blk_e, gather_tok, scatter_row)
            xt = _moe_combine(xt, top_w.T, y, row(g_final), final_norm=(l == depth - 1))
    return xt.reshape(batch, s_len, d).astype(x.dtype)
```

```python
import functools

import jax
import jax.numpy as jnp
from jax import lax
from jax.experimental import pallas as pl
from jax.experimental.pallas import tpu as pltpu

F32 = jnp.float32
BF16 = jnp.bfloat16

ROPE_THETA = 10000.0
EPS = 1e-6
NEG_INF = -1e30
LOG2E = 1.4426950408889634

A_HEADS, A_KV_HEADS, A_HEAD_DIM, A_WINDOW = 8, 2, 64, 128
B_HEADS, B_Q_RANK, B_KV_RANK, B_NOPE, B_ROPE, B_V = 8, 256, 128, 64, 32, 64
M_HEADS, M_HEAD_DIM = 4, 128
N_EXPERTS, TOP_K = 8, 2
LANES = 128
SUBLANES = 8
BLK = 128

P_GATES = 0
P_QM = 3072
P_KA = 3584
P_CQ = 3840
P_CKV = 4096
P_WIDTH = 4224

TM_PROJ = 512
TM_PREP = 512
TM_MIX = 512
TM_MEM = 512
TM_FFN = 1024
TF_FFN = 512
TQ_WIN = 256
TQ_MLA = 2048
TK_MLA = 512
TM_ROUTE = 1024
TM_EXP = 1024
TM_COMB = 512
DMA_QUEUES = 2


def _rms(x, g):
    return x * lax.rsqrt(jnp.mean(x * x, axis=-1, keepdims=True) + EPS) * g


def _rot_tile(x, cos, sin_signed, first, shift):
    up = pltpu.roll(x, LANES - shift, 1)
    dn = pltpu.roll(x, shift, 1)
    return x * cos + jnp.where(first, up, dn) * sin_signed


def _rope_table_kernel(posr_ref, par_ref, invc_ref, ca_ref, sa_ref, cb_ref, sb_ref, cbt_ref, sbt_ref):
    ang_t = invc_ref[...] * posr_ref[...]
    ct, st = jnp.cos(ang_t), jnp.sin(ang_t)
    cbt_ref[...] = ct
    sbt_ref[...] = st
    half_a, half_b = A_HEAD_DIM // 2, B_ROPE // 2
    tm = ct.shape[1]
    ca, sa = ct[:half_a].T, st[:half_a].T
    ca_ref[...] = jnp.concatenate([ca] * (LANES // half_a), axis=1)
    sa_ref[...] = jnp.concatenate([sa] * (LANES // half_a), axis=1) * par_ref[1:2, :]
    cb, sb = ct[half_a:].T, st[half_a:].T
    one = jnp.ones((tm, B_NOPE), F32)
    pad = jnp.ones((tm, LANES - B_NOPE - B_ROPE), F32)
    cb_ref[...] = jnp.concatenate([one, cb, cb, pad], axis=1)
    sb_ref[...] = jnp.concatenate([one, sb, sb, pad], axis=1) * par_ref[3:4, :]


def _rope_tables(positions):
    t = positions.size
    tm = 2048
    lane = jnp.arange(LANES)
    half_a = A_HEAD_DIM // 2
    inv_a = jnp.power(F32(ROPE_THETA), -jnp.arange(half_a, dtype=F32) * (2.0 / A_HEAD_DIM))
    sgn_a = jnp.where((lane % A_HEAD_DIM) < half_a, -1.0, 1.0).astype(F32)
    half_b = B_ROPE // 2
    inv_b = jnp.power(F32(ROPE_THETA), -jnp.arange(half_b, dtype=F32) * (2.0 / B_ROPE))
    in_rope = (lane >= B_NOPE) & (lane < B_NOPE + B_ROPE)
    sgn_b = jnp.where(in_rope, jnp.where(lane < B_NOPE + half_b, -1.0, 1.0), 0.0).astype(F32)
    par = jnp.zeros((8, LANES), F32).at[1].set(sgn_a).at[3].set(sgn_b)
    tab = jax.ShapeDtypeStruct((t, LANES), F32)
    n_freq = half_a + half_b
    inv_col = jnp.concatenate([inv_a, inv_b]).reshape(n_freq, 1)
    tab_t = jax.ShapeDtypeStruct((n_freq, t), F32)
    return pl.pallas_call(
        _rope_table_kernel,
        grid=(t // tm,),
        in_specs=[pl.BlockSpec((1, tm), lambda i: (0, i)),
                  pl.BlockSpec((8, LANES), lambda i: (0, 0)),
                  pl.BlockSpec((n_freq, 1), lambda i: (0, 0))],
        out_specs=[pl.BlockSpec((tm, LANES), lambda i: (i, 0))] * 4
        + [pl.BlockSpec((n_freq, tm), lambda i: (0, i))] * 2,
        out_shape=[tab] * 4 + [tab_t] * 2,
        name="rope_tables",
    )(positions.reshape(1, t).astype(F32), par, inv_col)


def _in_proj_kernel(x_ref, g_ref, w_ref, wqt_ref, wvt_ref, cos_ref, sin_ref, cost_ref, sint_ref,
                    o_ref, qt_ref, vt_ref):
    h = _rms(x_ref[...], g_ref[...]).astype(BF16)
    half = A_HEAD_DIM // 2

    def proj(c0, c1):
        return jnp.dot(h, w_ref[:, c0:c1], preferred_element_type=F32)

    for c0 in range(P_GATES, P_QM, 512):
        o_ref[:, c0:c0 + 512] = proj(c0, c0 + 512).astype(BF16)
    o_ref[:, P_QM:P_KA] = (proj(P_QM, P_KA) * (M_HEAD_DIM ** -0.5)).astype(BF16)
    cos, sin = cos_ref[...], sin_ref[...]
    lane = lax.broadcasted_iota(jnp.int32, cos.shape, 1)
    ka = proj(P_KA, P_CQ)
    o_ref[:, P_KA:P_KA + LANES] = _rot_tile(ka[:, :LANES], cos, sin, (lane % A_HEAD_DIM) < half, half).astype(BF16)
    o_ref[:, P_KA + LANES:P_CQ] = ka[:, LANES:].astype(BF16)
    o_ref[:, P_CQ:P_WIDTH] = proj(P_CQ, P_WIDTH).astype(BF16)

    nt = (((1,), (1,)), ((), ()))
    ct, st = cost_ref[...], sint_ref[...]
    qscale = A_HEAD_DIM ** -0.5 * LOG2E
    qt = lax.dot_general(wqt_ref[...], h, nt, preferred_element_type=F32)
    for hd in range(A_HEADS):
        r0 = hd * A_HEAD_DIM
        x1, x2 = qt[r0:r0 + half], qt[r0 + half:r0 + A_HEAD_DIM]
        qt_ref[r0:r0 + half, :] = ((x1 * ct - x2 * st) * qscale).astype(BF16)
        qt_ref[r0 + half:r0 + A_HEAD_DIM, :] = ((x2 * ct + x1 * st) * qscale).astype(BF16)
    vt = lax.dot_general(wvt_ref[...], h, nt, preferred_element_type=F32)
    row = lax.broadcasted_iota(jnp.int32, (LANES - A_HEAD_DIM, vt.shape[1]), 0)
    ones_pad = (row == 0).astype(BF16)
    for g in range(A_KV_HEADS):
        vt_ref[g * LANES:g * LANES + A_HEAD_DIM, :] = vt[g * A_HEAD_DIM:(g + 1) * A_HEAD_DIM].astype(BF16)
        vt_ref[g * LANES + A_HEAD_DIM:(g + 1) * LANES, :] = ones_pad


def _in_proj(x, g, w, wqt, wvt, layer, cos_a, sin_a, cos_t, sin_t):
    t, d = x.shape
    tm = TM_PROJ
    half = A_HEAD_DIM // 2
    qrows, vrows = A_HEADS * A_HEAD_DIM, A_KV_HEADS * LANES
    const = lambda i: (0, 0)
    this_layer = lambda i: (layer, 0, 0)
    return pl.pallas_call(
        _in_proj_kernel,
        grid=(t // tm,),
        in_specs=[pl.BlockSpec((tm, d), lambda i: (i, 0)),
                  pl.BlockSpec((1, d), const),
                  pl.BlockSpec((None, d, P_WIDTH), this_layer),
                  pl.BlockSpec((None, qrows, d), this_layer),
                  pl.BlockSpec((None, A_KV_HEADS * A_HEAD_DIM, d), this_layer),
                  pl.BlockSpec((tm, LANES), lambda i: (i, 0)),
                  pl.BlockSpec((tm, LANES), lambda i: (i, 0)),
                  pl.BlockSpec((half, tm), lambda i: (0, i)),
                  pl.BlockSpec((half, tm), lambda i: (0, i))],
        out_specs=[pl.BlockSpec((tm, P_WIDTH), lambda i: (i, 0)),
                   pl.BlockSpec((qrows, tm), lambda i: (0, i)),
                   pl.BlockSpec((vrows, tm), lambda i: (0, i))],
        out_shape=[jax.ShapeDtypeStruct((t, P_WIDTH), BF16),
                   jax.ShapeDtypeStruct((qrows, t), BF16),
                   jax.ShapeDtypeStruct((vrows, t), BF16)],
        name="in_proj",
    )(x, g, w, wqt, wvt, cos_a, sin_a, cos_t, sin_t)


def _mla_prep_kernel(cq_ref, ckv_ref, kr_ref, gq_ref, gkv_ref, wqt_ref, wk_ref, wvt_ref,
                     cos_ref, sin_ref, cost_ref, sint_ref, qt_ref, k_ref, vt_ref):
    half = B_ROPE // 2
    nt = (((1,), (1,)), ((), ()))
    cq = _rms(cq_ref[...].astype(F32), gq_ref[...]).astype(BF16)
    ckv = _rms(ckv_ref[...].astype(F32), gkv_ref[...]).astype(BF16)

    cos, sin = cos_ref[...], sin_ref[...]
    lane = lax.broadcasted_iota(jnp.int32, cos.shape, 1)
    first = (lane >= B_NOPE) & (lane < B_NOPE + half)
    kr = _rot_tile(pltpu.roll(kr_ref[:, LANES:].astype(F32), B_NOPE, 1), cos, sin, first, half)
    k = jnp.dot(ckv, wk_ref[...], preferred_element_type=F32)
    for hd in range(B_HEADS):
        sl = slice(hd * LANES, (hd + 1) * LANES)
        k_ref[:, sl] = (k[:, sl] + kr).astype(BF16)

    ct, st = cost_ref[...], sint_ref[...]
    qscale = (B_NOPE + B_ROPE) ** -0.5 * LOG2E
    qt = lax.dot_general(wqt_ref[...], cq, nt, preferred_element_type=F32)
    vt = lax.dot_general(wvt_ref[...], ckv, nt, preferred_element_type=F32)
    row = lax.broadcasted_iota(jnp.int32, (LANES, vt.shape[1]), 0)
    ones_row = (row == B_V).astype(F32)
    for hd in range(B_HEADS):
        r0 = hd * LANES
        x1 = qt[r0 + B_NOPE:r0 + B_NOPE + half]
        x2 = qt[r0 + B_NOPE + half:r0 + B_NOPE + B_ROPE]
        qh = jnp.concatenate([qt[r0:r0 + B_NOPE], x1 * ct - x2 * st, x2 * ct + x1 * st,
                              qt[r0 + B_NOPE + B_ROPE:r0 + LANES]], axis=0)
        qt_ref[r0:r0 + LANES, :] = (qh * qscale).astype(BF16)
        vt_ref[r0:r0 + LANES, :] = (vt[r0:r0 + LANES] + ones_row).astype(BF16)


def _mla_prep(p, gq, gkv, wqt, wk, wvt, cos_b, sin_b, cos_bt, sin_bt, batch):
    t = p.shape[0]
    s_len = t // batch
    tm = TM_PREP
    nt = s_len // tm
    hw = B_HEADS * LANES
    half = B_ROPE // 2
    const = lambda b, i: (0, 0)
    tok = lambda b, i: (b * nt + i, 0)
    return pl.pallas_call(
        _mla_prep_kernel,
        grid=(batch, nt),
        in_specs=[pl.BlockSpec((tm, B_Q_RANK), lambda b, i: (b * nt + i, P_CQ // B_Q_RANK)),
                  pl.BlockSpec((tm, B_KV_RANK), lambda b, i: (b * nt + i, P_CKV // B_KV_RANK)),
                  pl.BlockSpec((tm, 2 * LANES), lambda b, i: (b * nt + i, P_KA // (2 * LANES))),
                  pl.BlockSpec((1, B_Q_RANK), const),
                  pl.BlockSpec((1, B_KV_RANK), const),
                  pl.BlockSpec((hw, B_Q_RANK), const),
                  pl.BlockSpec((B_KV_RANK, hw), const),
                  pl.BlockSpec((hw, B_KV_RANK), const),
                  pl.BlockSpec((tm, LANES), tok),
                  pl.BlockSpec((tm, LANES), tok),
                  pl.BlockSpec((half, tm), lambda b, i: (A_HEAD_DIM // 2 // half, b * nt + i)),
                  pl.BlockSpec((half, tm), lambda b, i: (A_HEAD_DIM // 2 // half, b * nt + i))],
        out_specs=[pl.BlockSpec((None, hw, tm), lambda b, i: (b, 0, i)),
                   pl.BlockSpec((None, tm, hw), lambda b, i: (b, i, 0)),
                   pl.BlockSpec((None, hw, tm), lambda b, i: (b, 0, i))],
        out_shape=[jax.ShapeDtypeStruct((batch, hw, s_len), BF16),
                   jax.ShapeDtypeStruct((batch, s_len, hw), BF16),
                   jax.ShapeDtypeStruct((batch, hw, s_len), BF16)],
        name="mla_prep",
    )(p, p, p, gq, gkv, wqt, wk, wvt, cos_b, sin_b, cos_bt, sin_bt)


def _mla_attn_kernel(qt_ref, k_ref, vt_ref, o_ref, m_sc, acc_sc, s0_sc, s1_sc, *, tk):
    n_chunks = k_ref.shape[0] // tk
    m_sc[...] = jnp.full(m_sc.shape, -jnp.inf, F32)
    acc_sc[...] = jnp.zeros(acc_sc.shape, F32)
    heads = [slice(hd * LANES, (hd + 1) * LANES) for hd in range(2)]

    def keys(j):
        return pl.ds(pl.multiple_of(j * tk, tk), tk)

    def scores(j, dst):
        for hd, sl in enumerate(heads):
            dst[hd] = jnp.dot(k_ref[keys(j), sl], qt_ref[sl, :], preferred_element_type=F32)

    def accumulate(j, src):
        for hd, sl in enumerate(heads):
            st = src[hd]
            m_old = m_sc[hd]
            m_new = jnp.maximum(m_old, jnp.max(st, axis=0, keepdims=True))
            pt = jnp.exp2(st - m_new).astype(BF16)
            acc_sc[hd] = acc_sc[hd] * jnp.exp2(m_old - m_new) + jnp.dot(
                vt_ref[sl, keys(j)], pt, preferred_element_type=F32)
            m_sc[hd] = m_new

    scores(0, s0_sc)

    def body(jj, carry):
        j = 2 * jj
        scores(j + 1, s1_sc)
        accumulate(j, s0_sc)
        scores(j + 2, s0_sc)
        accumulate(j + 1, s1_sc)
        return carry

    lax.fori_loop(0, n_chunks // 2 - 1, body, 0)
    scores(n_chunks - 1, s1_sc)
    accumulate(n_chunks - 2, s0_sc)
    accumulate(n_chunks - 1, s1_sc)
    outs = []
    for hd in range(2):
        acc = acc_sc[hd]
        outs.append(acc[:B_V] * (1.0 / acc[B_V:B_V + 1]))
    o_ref[...] = jnp.concatenate(outs, axis=0).T.astype(BF16)


def _mla_attn(qt, k, vt):
    batch, s_len, hw = k.shape
    tq = TQ_MLA
    out = pl.pallas_call(
        functools.partial(_mla_attn_kernel, tk=TK_MLA),
        grid=(batch, B_HEADS // 2, s_len // tq),
        in_specs=[pl.BlockSpec((None, 2 * LANES, tq), lambda b, p, i: (b, p, i)),
                  pl.BlockSpec((None, s_len, 2 * LANES), lambda b, p, i: (b, 0, p)),
                  pl.BlockSpec((None, 2 * LANES, s_len), lambda b, p, i: (b, p, 0))],
        out_specs=pl.BlockSpec((None, tq, LANES), lambda b, p, i: (b, i, p)),
        out_shape=jax.ShapeDtypeStruct((batch, s_len, B_HEADS * B_V), BF16),
        scratch_shapes=[pltpu.VMEM((2, 1, tq), F32), pltpu.VMEM((2, LANES, tq), F32),
                        pltpu.VMEM((2, TK_MLA, tq), F32), pltpu.VMEM((2, TK_MLA, tq), F32)],
        name="mla_attn",
    )(qt, k, vt)
    return out.reshape(batch * s_len, B_HEADS * B_V)


def _win_attn_kernel(sink_ref, qt_ref, kp_ref, kc_ref, kn_ref, vp_ref, vc_ref, vn_ref, o_ref, *, s_len):
    n = pl.program_id(1)
    grp = A_HEADS // A_KV_HEADS
    dh = A_HEAD_DIM
    tq = kc_ref.shape[0]
    kj = lax.broadcasted_iota(jnp.int32, (tq + 2 * BLK, tq), 0)
    qi = lax.broadcasted_iota(jnp.int32, (tq + 2 * BLK, tq), 1)
    k_abs = n * tq - BLK + kj
    ok = (jnp.abs(kj - BLK - qi) <= A_WINDOW) & (k_abs >= 0) & (k_abs < s_len)
    ok = jnp.concatenate([ok] * grp, axis=1)
    lane_head = lax.broadcasted_iota(jnp.int32, (1, grp * tq), 1) // tq
    kband = jnp.concatenate([kp_ref[:, :LANES], kc_ref[:, :LANES], kn_ref[:, :LANES]], axis=0)
    zeros = jnp.zeros((dh, tq), BF16)
    heads = []
    for g in range(A_KV_HEADS):
        tiles = []
        for a in range(grp):
            qh = qt_ref[(g * grp + a) * dh:(g * grp + a + 1) * dh, :]
            tiles.append(jnp.concatenate([zeros] * g + [qh] + [zeros] * (A_KV_HEADS - 1 - g), axis=0))
        qg = jnp.concatenate(tiles, axis=1)
        st = jnp.where(ok, jnp.dot(kband, qg, preferred_element_type=F32), NEG_INF)
        sk = jnp.zeros((1, grp * tq), F32)
        for a in range(grp):
            sk = jnp.where(lane_head == a, sink_ref[g * grp + a] * LOG2E, sk)
        m = jnp.maximum(jnp.max(st, axis=0, keepdims=True), sk)
        pt = jnp.exp2(st - m).astype(BF16)
        rows = slice(g * LANES, (g + 1) * LANES)
        vband = jnp.concatenate([vp_ref[rows, :], vc_ref[rows, :], vn_ref[rows, :]], axis=1)
        ot = jnp.dot(vband, pt, preferred_element_type=F32)
        o = ot[:dh] * (1.0 / (ot[dh:dh + 1] + jnp.exp2(sk - m)))
        heads.extend(o[:, a * tq:(a + 1) * tq] for a in range(grp))
    for pair in range(A_HEADS // 2):
        o_ref[:, pair * LANES:(pair + 1) * LANES] = jnp.concatenate(
            [heads[2 * pair], heads[2 * pair + 1]], axis=0).T.astype(BF16)


def _win_attn(p, qt, vt, sink, batch):
    t = p.shape[0]
    s_len = t // batch
    tq = TQ_WIN
    nq, nb, r = s_len // tq, s_len // BLK, tq // BLK
    p3 = p.reshape(batch, s_len, P_WIDTH)
    kcol = P_KA // (2 * LANES)
    lo = lambda n: jnp.maximum(n * r - 1, 0)
    hi = lambda n: jnp.minimum((n + 1) * r, nb - 1)
    kedge = lambda f: pl.BlockSpec((None, BLK, 2 * LANES), lambda b, n, s: (b, f(n), kcol))
    vedge = lambda f: pl.BlockSpec((A_KV_HEADS * LANES, BLK), lambda b, n, s: (0, b * nb + f(n)))
    out = pl.pallas_call(
        functools.partial(_win_attn_kernel, s_len=s_len),
        grid_spec=pltpu.PrefetchScalarGridSpec(
            num_scalar_prefetch=1,
            grid=(batch, nq),
            in_specs=[pl.BlockSpec((A_HEADS * A_HEAD_DIM, tq), lambda b, n, s: (0, b * nq + n)),
                      kedge(lo), pl.BlockSpec((None, tq, 2 * LANES), lambda b, n, s: (b, n, kcol)), kedge(hi),
                      vedge(lo), pl.BlockSpec((A_KV_HEADS * LANES, tq), lambda b, n, s: (0, b * nq + n)), vedge(hi)],
            out_specs=pl.BlockSpec((None, tq, A_HEADS * A_HEAD_DIM), lambda b, n, s: (b, n, 0)),
        ),
        out_shape=jax.ShapeDtypeStruct((batch, s_len, A_HEADS * A_HEAD_DIM), BF16),
        name="win_attn",
    )(sink, qt, p3, p3, p3, vt, vt, vt)
    return out.reshape(t, A_HEADS * A_HEAD_DIM)


def _mem_kv_kernel(mem_ref, g_ref, wk_ref, wvt_ref, k_ref, vt_ref):
    mem_n = _rms(mem_ref[...], g_ref[...]).astype(BF16)
    k_ref[...] = jnp.dot(mem_n, wk_ref[...], preferred_element_type=F32).astype(BF16)
    vt_ref[...] = lax.dot_general(wvt_ref[...], mem_n, (((1,), (1,)), ((), ())),
                                  preferred_element_type=F32).astype(BF16)


def _mem_kv(mem2, g_mem, w_k, w_vt):
    depth, d, width = w_k.shape
    rows = mem2.shape[0]
    return pl.pallas_call(
        _mem_kv_kernel,
        grid=(depth,),
        in_specs=[pl.BlockSpec((rows, d), lambda l: (0, 0)),
                  pl.BlockSpec((1, d), lambda l: (0, 0)),
                  pl.BlockSpec((None, d, width), lambda l: (l, 0, 0)),
                  pl.BlockSpec((None, width, d), lambda l: (l, 0, 0))],
        out_specs=[pl.BlockSpec((None, rows, width), lambda l: (l, 0, 0)),
                   pl.BlockSpec((None, width, rows), lambda l: (l, 0, 0))],
        out_shape=[jax.ShapeDtypeStruct((depth, rows, width), BF16),
                   jax.ShapeDtypeStruct((depth, width, rows), BF16)],
        name="mem_kv",
    )(mem2, g_mem, w_k, w_vt)


def _mem_attn_kernel(q_ref, k_ref, vt_ref, o_ref):
    dh = M_HEAD_DIM
    for hd in range(M_HEADS):
        sl = slice(hd * dh, (hd + 1) * dh)
        st = lax.dot_general(k_ref[:, sl], q_ref[:, sl], (((1,), (1,)), ((), ())), preferred_element_type=F32)
        e = jnp.exp(st - jnp.max(st, axis=0, keepdims=True))
        ot = jnp.dot(vt_ref[sl, :], e.astype(BF16), preferred_element_type=F32)
        o_ref[:, sl] = (ot * (1.0 / jnp.sum(e, axis=0, keepdims=True))).T.astype(BF16)


def _mem_attn(p, mem_k, mem_vt, layer, batch):
    t = p.shape[0]
    s_len = t // batch
    tm = TM_MEM
    n_mem = mem_k.shape[1] // batch
    width = M_HEADS * M_HEAD_DIM
    p3 = p.reshape(batch, s_len, P_WIDTH)
    out = pl.pallas_call(
        _mem_attn_kernel,
        grid=(batch, s_len // tm),
        in_specs=[pl.BlockSpec((None, tm, width), lambda b, i: (b, i, P_QM // width)),
                  pl.BlockSpec((None, n_mem, width), lambda b, i: (layer, b, 0)),
                  pl.BlockSpec((None, width, n_mem), lambda b, i: (layer, 0, b))],
        out_specs=pl.BlockSpec((None, tm, width), lambda b, i: (b, i, 0)),
        out_shape=jax.ShapeDtypeStruct((batch, s_len, width), BF16),
        name="mem_attn",
    )(p3, mem_k, mem_vt)
    return out.reshape(t, width)


def _mix_kernel(x_ref, ya_ref, yb_ref, ym_ref, g_ref, wb_ref, wo_ref, o_ref):
    d = x_ref.shape[1]
    mixed = None
    for i, y_ref in enumerate((ya_ref, yb_ref, ym_ref)):
        gate = jax.nn.sigmoid(g_ref[:, i * d:(i + 1) * d].astype(F32))
        term = gate * jnp.dot(y_ref[...], wb_ref[i], preferred_element_type=F32)
        mixed = term if mixed is None else mixed + term
    o_ref[...] = x_ref[...] + jnp.dot(mixed.astype(BF16), wo_ref[...], preferred_element_type=F32)


def _mix(x, ya, yb, ym, p, wb, wo):
    t, d = x.shape
    tm = TM_MIX
    bw = ya.shape[1]
    row = lambda i: (i, 0)
    return pl.pallas_call(
        _mix_kernel,
        grid=(t // tm,),
        in_specs=[pl.BlockSpec((tm, d), row),
                  pl.BlockSpec((tm, bw), row), pl.BlockSpec((tm, bw), row), pl.BlockSpec((tm, bw), row),
                  pl.BlockSpec((tm, 3 * d), lambda i: (i, P_GATES)),
                  pl.BlockSpec((3, bw, d), lambda i: (0, 0, 0)),
                  pl.BlockSpec((d, d), lambda i: (0, 0))],
        out_specs=pl.BlockSpec((tm, d), row),
        out_shape=jax.ShapeDtypeStruct((t, d), F32),
        name="mix_out",
    )(x, ya, yb, ym, p, wb, wo)


def _swiglu_step(h, wg, wu, wd):
    gate = jnp.dot(h, wg, preferred_element_type=F32)
    up = jnp.dot(h, wu, preferred_element_type=F32)
    act = (gate * jax.nn.sigmoid(gate) * up).astype(BF16)
    return jnp.dot(act, wd, preferred_element_type=F32)


def _ffn_kernel(x_ref, g_ref, wg_ref, wu_ref, wd_ref, o_ref, h_sc, acc_sc):
    j = pl.program_id(1)

    @pl.when(j == 0)
    def _():
        h_sc[...] = _rms(x_ref[...], g_ref[...]).astype(BF16)
        acc_sc[...] = jnp.zeros(acc_sc.shape, F32)

    acc_sc[...] += _swiglu_step(h_sc[...], wg_ref[...], wu_ref[...], wd_ref[...])

    @pl.when(j == pl.num_programs(1) - 1)
    def _():
        o_ref[...] = x_ref[...] + acc_sc[...]


def _ffn(x, g, wg, wu, wd):
    t, d = x.shape
    f = wg.shape[1]
    tm, tf = TM_FFN, TF_FFN
    return pl.pallas_call(
        _ffn_kernel,
        grid=(t // tm, f // tf),
        in_specs=[pl.BlockSpec((tm, d), lambda i, j: (i, 0)),
                  pl.BlockSpec((1, d), lambda i, j: (0, 0)),
                  pl.BlockSpec((d, tf), lambda i, j: (0, j)),
                  pl.BlockSpec((d, tf), lambda i, j: (0, j)),
                  pl.BlockSpec((tf, d), lambda i, j: (j, 0))],
        out_specs=pl.BlockSpec((tm, d), lambda i, j: (i, 0)),
        out_shape=jax.ShapeDtypeStruct((t, d), F32),
        scratch_shapes=[pltpu.VMEM((tm, d), BF16), pltpu.VMEM((tm, d), F32)],
        name="ffn_dense",
    )(x, g, wg, wu, wd)


def _router_kernel(x_ref, g_ref, wr_ref, idx_ref, wgt_ref, xt_ref):
    x = x_ref[...]
    xt_ref[...] = x.reshape(xt_ref.shape)
    h = _rms(x, g_ref[...])
    logits = lax.dot_general(wr_ref[...], h, (((1,), (1,)), ((), ())),
                             precision=lax.Precision.HIGHEST, preferred_element_type=F32)
    e_id = lax.broadcasted_iota(jnp.int32, logits.shape, 0)
    m1 = jnp.max(logits, axis=0, keepdims=True)
    i1 = jnp.min(jnp.where(logits == m1, e_id, N_EXPERTS), axis=0, keepdims=True)
    rest = jnp.where(e_id == i1, -jnp.inf, logits)
    m2 = jnp.max(rest, axis=0, keepdims=True)
    i2 = jnp.min(jnp.where(rest == m2, e_id, N_EXPERTS), axis=0, keepdims=True)
    e2 = jnp.exp(m2 - m1)
    w1 = 1.0 / (1.0 + e2)
    idx_ref[...] = jnp.concatenate([i1, i2], axis=0)
    wgt_ref[...] = jnp.concatenate([w1, e2 * w1], axis=0)


def _router(x, g, wr_t):
    t, d = x.shape
    tm = TM_ROUTE
    return pl.pallas_call(
        _router_kernel,
        grid=(t // tm,),
        in_specs=[pl.BlockSpec((tm, d), lambda i: (i, 0)),
                  pl.BlockSpec((1, d), lambda i: (0, 0)),
                  pl.BlockSpec((N_EXPERTS, d), lambda i: (0, 0))],
        out_specs=[pl.BlockSpec((TOP_K, tm), lambda i: (0, i)),
                   pl.BlockSpec((TOP_K, tm), lambda i: (0, i)),
                   pl.BlockSpec((tm * SUBLANES, LANES), lambda i: (i, 0))],
        out_shape=[jax.ShapeDtypeStruct((TOP_K, t), jnp.int32),
                   jax.ShapeDtypeStruct((TOP_K, t), F32),
                   jax.ShapeDtypeStruct((t * SUBLANES, LANES), F32)],
        name="moe_router",
    )(x, g, wr_t)


def _moe_rows_per_step(n_steps):
    return -(-TM_EXP // (n_steps * SUBLANES)) * SUBLANES


def _slot_tables(top_i, n_tok, buf_rows):
    n_asg = n_tok * TOP_K
    flat_e = top_i.T.reshape(n_asg)
    order = jnp.argsort(flat_e, stable=True).astype(jnp.int32)
    counts = jnp.sum(flat_e[:, None] == jnp.arange(N_EXPERTS, dtype=jnp.int32)[None, :], axis=0, dtype=jnp.int32)
    padded = (counts + TM_EXP - 1) // TM_EXP * TM_EXP
    pad_end = jnp.cumsum(padded)
    start = jnp.cumsum(counts) - counts
    pads_before = jnp.cumsum(padded - counts) - (padded - counts)
    n_slots = n_asg + N_EXPERTS * TM_EXP
    n_blk = n_slots // TM_EXP
    extra = buf_rows - TM_EXP
    blk_e = jnp.minimum(jnp.searchsorted(pad_end, jnp.arange(n_blk, dtype=jnp.int32) * TM_EXP, side='right'),
                        N_EXPERTS - 1).astype(jnp.int32)
    slot_e = jnp.repeat(blk_e, TM_EXP)
    rank = jnp.arange(n_slots, dtype=jnp.int32) - (pad_end - padded)[slot_e]
    is_pad = rank >= counts[slot_e]
    asg = order[jnp.clip(start[slot_e] + rank, 0, n_asg - 1)]
    slot_tok = jnp.where(is_pad, 0, asg // TOP_K)
    slot_row = jnp.where(is_pad, n_asg + pads_before[slot_e] + rank - counts[slot_e],
                         (asg % TOP_K) * n_tok + slot_tok)
    gather_tok = jnp.concatenate([slot_tok.reshape(n_blk, TM_EXP), jnp.zeros((n_blk, extra), jnp.int32)], axis=1)
    spare = n_slots + jnp.arange((n_blk + 1) * buf_rows - n_slots, dtype=jnp.int32)
    lead, over = spare[:TM_EXP], spare[TM_EXP:].reshape(n_blk + 1, extra)
    scatter_row = jnp.concatenate(
        [jnp.concatenate([lead[None], slot_row.reshape(n_blk, TM_EXP)], axis=0), over], axis=1)
    n_live = (pad_end[-1] // TM_EXP).astype(jnp.int32)
    blk_meta = jnp.concatenate([blk_e, n_live[None]])
    return blk_meta, gather_tok.reshape(-1).astype(jnp.int32), scatter_row.reshape(-1).astype(jnp.int32)


def _moe_ffn_kernel(blk_e_ref, gtok_ref, srow_ref, x_hbm, g_ref, wg_ref, wu_ref, wd_ref, o_hbm,
                    xg_sc, ys_sc, h_sc, acc_sc, gsem, ssem, *, rows_per_step):
    i, j = pl.program_id(0), pl.program_id(1)
    n_blk, n_steps = pl.num_programs(0), pl.num_programs(1)
    rows, d = h_sc.shape
    tr = d // LANES
    buf_rows = xg_sc.shape[1] // tr
    slot = i % 2
    other = 1 - slot

    def tile(ref, k):
        return ref.at[pl.ds(pl.multiple_of(k * tr, tr), tr), :]

    def gathered(s):
        return pltpu.make_async_copy(x_hbm.at[pl.ds(0, buf_rows * tr), :], xg_sc.at[s], gsem.at[s])

    def scattered(s):
        return pltpu.make_async_copy(ys_sc.at[s], o_hbm.at[pl.ds(0, buf_rows * tr), :], ssem.at[s])

    def get_row(s, k, entry):
        return pltpu.make_async_copy(tile(x_hbm, gtok_ref[entry]), tile(xg_sc.at[s], k), gsem.at[s])

    def put_row(s, k, entry):
        return pltpu.make_async_copy(tile(ys_sc.at[s], k), tile(o_hbm, srow_ref[entry]), ssem.at[s])

    @pl.when((i == 0) & (j == 0))
    def _():
        ys_sc[...] = jnp.zeros(ys_sc.shape, F32)

        def start(k, c):
            get_row(0, k, k).start()
            return c

        lax.fori_loop(0, buf_rows, start, 0)

    @pl.when(j == 0)
    def _():
        gathered(slot).wait()
        for s in range(2):
            @pl.when(slot == s)
            def _():
                xs = [xg_sc[s, pl.ds(c, rows, stride=tr), :] for c in range(tr)]
                ms = sum(jnp.sum(x * x, axis=-1, keepdims=True) for x in xs) * (1.0 / d)
                inv = lax.rsqrt(ms + EPS)
                for c, x in enumerate(xs):
                    cols = slice(c * LANES, (c + 1) * LANES)
                    h_sc[:, cols] = (x * inv * g_ref[:, cols]).astype(BF16)

        acc_sc[...] = jnp.zeros(acc_sc.shape, F32)

    nxt = jnp.minimum(i + 1, n_blk - 1)

    def issue_row_copies():
        for r in range(rows_per_step):
            k = j * rows_per_step + r
            get_row(other, k, nxt * buf_rows + k).start(priority=r % DMA_QUEUES)
            put_row(other, k, i * buf_rows + k).start(priority=(r + 1) % DMA_QUEUES)

    live = i < blk_e_ref[n_blk]

    @pl.when(live)
    def _():
        issue_row_copies()
        acc_sc[...] += _swiglu_step(h_sc[...], wg_ref[...], wu_ref[...], wd_ref[...])

    @pl.when(jnp.logical_not(live))
    def _():
        issue_row_copies()

    @pl.when(j == n_steps - 1)
    def _():
        @pl.when(i > 0)
        def _():
            scattered(slot).wait()

        for s in range(2):
            @pl.when(slot == s)
            def _():
                for c in range(tr):
                    ys_sc[s, pl.ds(c, rows, stride=tr), :] = acc_sc[:, c * LANES:(c + 1) * LANES]

    @pl.when((i == n_blk - 1) & (j == n_steps - 1))
    def _():
        def start(k, c):
            put_row(slot, k, n_blk * buf_rows + k).start()
            return c

        lax.fori_loop(0, buf_rows, start, 0)
        scattered(other).wait()
        scattered(slot).wait()
        gathered(other).wait()


def _moe_ffn(x_tiles, g, wg, wu, wd, blk_e, gather_tok, scatter_row):
    d, f = wg.shape[1], wg.shape[2]
    tr = d // LANES
    assert tr % SUBLANES == 0
    tm, tf = TM_EXP, TF_FFN
    n_steps = f // tf
    rows_per_step = _moe_rows_per_step(n_steps)
    buf_rows = rows_per_step * n_steps
    n_blk = gather_tok.shape[0] // buf_rows

    def ff(i, j, be):
        return jnp.where(i < be[n_blk], j, n_steps - 1)

    return pl.pallas_call(
        functools.partial(_moe_ffn_kernel, rows_per_step=rows_per_step),
        grid_spec=pltpu.PrefetchScalarGridSpec(
            num_scalar_prefetch=3,
            grid=(n_blk, n_steps),
            in_specs=[pl.BlockSpec(memory_space=pl.ANY),
                      pl.BlockSpec((1, d), lambda i, j, be, gt, sr: (0, 0)),
                      pl.BlockSpec((None, d, tf), lambda i, j, be, gt, sr: (be[i], 0, ff(i, j, be))),
                      pl.BlockSpec((None, d, tf), lambda i, j, be, gt, sr: (be[i], 0, ff(i, j, be))),
                      pl.BlockSpec((None, tf, d), lambda i, j, be, gt, sr: (be[i], ff(i, j, be), 0))],
            out_specs=pl.BlockSpec(memory_space=pl.ANY),
            scratch_shapes=[pltpu.VMEM((2, buf_rows * tr, LANES), F32), pltpu.VMEM((2, buf_rows * tr, LANES), F32),
                            pltpu.VMEM((tm, d), BF16), pltpu.VMEM((tm, d), F32),
                            pltpu.SemaphoreType.DMA((2,)), pltpu.SemaphoreType.DMA((2,))],
        ),
        out_shape=jax.ShapeDtypeStruct((scatter_row.shape[0] * tr, LANES), F32),
        name="moe_ffn",
    )(blk_e, gather_tok, scatter_row, x_tiles, g, wg, wu, wd)


def _moe_combine_kernel(x_ref, w_ref, y0_ref, y1_ref, gf_ref, o_ref, *, final_norm):
    tm, d = x_ref.shape
    y0, y1 = y0_ref[...].reshape(tm, d), y1_ref[...].reshape(tm, d)
    out = x_ref[...] + w_ref[:, 0:1] * y0 + w_ref[:, 1:2] * y1
    if final_norm:
        out = _rms(out, gf_ref[...])
    o_ref[...] = out


def _moe_combine(x, top_w, y, g_final, final_norm):
    t, d = x.shape
    tr = d // LANES
    tm = TM_COMB
    nt = t // tm
    return pl.pallas_call(
        functools.partial(_moe_combine_kernel, final_norm=final_norm),
        grid=(nt,),
        in_specs=[pl.BlockSpec((tm, d), lambda i: (i, 0)),
                  pl.BlockSpec((tm, TOP_K), lambda i: (i, 0)),
                  pl.BlockSpec((tm * tr, LANES), lambda i: (i, 0)),
                  pl.BlockSpec((tm * tr, LANES), lambda i: (nt + i, 0)),
                  pl.BlockSpec((1, d), lambda i: (0, 0))],
        out_specs=pl.BlockSpec((tm, d), lambda i: (i, 0)),
        out_shape=jax.ShapeDtypeStruct((t, d), F32),
        name="moe_combine",
    )(x, top_w, y, y, g_final)


def _final_norm_kernel(x_ref, g_ref, o_ref):
    o_ref[...] = _rms(x_ref[...], g_ref[...])


def _final_norm(x, g):
    t, d = x.shape
    tm = 1024
    return pl.pallas_call(
        _final_norm_kernel,
        grid=(t // tm,),
        in_specs=[pl.BlockSpec((tm, d), lambda i: (i, 0)), pl.BlockSpec((1, d), lambda i: (0, 0))],
        out_specs=pl.BlockSpec((tm, d), lambda i: (i, 0)),
        out_shape=jax.ShapeDtypeStruct((t, d), F32),
        name="final_norm",
    )(x, g)


def _w_in_relayout_kernel(w_ref, tok_ref, qt_ref, vt_ref):
    w = w_ref[...]
    rows = w.shape[0]
    qa, ka, va = w[:, 0:512], w[:, 512:640], w[:, 640:768]
    cq, ckv, kr = w[:, 768:1024], w[:, 1024:1152], w[:, 1152:1184]
    qm, gates = w[:, 1184:1696], w[:, 1696:4768]
    pad = jnp.zeros((rows, LANES - B_ROPE), w.dtype)
    tok_ref[...] = jnp.concatenate([gates, qm, ka, kr, pad, cq, ckv], axis=1).astype(BF16)
    qt_ref[...] = qa.T.astype(BF16)
    vt_ref[...] = va.T.astype(BF16)


def _relayout_w_in(w_in):
    depth, d, width = w_in.shape
    tr = 256
    qrows, vrows = A_HEADS * A_HEAD_DIM, A_KV_HEADS * A_HEAD_DIM
    return pl.pallas_call(
        _w_in_relayout_kernel,
        grid=(depth, d // tr),
        in_specs=[pl.BlockSpec((None, tr, width), lambda l, i: (l, i, 0))],
        out_specs=[pl.BlockSpec((None, tr, P_WIDTH), lambda l, i: (l, i, 0)),
                   pl.BlockSpec((None, qrows, tr), lambda l, i: (l, 0, i)),
                   pl.BlockSpec((None, vrows, tr), lambda l, i: (l, 0, i))],
        out_shape=[jax.ShapeDtypeStruct((depth, d, P_WIDTH), BF16),
                   jax.ShapeDtypeStruct((depth, qrows, d), BF16),
                   jax.ShapeDtypeStruct((depth, vrows, d), BF16)],
        name="w_in_relayout",
    )(w_in)


def _relayout_latent(w_uq, w_ukv):
    depth = w_uq.shape[0]
    q = w_uq.reshape(depth, B_Q_RANK, B_HEADS, B_NOPE + B_ROPE)
    q = jnp.pad(q, ((0, 0), (0, 0), (0, 0), (0, LANES - B_NOPE - B_ROPE)))
    kv = w_ukv.reshape(depth, B_KV_RANK, B_HEADS, B_NOPE + B_V)
    k = jnp.pad(kv[..., :B_NOPE], ((0, 0), (0, 0), (0, 0), (0, LANES - B_NOPE)))
    v = jnp.pad(kv[..., B_NOPE:], ((0, 0), (0, 0), (0, 0), (0, LANES - B_V)))
    flat = lambda a, r: a.reshape(depth, r, B_HEADS * LANES).astype(BF16)
    return (jnp.swapaxes(flat(q, B_Q_RANK), 1, 2), flat(k, B_KV_RANK),
            jnp.swapaxes(flat(v, B_KV_RANK), 1, 2))


def kernel(x, mem, positions, g_mix, w_in, g_q_lat, w_uq, g_kv_lat, w_ukv, sink, w_mem_kv, w_branch, w_out,
           g_ffn, w_gate_dense, w_up_dense, w_down_dense, w_router, w_gate_exp, w_up_exp, w_down_exp,
           g_mem, g_final):
    batch, s_len, d = x.shape
    depth = w_in.shape[0]
    t = batch * s_len
    row = lambda a: a.reshape(1, -1).astype(F32)

    w_in_p, wqa_t, wva_t = _relayout_w_in(w_in)
    wq_p, wk_p, wv_p = _relayout_latent(w_uq, w_ukv)
    bf = lambda w: w.astype(BF16)
    wr_t = jnp.swapaxes(w_router, 1, 2).astype(F32)

    cos_a, sin_a, cos_b, sin_b, cos_t, sin_t = _rope_tables(positions)
    mw = M_HEADS * M_HEAD_DIM
    mem_k, mem_vt = _mem_kv(mem.reshape(batch * mem.shape[1], d).astype(F32), row(g_mem),
                            bf(w_mem_kv[:, :, :mw]), bf(jnp.swapaxes(w_mem_kv[:, :, mw:], 1, 2)))

    xt = x.reshape(t, d).astype(F32)
    for l in range(depth):
        p, qa_t, va_t = _in_proj(xt, row(g_mix[l]), w_in_p, wqa_t, wva_t, l, cos_a, sin_a, cos_t, sin_t)
        y_a = _win_attn(p, qa_t, va_t, sink[l].astype(F32), batch)
        qt, k, vt = _mla_prep(p, row(g_q_lat[l]), row(g_kv_lat[l]), wq_p[l], wk_p[l], wv_p[l],
                              cos_b, sin_b, cos_t, sin_t, batch)
        y_b = _mla_attn(qt, k, vt)
        y_m = _mem_attn(p, mem_k, mem_vt, l, batch)
        xt = _mix(xt, y_a, y_b, y_m, p, bf(w_branch[l]), bf(w_out[l]))
        i = l // 2
        if l % 2 == 0:
            xt = _ffn(xt, row(g_ffn[l]), bf(w_gate_dense[i]), bf(w_up_dense[i]), bf(w_down_dense[i]))
            if l == depth - 1:
                xt = _final_norm(xt, row(g_final))
        else:
            top_i, top_w, x_tiles = _router(xt, row(g_ffn[l]), wr_t[i])
            n_steps = w_gate_exp.shape[3] // TF_FFN
            buf_rows = _moe_rows_per_step(n_steps) * n_steps
            blk_e, gather_tok, scatter_row = _slot_tables(top_i, t, buf_rows)
            y = _moe_ffn(x_tiles, row(g_ffn[l]), bf(w_gate_exp[i]), bf(w_up_exp[i]), bf(w_down_exp[i]),
                         blk_e, gather_tok, scatter_row)
            xt = _moe_combine(xt, top_w.T, y, row(g_final), final_norm=(l == depth - 1))
    return xt.reshape(batch, s_len, d).astype(x.dtype)
```

```python
import functools

import jax
import jax.numpy as jnp
from jax import lax
from jax.experimental import pallas as pl
from jax.experimental.pallas import tpu as pltpu

F32 = jnp.float32
BF16 = jnp.bfloat16

ROPE_THETA = 10000.0
EPS = 1e-6
NEG_INF = -1e30
LOG2E = 1.4426950408889634

A_HEADS, A_KV_HEADS, A_HEAD_DIM, A_WINDOW = 8, 2, 64, 128
B_HEADS, B_Q_RANK, B_KV_RANK, B_NOPE, B_ROPE, B_V = 8, 256, 128, 64, 32, 64
M_HEADS, M_HEAD_DIM = 4, 128
N_EXPERTS, TOP_K = 8, 2
LANES = 128
SUBLANES = 8
BLK = 128

P_GATES = 0
P_QM = 3072
P_KA = 3584
P_CQ = 3840
P_CKV = 4096
P_WIDTH = 4224

TM_PROJ = 512
TM_PREP = 512
TM_MIX = 512
TM_MEM = 512
TM_FFN = 1024
TF_FFN = 512
TQ_WIN = 256
TQ_MLA = 2048
TK_MLA = 512
TM_ROUTE = 1024
TM_EXP = 1024
TM_COMB = 512
DMA_QUEUES = 2


def _rms(x, g):
    return x * lax.rsqrt(jnp.mean(x * x, axis=-1, keepdims=True) + EPS) * g


def _rot_tile(x, cos, sin_signed, first, shift):
    up = pltpu.roll(x, LANES - shift, 1)
    dn = pltpu.roll(x, shift, 1)
    return x * cos + jnp.where(first, up, dn) * sin_signed


def _rope_table_kernel(posr_ref, par_ref, invc_ref, ca_ref, sa_ref, cb_ref, sb_ref, cbt_ref, sbt_ref):
    ang_t = invc_ref[...] * posr_ref[...]
    ct, st = jnp.cos(ang_t), jnp.sin(ang_t)
    cbt_ref[...] = ct
    sbt_ref[...] = st
    half_a, half_b = A_HEAD_DIM // 2, B_ROPE // 2
    tm = ct.shape[1]
    ca, sa = ct[:half_a].T, st[:half_a].T
    ca_ref[...] = jnp.concatenate([ca] * (LANES // half_a), axis=1)
    sa_ref[...] = jnp.concatenate([sa] * (LANES // half_a), axis=1) * par_ref[1:2, :]
    cb, sb = ct[half_a:].T, st[half_a:].T
    one = jnp.ones((tm, B_NOPE), F32)
    pad = jnp.ones((tm, LANES - B_NOPE - B_ROPE), F32)
    cb_ref[...] = jnp.concatenate([one, cb, cb, pad], axis=1)
    sb_ref[...] = jnp.concatenate([one, sb, sb, pad], axis=1) * par_ref[3:4, :]


def _rope_tables(positions):
    t = positions.size
    tm = 2048
    lane = jnp.arange(LANES)
    half_a = A_HEAD_DIM // 2
    inv_a = jnp.power(F32(ROPE_THETA), -jnp.arange(half_a, dtype=F32) * (2.0 / A_HEAD_DIM))
    sgn_a = jnp.where((lane % A_HEAD_DIM) < half_a, -1.0, 1.0).astype(F32)
    half_b = B_ROPE // 2
    inv_b = jnp.power(F32(ROPE_THETA), -jnp.arange(half_b, dtype=F32) * (2.0 / B_ROPE))
    in_rope = (lane >= B_NOPE) & (lane < B_NOPE + B_ROPE)
    sgn_b = jnp.where(in_rope, jnp.where(lane < B_NOPE + half_b, -1.0, 1.0), 0.0).astype(F32)
    par = jnp.zeros((8, LANES), F32).at[1].set(sgn_a).at[3].set(sgn_b)
    tab = jax.ShapeDtypeStruct((t, LANES), F32)
    n_freq = half_a + half_b
    inv_col = jnp.concatenate([inv_a, inv_b]).reshape(n_freq, 1)
    tab_t = jax.ShapeDtypeStruct((n_freq, t), F32)
    return pl.pallas_call(
        _rope_table_kernel,
        grid=(t // tm,),
        in_specs=[pl.BlockSpec((1, tm), lambda i: (0, i)),
                  pl.BlockSpec((8, LANES), lambda i: (0, 0)),
                  pl.BlockSpec((n_freq, 1), lambda i: (0, 0))],
        out_specs=[pl.BlockSpec((tm, LANES), lambda i: (i, 0))] * 4
        + [pl.BlockSpec((n_freq, tm), lambda i: (0, i))] * 2,
        out_shape=[tab] * 4 + [tab_t] * 2,
        name="rope_tables",
    )(positions.reshape(1, t).astype(F32), par, inv_col)


def _in_proj_kernel(x_ref, g_ref, w_ref, wqt_ref, wvt_ref, cos_ref, sin_ref, cost_ref, sint_ref,
                    o_ref, qt_ref, vt_ref):
    h = _rms(x_ref[...], g_ref[...]).astype(BF16)
    half = A_HEAD_DIM // 2

    def proj(c0, c1):
        return jnp.dot(h, w_ref[:, c0:c1], preferred_element_type=F32)

    for c0 in range(P_GATES, P_QM, 512):
        o_ref[:, c0:c0 + 512] = proj(c0, c0 + 512).astype(BF16)
    o_ref[:, P_QM:P_KA] = (proj(P_QM, P_KA) * (M_HEAD_DIM ** -0.5)).astype(BF16)
    cos, sin = cos_ref[...], sin_ref[...]
    lane = lax.broadcasted_iota(jnp.int32, cos.shape, 1)
    ka = proj(P_KA, P_CQ)
    o_ref[:, P_KA:P_KA + LANES] = _rot_tile(ka[:, :LANES], cos, sin, (lane % A_HEAD_DIM) < half, half).astype(BF16)
    o_ref[:, P_KA + LANES:P_CQ] = ka[:, LANES:].astype(BF16)
    o_ref[:, P_CQ:P_WIDTH] = proj(P_CQ, P_WIDTH).astype(BF16)

    nt = (((1,), (1,)), ((), ()))
    ct, st = cost_ref[...], sint_ref[...]
    qscale = A_HEAD_DIM ** -0.5 * LOG2E
    qt = lax.dot_general(wqt_ref[...], h, nt, preferred_element_type=F32)
    for hd in range(A_HEADS):
        r0 = hd * A_HEAD_DIM
        x1, x2 = qt[r0:r0 + half], qt[r0 + half:r0 + A_HEAD_DIM]
        qt_ref[r0:r0 + half, :] = ((x1 * ct - x2 * st) * qscale).astype(BF16)
        qt_ref[r0 + half:r0 + A_HEAD_DIM, :] = ((x2 * ct + x1 * st) * qscale).astype(BF16)
    vt = lax.dot_general(wvt_ref[...], h, nt, preferred_element_type=F32)
    row = lax.broadcasted_iota(jnp.int32, (LANES - A_HEAD_DIM, vt.shape[1]), 0)
    ones_pad = (row == 0).astype(BF16)
    for g in range(A_KV_HEADS):
        vt_ref[g * LANES:g * LANES + A_HEAD_DIM, :] = vt[g * A_HEAD_DIM:(g + 1) * A_HEAD_DIM].astype(BF16)
        vt_ref[g * LANES + A_HEAD_DIM:(g + 1) * LANES, :] = ones_pad


def _in_proj(x, g, w, wqt, wvt, layer, cos_a, sin_a, cos_t, sin_t):
    t, d = x.shape
    tm = TM_PROJ
    half = A_HEAD_DIM // 2
    qrows, vrows = A_HEADS * A_HEAD_DIM, A_KV_HEADS * LANES
    const = lambda i: (0, 0)
    this_layer = lambda i: (layer, 0, 0)
    return pl.pallas_call(
        _in_proj_kernel,
        grid=(t // tm,),
        in_specs=[pl.BlockSpec((tm, d), lambda i: (i, 0)),
                  pl.BlockSpec((1, d), const),
                  pl.BlockSpec((None, d, P_WIDTH), this_layer),
                  pl.BlockSpec((None, qrows, d), this_layer),
                  pl.BlockSpec((None, A_KV_HEADS * A_HEAD_DIM, d), this_layer),
                  pl.BlockSpec((tm, LANES), lambda i: (i, 0)),
                  pl.BlockSpec((tm, LANES), lambda i: (i, 0)),
                  pl.BlockSpec((half, tm), lambda i: (0, i)),
                  pl.BlockSpec((half, tm), lambda i: (0, i))],
        out_specs=[pl.BlockSpec((tm, P_WIDTH), lambda i: (i, 0)),
                   pl.BlockSpec((qrows, tm), lambda i: (0, i)),
                   pl.BlockSpec((vrows, tm), lambda i: (0, i))],
        out_shape=[jax.ShapeDtypeStruct((t, P_WIDTH), BF16),
                   jax.ShapeDtypeStruct((qrows, t), BF16),
                   jax.ShapeDtypeStruct((vrows, t), BF16)],
        name="in_proj",
    )(x, g, w, wqt, wvt, cos_a, sin_a, cos_t, sin_t)


def _mla_prep_kernel(cq_ref, ckv_ref, kr_ref, gq_ref, gkv_ref, wqt_ref, wk_ref, wvt_ref,
                     cos_ref, sin_ref, cost_ref, sint_ref, qt_ref, k_ref, vt_ref):
    half = B_ROPE // 2
    nt = (((1,), (1,)), ((), ()))
    cq = _rms(cq_ref[...].astype(F32), gq_ref[...]).astype(BF16)
    ckv = _rms(ckv_ref[...].astype(F32), gkv_ref[...]).astype(BF16)

    cos, sin = cos_ref[...], sin_ref[...]
    lane = lax.broadcasted_iota(jnp.int32, cos.shape, 1)
    first = (lane >= B_NOPE) & (lane < B_NOPE + half)
    kr = _rot_tile(pltpu.roll(kr_ref[:, LANES:].astype(F32), B_NOPE, 1), cos, sin, first, half)
    k = jnp.dot(ckv, wk_ref[...], preferred_element_type=F32)
    for hd in range(B_HEADS):
        sl = slice(hd * LANES, (hd + 1) * LANES)
        k_ref[:, sl] = (k[:, sl] + kr).astype(BF16)

    ct, st = cost_ref[...], sint_ref[...]
    qscale = (B_NOPE + B_ROPE) ** -0.5 * LOG2E
    qt = lax.dot_general(wqt_ref[...], cq, nt, preferred_element_type=F32)
    vt = lax.dot_general(wvt_ref[...], ckv, nt, preferred_element_type=F32)
    row = lax.broadcasted_iota(jnp.int32, (LANES, vt.shape[1]), 0)
    ones_row = (row == B_V).astype(F32)
    for hd in range(B_HEADS):
        r0 = hd * LANES
        x1 = qt[r0 + B_NOPE:r0 + B_NOPE + half]
        x2 = qt[r0 + B_NOPE + half:r0 + B_NOPE + B_ROPE]
        qh = jnp.concatenate([qt[r0:r0 + B_NOPE], x1 * ct - x2 * st, x2 * ct + x1 * st,
                              qt[r0 + B_NOPE + B_ROPE:r0 + LANES]], axis=0)
        qt_ref[r0:r0 + LANES, :] = (qh * qscale).astype(BF16)
        vt_ref[r0:r0 + LANES, :] = (vt[r0:r0 + LANES] + ones_row).astype(BF16)


def _mla_prep(p, gq, gkv, wqt, wk, wvt, cos_b, sin_b, cos_bt, sin_bt, batch):
    t = p.shape[0]
    s_len = t // batch
    tm = TM_PREP
    nt = s_len // tm
    hw = B_HEADS * LANES
    half = B_ROPE // 2
    const = lambda b, i: (0, 0)
    tok = lambda b, i: (b * nt + i, 0)
    return pl.pallas_call(
        _mla_prep_kernel,
        grid=(batch, nt),
        in_specs=[pl.BlockSpec((tm, B_Q_RANK), lambda b, i: (b * nt + i, P_CQ // B_Q_RANK)),
                  pl.BlockSpec((tm, B_KV_RANK), lambda b, i: (b * nt + i, P_CKV // B_KV_RANK)),
                  pl.BlockSpec((tm, 2 * LANES), lambda b, i: (b * nt + i, P_KA // (2 * LANES))),
                  pl.BlockSpec((1, B_Q_RANK), const),
                  pl.BlockSpec((1, B_KV_RANK), const),
                  pl.BlockSpec((hw, B_Q_RANK), const),
                  pl.BlockSpec((B_KV_RANK, hw), const),
                  pl.BlockSpec((hw, B_KV_RANK), const),
                  pl.BlockSpec((tm, LANES), tok),
                  pl.BlockSpec((tm, LANES), tok),
                  pl.BlockSpec((half, tm), lambda b, i: (A_HEAD_DIM // 2 // half, b * nt + i)),
                  pl.BlockSpec((half, tm), lambda b, i: (A_HEAD_DIM // 2 // half, b * nt + i))],
        out_specs=[pl.BlockSpec((None, hw, tm), lambda b, i: (b, 0, i)),
                   pl.BlockSpec((None, tm, hw), lambda b, i: (b, i, 0)),
                   pl.BlockSpec((None, hw, tm), lambda b, i: (b, 0, i))],
        out_shape=[jax.ShapeDtypeStruct((batch, hw, s_len), BF16),
                   jax.ShapeDtypeStruct((batch, s_len, hw), BF16),
                   jax.ShapeDtypeStruct((batch, hw, s_len), BF16)],
        name="mla_prep",
    )(p, p, p, gq, gkv, wqt, wk, wvt, cos_b, sin_b, cos_bt, sin_bt)


def _mla_attn_kernel(qt_ref, k_ref, vt_ref, o_ref, m_sc, acc_sc, s0_sc, s1_sc, *, tk):
    n_chunks = k_ref.shape[0] // tk
    m_sc[...] = jnp.full(m_sc.shape, -jnp.inf, F32)
    acc_sc[...] = jnp.zeros(acc_sc.shape, F32)
    heads = [slice(hd * LANES, (hd + 1) * LANES) for hd in range(2)]

    def keys(j):
        return pl.ds(pl.multiple_of(j * tk, tk), tk)

    def scores(j, dst):
        for hd, sl in enumerate(heads):
            dst[hd] = jnp.dot(k_ref[keys(j), sl], qt_ref[sl, :], preferred_element_type=F32)

    def accumulate(j, src):
        for hd, sl in enumerate(heads):
            st = src[hd]
            m_old = m_sc[hd]
            m_new = jnp.maximum(m_old, jnp.max(st, axis=0, keepdims=True))
            pt = jnp.exp2(st - m_new).astype(BF16)
            acc_sc[hd] = acc_sc[hd] * jnp.exp2(m_old - m_new) + jnp.dot(
                vt_ref[sl, keys(j)], pt, preferred_element_type=F32)
            m_sc[hd] = m_new

    scores(0, s0_sc)

    def body(jj, carry):
        j = 2 * jj
        scores(j + 1, s1_sc)
        accumulate(j, s0_sc)
        scores(j + 2, s0_sc)
        accumulate(j + 1, s1_sc)
        return carry

    lax.fori_loop(0, n_chunks // 2 - 1, body, 0)
    scores(n_chunks - 1, s1_sc)
    accumulate(n_chunks - 2, s0_sc)
    accumulate(n_chunks - 1, s1_sc)
    outs = []
    for hd in range(2):
        acc = acc_sc[hd]
        outs.append(acc[:B_V] * (1.0 / acc[B_V:B_V + 1]))
    o_ref[...] = jnp.concatenate(outs, axis=0).T.astype(BF16)


def _mla_attn(qt, k, vt):
    batch, s_len, hw = k.shape
    tq = TQ_MLA
    out = pl.pallas_call(
        functools.partial(_mla_attn_kernel, tk=TK_MLA),
        grid=(batch, B_HEADS // 2, s_len // tq),
        in_specs=[pl.BlockSpec((None, 2 * LANES, tq), lambda b, p, i: (b, p, i)),
                  pl.BlockSpec((None, s_len, 2 * LANES), lambda b, p, i: (b, 0, p)),
                  pl.BlockSpec((None, 2 * LANES, s_len), lambda b, p, i: (b, p, 0))],
        out_specs=pl.BlockSpec((None, tq, LANES), lambda b, p, i: (b, i, p)),
        out_shape=jax.ShapeDtypeStruct((batch, s_len, B_HEADS * B_V), BF16),
        scratch_shapes=[pltpu.VMEM((2, 1, tq), F32), pltpu.VMEM((2, LANES, tq), F32),
                        pltpu.VMEM((2, TK_MLA, tq), F32), pltpu.VMEM((2, TK_MLA, tq), F32)],
        name="mla_attn",
    )(qt, k, vt)
    return out.reshape(batch * s_len, B_HEADS * B_V)


def _win_attn_kernel(sink_ref, qt_ref, kp_ref, kc_ref, kn_ref, vp_ref, vc_ref, vn_ref, o_ref, *, s_len):
    n = pl.program_id(1)
    grp = A_HEADS // A_KV_HEADS
    dh = A_HEAD_DIM
    tq = kc_ref.shape[0]
    kj = lax.broadcasted_iota(jnp.int32, (tq + 2 * BLK, tq), 0)
    qi = lax.broadcasted_iota(jnp.int32, (tq + 2 * BLK, tq), 1)
    k_abs = n * tq - BLK + kj
    ok = (jnp.abs(kj - BLK - qi) <= A_WINDOW) & (k_abs >= 0) & (k_abs < s_len)
    ok = jnp.concatenate([ok] * grp, axis=1)
    lane_head = lax.broadcasted_iota(jnp.int32, (1, grp * tq), 1) // tq
    kband = jnp.concatenate([kp_ref[:, :LANES], kc_ref[:, :LANES], kn_ref[:, :LANES]], axis=0)
    zeros = jnp.zeros((dh, tq), BF16)
    heads = []
    for g in range(A_KV_HEADS):
        tiles = []
        for a in range(grp):
            qh = qt_ref[(g * grp + a) * dh:(g * grp + a + 1) * dh, :]
            tiles.append(jnp.concatenate([zeros] * g + [qh] + [zeros] * (A_KV_HEADS - 1 - g), axis=0))
        qg = jnp.concatenate(tiles, axis=1)
        st = jnp.where(ok, jnp.dot(kband, qg, preferred_element_type=F32), NEG_INF)
        sk = jnp.zeros((1, grp * tq), F32)
        for a in range(grp):
            sk = jnp.where(lane_head == a, sink_ref[g * grp + a] * LOG2E, sk)
        m = jnp.maximum(jnp.max(st, axis=0, keepdims=True), sk)
        pt = jnp.exp2(st - m).astype(BF16)
        rows = slice(g * LANES, (g + 1) * LANES)
        vband = jnp.concatenate([vp_ref[rows, :], vc_ref[rows, :], vn_ref[rows, :]], axis=1)
        ot = jnp.dot(vband, pt, preferred_element_type=F32)
        o = ot[:dh] * (1.0 / (ot[dh:dh + 1] + jnp.exp2(sk - m)))
        heads.extend(o[:, a * tq:(a + 1) * tq] for a in range(grp))
    for pair in range(A_HEADS // 2):
        o_ref[:, pair * LANES:(pair + 1) * LANES] = jnp.concatenate(
            [heads[2 * pair], heads[2 * pair + 1]], axis=0).T.astype(BF16)


def _win_attn(p, qt, vt, sink, batch):
    t = p.shape[0]
    s_len = t // batch
    tq = TQ_WIN
    nq, nb, r = s_len // tq, s_len // BLK, tq // BLK
    p3 = p.reshape(batch, s_len, P_WIDTH)
    kcol = P_KA // (2 * LANES)
    lo = lambda n: jnp.maximum(n * r - 1, 0)
    hi = lambda n: jnp.minimum((n + 1) * r, nb - 1)
    kedge = lambda f: pl.BlockSpec((None, BLK, 2 * LANES), lambda b, n, s: (b, f(n), kcol))
    vedge = lambda f: pl.BlockSpec((A_KV_HEADS * LANES, BLK), lambda b, n, s: (0, b * nb + f(n)))
    out = pl.pallas_call(
        functools.partial(_win_attn_kernel, s_len=s_len),
        grid_spec=pltpu.PrefetchScalarGridSpec(
            num_scalar_prefetch=1,
            grid=(batch, nq),
            in_specs=[pl.BlockSpec((A_HEADS * A_HEAD_DIM, tq), lambda b, n, s: (0, b * nq + n)),
                      kedge(lo), pl.BlockSpec((None, tq, 2 * LANES), lambda b, n, s: (b, n, kcol)), kedge(hi),
                      vedge(lo), pl.BlockSpec((A_KV_HEADS * LANES, tq), lambda b, n, s: (0, b * nq + n)), vedge(hi)],
            out_specs=pl.BlockSpec((None, tq, A_HEADS * A_HEAD_DIM), lambda b, n, s: (b, n, 0)),
        ),
        out_shape=jax.ShapeDtypeStruct((batch, s_len, A_HEADS * A_HEAD_DIM), BF16),
        name="win_attn",
    )(sink, qt, p3, p3, p3, vt, vt, vt)
    return out.reshape(t, A_HEADS * A_HEAD_DIM)


def _mem_kv_kernel(mem_ref, g_ref, w_ref, o_ref):
    mem_n = _rms(mem_ref[...], g_ref[...]).astype(BF16)
    o_ref[...] = jnp.dot(mem_n, w_ref[...], preferred_element_type=F32).astype(BF16)


def _mem_kv(mem2, g_mem, w_mem_kv):
    depth, d, width = w_mem_kv.shape
    rows = mem2.shape[0]
    return pl.pallas_call(
        _mem_kv_kernel,
        grid=(depth,),
        in_specs=[pl.BlockSpec((rows, d), lambda l: (0, 0)),
                  pl.BlockSpec((1, d), lambda l: (0, 0)),
                  pl.BlockSpec((None, d, width), lambda l: (l, 0, 0))],
        out_specs=pl.BlockSpec((None, rows, width), lambda l: (l, 0, 0)),
        out_shape=jax.ShapeDtypeStruct((depth, rows, width), BF16),
        name="mem_kv",
    )(mem2, g_mem, w_mem_kv)


def _mem_attn_kernel(q_ref, kv_ref, o_ref):
    dh = M_HEAD_DIM
    for hd in range(M_HEADS):
        q = q_ref[:, hd * dh:(hd + 1) * dh]
        k = kv_ref[:, hd * dh:(hd + 1) * dh]
        v = kv_ref[:, (M_HEADS + hd) * dh:(M_HEADS + hd + 1) * dh]
        sc = lax.dot_general(q, k, (((1,), (1,)), ((), ())), preferred_element_type=F32)
        e = jnp.exp(sc - jnp.max(sc, axis=-1, keepdims=True))
        pr = (e * (1.0 / jnp.sum(e, axis=-1, keepdims=True))).astype(BF16)
        o_ref[:, hd * dh:(hd + 1) * dh] = jnp.dot(pr, v, preferred_element_type=F32).astype(BF16)


def _mem_attn(p, mem_kv, layer, batch):
    t = p.shape[0]
    s_len = t // batch
    tm = TM_MEM
    n_mem = mem_kv.shape[1] // batch
    width = M_HEADS * M_HEAD_DIM
    p3 = p.reshape(batch, s_len, P_WIDTH)
    out = pl.pallas_call(
        _mem_attn_kernel,
        grid=(batch, s_len // tm),
        in_specs=[pl.BlockSpec((None, tm, width), lambda b, i: (b, i, P_QM // width)),
                  pl.BlockSpec((None, n_mem, 2 * width), lambda b, i: (layer, b, 0))],
        out_specs=pl.BlockSpec((None, tm, width), lambda b, i: (b, i, 0)),
        out_shape=jax.ShapeDtypeStruct((batch, s_len, width), BF16),
        name="mem_attn",
    )(p3, mem_kv)
    return out.reshape(t, width)


def _mix_kernel(x_ref, ya_ref, yb_ref, ym_ref, g_ref, wb_ref, wo_ref, o_ref):
    d = x_ref.shape[1]
    mixed = None
    for i, y_ref in enumerate((ya_ref, yb_ref, ym_ref)):
        gate = jax.nn.sigmoid(g_ref[:, i * d:(i + 1) * d].astype(F32))
        term = gate * jnp.dot(y_ref[...], wb_ref[i], preferred_element_type=F32)
        mixed = term if mixed is None else mixed + term
    o_ref[...] = x_ref[...] + jnp.dot(mixed.astype(BF16), wo_ref[...], preferred_element_type=F32)


def _mix(x, ya, yb, ym, p, wb, wo, layer):
    t, d = x.shape
    tm = TM_MIX
    bw = ya.shape[1]
    row = lambda i: (i, 0)
    return pl.pallas_call(
        _mix_kernel,
        grid=(t // tm,),
        in_specs=[pl.BlockSpec((tm, d), row),
                  pl.BlockSpec((tm, bw), row), pl.BlockSpec((tm, bw), row), pl.BlockSpec((tm, bw), row),
                  pl.BlockSpec((tm, 3 * d), lambda i: (i, P_GATES)),
                  pl.BlockSpec((None, 3, bw, d), lambda i: (layer, 0, 0, 0)),
                  pl.BlockSpec((None, d, d), lambda i: (layer, 0, 0))],
        out_specs=pl.BlockSpec((tm, d), row),
        out_shape=jax.ShapeDtypeStruct((t, d), F32),
        name="mix_out",
    )(x, ya, yb, ym, p, wb, wo)


def _swiglu_step(h, wg, wu, wd):
    gate = jnp.dot(h, wg, preferred_element_type=F32)
    up = jnp.dot(h, wu, preferred_element_type=F32)
    act = (gate * jax.nn.sigmoid(gate) * up).astype(BF16)
    return jnp.dot(act, wd, preferred_element_type=F32)


def _ffn_kernel(x_ref, g_ref, wg_ref, wu_ref, wd_ref, o_ref, h_sc, acc_sc):
    j = pl.program_id(1)

    @pl.when(j == 0)
    def _():
        h_sc[...] = _rms(x_ref[...], g_ref[...]).astype(BF16)
        acc_sc[...] = jnp.zeros(acc_sc.shape, F32)

    acc_sc[...] += _swiglu_step(h_sc[...], wg_ref[...], wu_ref[...], wd_ref[...])

    @pl.when(j == pl.num_programs(1) - 1)
    def _():
        o_ref[...] = x_ref[...] + acc_sc[...]


def _ffn(x, g, wg, wu, wd, layer):
    t, d = x.shape
    f = wg.shape[2]
    tm, tf = TM_FFN, TF_FFN
    return pl.pallas_call(
        _ffn_kernel,
        grid=(t // tm, f // tf),
        in_specs=[pl.BlockSpec((tm, d), lambda i, j: (i, 0)),
                  pl.BlockSpec((1, d), lambda i, j: (0, 0)),
                  pl.BlockSpec((None, d, tf), lambda i, j: (layer, 0, j)),
                  pl.BlockSpec((None, d, tf), lambda i, j: (layer, 0, j)),
                  pl.BlockSpec((None, tf, d), lambda i, j: (layer, j, 0))],
        out_specs=pl.BlockSpec((tm, d), lambda i, j: (i, 0)),
        out_shape=jax.ShapeDtypeStruct((t, d), F32),
        scratch_shapes=[pltpu.VMEM((tm, d), BF16), pltpu.VMEM((tm, d), F32)],
        name="ffn_dense",
    )(x, g, wg, wu, wd)


def _router_kernel(x_ref, g_ref, wr_ref, idx_ref, wgt_ref, xt_ref):
    x = x_ref[...]
    xt_ref[...] = x.reshape(xt_ref.shape)
    h = _rms(x, g_ref[...])
    logits = lax.dot_general(wr_ref[...], h, (((1,), (1,)), ((), ())),
                             precision=lax.Precision.HIGHEST, preferred_element_type=F32)
    e_id = lax.broadcasted_iota(jnp.int32, logits.shape, 0)
    m1 = jnp.max(logits, axis=0, keepdims=True)
    i1 = jnp.min(jnp.where(logits == m1, e_id, N_EXPERTS), axis=0, keepdims=True)
    rest = jnp.where(e_id == i1, -jnp.inf, logits)
    m2 = jnp.max(rest, axis=0, keepdims=True)
    i2 = jnp.min(jnp.where(rest == m2, e_id, N_EXPERTS), axis=0, keepdims=True)
    e2 = jnp.exp(m2 - m1)
    w1 = 1.0 / (1.0 + e2)
    idx_ref[...] = jnp.concatenate([i1, i2], axis=0)
    wgt_ref[...] = jnp.concatenate([w1, e2 * w1], axis=0)


def _router(x, g, wr_t):
    t, d = x.shape
    tm = TM_ROUTE
    return pl.pallas_call(
        _router_kernel,
        grid=(t // tm,),
        in_specs=[pl.BlockSpec((tm, d), lambda i: (i, 0)),
                  pl.BlockSpec((1, d), lambda i: (0, 0)),
                  pl.BlockSpec((N_EXPERTS, d), lambda i: (0, 0))],
        out_specs=[pl.BlockSpec((TOP_K, tm), lambda i: (0, i)),
                   pl.BlockSpec((TOP_K, tm), lambda i: (0, i)),
                   pl.BlockSpec((tm * SUBLANES, LANES), lambda i: (i, 0))],
        out_shape=[jax.ShapeDtypeStruct((TOP_K, t), jnp.int32),
                   jax.ShapeDtypeStruct((TOP_K, t), F32),
                   jax.ShapeDtypeStruct((t * SUBLANES, LANES), F32)],
        name="moe_router",
    )(x, g, wr_t)


def _moe_rows_per_step(n_steps):
    return -(-TM_EXP // (n_steps * SUBLANES)) * SUBLANES


def _slot_tables(top_i, n_tok, buf_rows):
    n_asg = n_tok * TOP_K
    flat_e = top_i.T.reshape(n_asg)
    order = jnp.argsort(flat_e, stable=True).astype(jnp.int32)
    counts = jnp.sum(flat_e[:, None] == jnp.arange(N_EXPERTS, dtype=jnp.int32)[None, :], axis=0, dtype=jnp.int32)
    padded = (counts + TM_EXP - 1) // TM_EXP * TM_EXP
    pad_end = jnp.cumsum(padded)
    start = jnp.cumsum(counts) - counts
    pads_before = jnp.cumsum(padded - counts) - (padded - counts)
    n_slots = n_asg + N_EXPERTS * TM_EXP
    n_blk = n_slots // TM_EXP
    extra = buf_rows - TM_EXP
    blk_e = jnp.minimum(jnp.searchsorted(pad_end, jnp.arange(n_blk, dtype=jnp.int32) * TM_EXP, side='right'),
                        N_EXPERTS - 1).astype(jnp.int32)
    slot_e = jnp.repeat(blk_e, TM_EXP)
    rank = jnp.arange(n_slots, dtype=jnp.int32) - (pad_end - padded)[slot_e]
    is_pad = rank >= counts[slot_e]
    asg = order[jnp.clip(start[slot_e] + rank, 0, n_asg - 1)]
    slot_tok = jnp.where(is_pad, 0, asg // TOP_K)
    slot_row = jnp.where(is_pad, n_asg + pads_before[slot_e] + rank - counts[slot_e],
                         (asg % TOP_K) * n_tok + slot_tok)
    gather_tok = jnp.concatenate([slot_tok.reshape(n_blk, TM_EXP), jnp.zeros((n_blk, extra), jnp.int32)], axis=1)
    spare = n_slots + jnp.arange((n_blk + 1) * buf_rows - n_slots, dtype=jnp.int32)
    lead, over = spare[:TM_EXP], spare[TM_EXP:].reshape(n_blk + 1, extra)
    scatter_row = jnp.concatenate(
        [jnp.concatenate([lead[None], slot_row.reshape(n_blk, TM_EXP)], axis=0), over], axis=1)
    n_live = (pad_end[-1] // TM_EXP).astype(jnp.int32)
    blk_meta = jnp.concatenate([blk_e, n_live[None]])
    return blk_meta, gather_tok.reshape(-1).astype(jnp.int32), scatter_row.reshape(-1).astype(jnp.int32)


def _moe_ffn_kernel(blk_e_ref, gtok_ref, srow_ref, x_hbm, g_ref, wg_ref, wu_ref, wd_ref, o_hbm,
                    xg_sc, ys_sc, h_sc, acc_sc, gsem, ssem, *, rows_per_step):
    i, j = pl.program_id(0), pl.program_id(1)
    n_blk, n_steps = pl.num_programs(0), pl.num_programs(1)
    rows, d = h_sc.shape
    tr = d // LANES
    buf_rows = xg_sc.shape[1] // tr
    slot = i % 2
    other = 1 - slot

    def tile(ref, k):
        return ref.at[pl.ds(pl.multiple_of(k * tr, tr), tr), :]

    def gathered(s):
        return pltpu.make_async_copy(x_hbm.at[pl.ds(0, buf_rows * tr), :], xg_sc.at[s], gsem.at[s])

    def scattered(s):
        return pltpu.make_async_copy(ys_sc.at[s], o_hbm.at[pl.ds(0, buf_rows * tr), :], ssem.at[s])

    def get_row(s, k, entry):
        return pltpu.make_async_copy(tile(x_hbm, gtok_ref[entry]), tile(xg_sc.at[s], k), gsem.at[s])

    def put_row(s, k, entry):
        return pltpu.make_async_copy(tile(ys_sc.at[s], k), tile(o_hbm, srow_ref[entry]), ssem.at[s])

    @pl.when((i == 0) & (j == 0))
    def _():
        ys_sc[...] = jnp.zeros(ys_sc.shape, F32)

        def start(k, c):
            get_row(0, k, k).start()
            return c

        lax.fori_loop(0, buf_rows, start, 0)

    @pl.when(j == 0)
    def _():
        gathered(slot).wait()
        for s in range(2):
            @pl.when(slot == s)
            def _():
                xs = [xg_sc[s, pl.ds(c, rows, stride=tr), :] for c in range(tr)]
                ms = sum(jnp.sum(x * x, axis=-1, keepdims=True) for x in xs) * (1.0 / d)
                inv = lax.rsqrt(ms + EPS)
                for c, x in enumerate(xs):
                    cols = slice(c * LANES, (c + 1) * LANES)
                    h_sc[:, cols] = (x * inv * g_ref[:, cols]).astype(BF16)

        acc_sc[...] = jnp.zeros(acc_sc.shape, F32)

    nxt = jnp.minimum(i + 1, n_blk - 1)

    def issue_row_copies():
        for r in range(rows_per_step):
            k = j * rows_per_step + r
            get_row(other, k, nxt * buf_rows + k).start(priority=r % DMA_QUEUES)
            put_row(other, k, i * buf_rows + k).start(priority=(r + 1) % DMA_QUEUES)

    live = i < blk_e_ref[n_blk]

    @pl.when(live)
    def _():
        issue_row_copies()
        acc_sc[...] += _swiglu_step(h_sc[...], wg_ref[...], wu_ref[...], wd_ref[...])

    @pl.when(jnp.logical_not(live))
    def _():
        issue_row_copies()

    @pl.when(j == n_steps - 1)
    def _():
        @pl.when(i > 0)
        def _():
            scattered(slot).wait()

        for s in range(2):
            @pl.when(slot == s)
            def _():
                for c in range(tr):
                    ys_sc[s, pl.ds(c, rows, stride=tr), :] = acc_sc[:, c * LANES:(c + 1) * LANES]

    @pl.when((i == n_blk - 1) & (j == n_steps - 1))
    def _():
        def start(k, c):
            put_row(slot, k, n_blk * buf_rows + k).start()
            return c

        lax.fori_loop(0, buf_rows, start, 0)
        scattered(other).wait()
        scattered(slot).wait()
        gathered(other).wait()


def _moe_ffn(x_tiles, g, wg, wu, wd, layer, blk_e, gather_tok, scatter_row):
    d, f = wg.shape[2], wg.shape[3]
    tr = d // LANES
    assert tr % SUBLANES == 0
    tm, tf = TM_EXP, TF_FFN
    n_steps = f // tf
    rows_per_step = _moe_rows_per_step(n_steps)
    buf_rows = rows_per_step * n_steps
    n_blk = gather_tok.shape[0] // buf_rows

    def ff(i, j, be):
        return jnp.where(i < be[n_blk], j, n_steps - 1)

    return pl.pallas_call(
        functools.partial(_moe_ffn_kernel, rows_per_step=rows_per_step),
        grid_spec=pltpu.PrefetchScalarGridSpec(
            num_scalar_prefetch=3,
            grid=(n_blk, n_steps),
            in_specs=[pl.BlockSpec(memory_space=pl.ANY),
                      pl.BlockSpec((1, d), lambda i, j, be, gt, sr: (0, 0)),
                      pl.BlockSpec((None, None, d, tf), lambda i, j, be, gt, sr: (layer, be[i], 0, ff(i, j, be))),
                      pl.BlockSpec((None, None, d, tf), lambda i, j, be, gt, sr: (layer, be[i], 0, ff(i, j, be))),
                      pl.BlockSpec((None, None, tf, d), lambda i, j, be, gt, sr: (layer, be[i], ff(i, j, be), 0))],
            out_specs=pl.BlockSpec(memory_space=pl.ANY),
            scratch_shapes=[pltpu.VMEM((2, buf_rows * tr, LANES), F32), pltpu.VMEM((2, buf_rows * tr, LANES), F32),
                            pltpu.VMEM((tm, d), BF16), pltpu.VMEM((tm, d), F32),
                            pltpu.SemaphoreType.DMA((2,)), pltpu.SemaphoreType.DMA((2,))],
        ),
        out_shape=jax.ShapeDtypeStruct((scatter_row.shape[0] * tr, LANES), F32),
        name="moe_ffn",
    )(blk_e, gather_tok, scatter_row, x_tiles, g, wg, wu, wd)


def _moe_combine_kernel(x_ref, w_ref, y0_ref, y1_ref, gf_ref, o_ref, *, final_norm):
    tm, d = x_ref.shape
    y0, y1 = y0_ref[...].reshape(tm, d), y1_ref[...].reshape(tm, d)
    out = x_ref[...] + w_ref[:, 0:1] * y0 + w_ref[:, 1:2] * y1
    if final_norm:
        out = _rms(out, gf_ref[...])
    o_ref[...] = out


def _moe_combine(x, top_w, y, g_final, final_norm):
    t, d = x.shape
    tr = d // LANES
    tm = TM_COMB
    nt = t // tm
    return pl.pallas_call(
        functools.partial(_moe_combine_kernel, final_norm=final_norm),
        grid=(nt,),
        in_specs=[pl.BlockSpec((tm, d), lambda i: (i, 0)),
                  pl.BlockSpec((tm, TOP_K), lambda i: (i, 0)),
                  pl.BlockSpec((tm * tr, LANES), lambda i: (i, 0)),
                  pl.BlockSpec((tm * tr, LANES), lambda i: (nt + i, 0)),
                  pl.BlockSpec((1, d), lambda i: (0, 0))],
        out_specs=pl.BlockSpec((tm, d), lambda i: (i, 0)),
        out_shape=jax.ShapeDtypeStruct((t, d), F32),
        name="moe_combine",
    )(x, top_w, y, y, g_final)


def _final_norm_kernel(x_ref, g_ref, o_ref):
    o_ref[...] = _rms(x_ref[...], g_ref[...])


def _final_norm(x, g):
    t, d = x.shape
    tm = 1024
    return pl.pallas_call(
        _final_norm_kernel,
        grid=(t // tm,),
        in_specs=[pl.BlockSpec((tm, d), lambda i: (i, 0)), pl.BlockSpec((1, d), lambda i: (0, 0))],
        out_specs=pl.BlockSpec((tm, d), lambda i: (i, 0)),
        out_shape=jax.ShapeDtypeStruct((t, d), F32),
        name="final_norm",
    )(x, g)


def _w_in_relayout_kernel(w_ref, tok_ref, qt_ref, vt_ref):
    w = w_ref[...]
    rows = w.shape[0]
    qa, ka, va = w[:, 0:512], w[:, 512:640], w[:, 640:768]
    cq, ckv, kr = w[:, 768:1024], w[:, 1024:1152], w[:, 1152:1184]
    qm, gates = w[:, 1184:1696], w[:, 1696:4768]
    pad = jnp.zeros((rows, LANES - B_ROPE), w.dtype)
    tok_ref[...] = jnp.concatenate([gates, qm, ka, kr, pad, cq, ckv], axis=1).astype(BF16)
    qt_ref[...] = qa.T.astype(BF16)
    vt_ref[...] = va.T.astype(BF16)


def _relayout_w_in(w_in):
    depth, d, width = w_in.shape
    tr = 256
    qrows, vrows = A_HEADS * A_HEAD_DIM, A_KV_HEADS * A_HEAD_DIM
    return pl.pallas_call(
        _w_in_relayout_kernel,
        grid=(depth, d // tr),
        in_specs=[pl.BlockSpec((None, tr, width), lambda l, i: (l, i, 0))],
        out_specs=[pl.BlockSpec((None, tr, P_WIDTH), lambda l, i: (l, i, 0)),
                   pl.BlockSpec((None, qrows, tr), lambda l, i: (l, 0, i)),
                   pl.BlockSpec((None, vrows, tr), lambda l, i: (l, 0, i))],
        out_shape=[jax.ShapeDtypeStruct((depth, d, P_WIDTH), BF16),
                   jax.ShapeDtypeStruct((depth, qrows, d), BF16),
                   jax.ShapeDtypeStruct((depth, vrows, d), BF16)],
        name="w_in_relayout",
    )(w_in)


def _relayout_latent(w_uq, w_ukv):
    depth = w_uq.shape[0]
    q = w_uq.reshape(depth, B_Q_RANK, B_HEADS, B_NOPE + B_ROPE)
    q = jnp.pad(q, ((0, 0), (0, 0), (0, 0), (0, LANES - B_NOPE - B_ROPE)))
    kv = w_ukv.reshape(depth, B_KV_RANK, B_HEADS, B_NOPE + B_V)
    k = jnp.pad(kv[..., :B_NOPE], ((0, 0), (0, 0), (0, 0), (0, LANES - B_NOPE)))
    v = jnp.pad(kv[..., B_NOPE:], ((0, 0), (0, 0), (0, 0), (0, LANES - B_V)))
    flat = lambda a, r: a.reshape(depth, r, B_HEADS * LANES).astype(BF16)
    return (jnp.swapaxes(flat(q, B_Q_RANK), 1, 2), flat(k, B_KV_RANK),
            jnp.swapaxes(flat(v, B_KV_RANK), 1, 2))


def kernel(x, mem, positions, g_mix, w_in, g_q_lat, w_uq, g_kv_lat, w_ukv, sink, w_mem_kv, w_branch, w_out,
           g_ffn, w_gate_dense, w_up_dense, w_down_dense, w_router, w_gate_exp, w_up_exp, w_down_exp,
           g_mem, g_final):
    batch, s_len, d = x.shape
    depth = w_in.shape[0]
    t = batch * s_len
    row = lambda a: a.reshape(1, -1).astype(F32)

    w_in_p, wqa_t, wva_t = _relayout_w_in(w_in)
    wq_p, wk_p, wv_p = _relayout_latent(w_uq, w_ukv)
    bf = lambda w: w.astype(BF16)
    w_branch_b, w_out_b = bf(w_branch), bf(w_out)
    wgd, wud, wdd = bf(w_gate_dense), bf(w_up_dense), bf(w_down_dense)
    wge, wue, wde = bf(w_gate_exp), bf(w_up_exp), bf(w_down_exp)
    wr_t = jnp.swapaxes(w_router, 1, 2).astype(F32)

    cos_a, sin_a, cos_b, sin_b, cos_t, sin_t = _rope_tables(positions)
    mem_kv = _mem_kv(mem.reshape(batch * mem.shape[1], d).astype(F32), row(g_mem), bf(w_mem_kv))

    xt = x.reshape(t, d).astype(F32)
    for l in range(depth):
        p, qa_t, va_t = _in_proj(xt, row(g_mix[l]), w_in_p, wqa_t, wva_t, l, cos_a, sin_a, cos_t, sin_t)
        y_a = _win_attn(p, qa_t, va_t, sink[l].astype(F32), batch)
        qt, k, vt = _mla_prep(p, row(g_q_lat[l]), row(g_kv_lat[l]), wq_p[l], wk_p[l], wv_p[l],
                              cos_b, sin_b, cos_t, sin_t, batch)
        y_b = _mla_attn(qt, k, vt)
        y_m = _mem_attn(p, mem_kv, l, batch)
        xt = _mix(xt, y_a, y_b, y_m, p, w_branch_b, w_out_b, l)
        i = l // 2
        if l % 2 == 0:
            xt = _ffn(xt, row(g_ffn[l]), wgd, wud, wdd, i)
            if l == depth - 1:
                xt = _final_norm(xt, row(g_final))
        else:
            top_i, top_w, x_tiles = _router(xt, row(g_ffn[l]), wr_t[i])
            n_steps = w_gate_exp.shape[3] // TF_FFN
            buf_rows = _moe_rows_per_step(n_steps) * n_steps
            blk_e, gather_tok, scatter_row = _slot_tables(top_i, t, buf_rows)
            y = _moe_ffn(x_tiles, row(g_ffn[l]), wge, wue, wde, i, blk_e, gather_tok, scatter_row)
            xt = _moe_combine(xt, top_w.T, y, row(g_final), final_norm=(l == depth - 1))
    return xt.reshape(batch, s_len, d).astype(x.dtype)
```

```python
import functools

import jax
import jax.numpy as jnp
from jax import lax
from jax.experimental import pallas as pl
from jax.experimental.pallas import tpu as pltpu

F32 = jnp.float32
BF16 = jnp.bfloat16

ROPE_THETA = 10000.0
EPS = 1e-6
NEG_INF = -1e30
LOG2E = 1.4426950408889634

A_HEADS, A_KV_HEADS, A_HEAD_DIM, A_WINDOW = 8, 2, 64, 128
B_HEADS, B_Q_RANK, B_KV_RANK, B_NOPE, B_ROPE, B_V = 8, 256, 128, 64, 32, 64
M_HEADS, M_HEAD_DIM = 4, 128
N_EXPERTS, TOP_K = 8, 2
LANES = 128
SUBLANES = 8
BLK = 128

P_GATES = 0
P_QM = 3072
P_KA = 3584
P_CQ = 3840
P_CKV = 4096
P_WIDTH = 4224

TM_PROJ = 512
TM_PREP = 512
TM_MIX = 512
TM_MEM = 512
TM_FFN = 1024
TF_FFN = 512
TQ_WIN = 256
TQ_MLA = 2048
TK_MLA = 512
TM_ROUTE = 1024
TM_EXP = 1024
TM_COMB = 512
DMA_QUEUES = 2


def _rms(x, g):
    return x * lax.rsqrt(jnp.mean(x * x, axis=-1, keepdims=True) + EPS) * g


def _rot_tile(x, cos, sin_signed, first, shift):
    up = pltpu.roll(x, LANES - shift, 1)
    dn = pltpu.roll(x, shift, 1)
    return x * cos + jnp.where(first, up, dn) * sin_signed


def _rope_table_kernel(posr_ref, par_ref, invc_ref, ca_ref, sa_ref, cb_ref, sb_ref, cbt_ref, sbt_ref):
    ang_t = invc_ref[...] * posr_ref[...]
    ct, st = jnp.cos(ang_t), jnp.sin(ang_t)
    cbt_ref[...] = ct
    sbt_ref[...] = st
    half_a, half_b = A_HEAD_DIM // 2, B_ROPE // 2
    tm = ct.shape[1]
    ca, sa = ct[:half_a].T, st[:half_a].T
    ca_ref[...] = jnp.concatenate([ca] * (LANES // half_a), axis=1)
    sa_ref[...] = jnp.concatenate([sa] * (LANES // half_a), axis=1) * par_ref[1:2, :]
    cb, sb = ct[half_a:].T, st[half_a:].T
    one = jnp.ones((tm, B_NOPE), F32)
    pad = jnp.ones((tm, LANES - B_NOPE - B_ROPE), F32)
    cb_ref[...] = jnp.concatenate([one, cb, cb, pad], axis=1)
    sb_ref[...] = jnp.concatenate([one, sb, sb, pad], axis=1) * par_ref[3:4, :]


def _rope_tables(positions):
    t = positions.size
    tm = 2048
    lane = jnp.arange(LANES)
    half_a = A_HEAD_DIM // 2
    inv_a = jnp.power(F32(ROPE_THETA), -jnp.arange(half_a, dtype=F32) * (2.0 / A_HEAD_DIM))
    sgn_a = jnp.where((lane % A_HEAD_DIM) < half_a, -1.0, 1.0).astype(F32)
    half_b = B_ROPE // 2
    inv_b = jnp.power(F32(ROPE_THETA), -jnp.arange(half_b, dtype=F32) * (2.0 / B_ROPE))
    in_rope = (lane >= B_NOPE) & (lane < B_NOPE + B_ROPE)
    sgn_b = jnp.where(in_rope, jnp.where(lane < B_NOPE + half_b, -1.0, 1.0), 0.0).astype(F32)
    par = jnp.zeros((8, LANES), F32).at[1].set(sgn_a).at[3].set(sgn_b)
    tab = jax.ShapeDtypeStruct((t, LANES), F32)
    n_freq = half_a + half_b
    inv_col = jnp.concatenate([inv_a, inv_b]).reshape(n_freq, 1)
    tab_t = jax.ShapeDtypeStruct((n_freq, t), F32)
    return pl.pallas_call(
        _rope_table_kernel,
        grid=(t // tm,),
        in_specs=[pl.BlockSpec((1, tm), lambda i: (0, i)),
                  pl.BlockSpec((8, LANES), lambda i: (0, 0)),
                  pl.BlockSpec((n_freq, 1), lambda i: (0, 0))],
        out_specs=[pl.BlockSpec((tm, LANES), lambda i: (i, 0))] * 4
        + [pl.BlockSpec((n_freq, tm), lambda i: (0, i))] * 2,
        out_shape=[tab] * 4 + [tab_t] * 2,
        name="rope_tables",
    )(positions.reshape(1, t).astype(F32), par, inv_col)


def _in_proj_kernel(x_ref, g_ref, w_ref, wqt_ref, wvt_ref, cos_ref, sin_ref, cost_ref, sint_ref,
                    o_ref, qt_ref, vt_ref):
    h = _rms(x_ref[...], g_ref[...]).astype(BF16)
    half = A_HEAD_DIM // 2

    def proj(c0, c1):
        return jnp.dot(h, w_ref[:, c0:c1], preferred_element_type=F32)

    for c0 in range(P_GATES, P_QM, 512):
        o_ref[:, c0:c0 + 512] = proj(c0, c0 + 512).astype(BF16)
    o_ref[:, P_QM:P_KA] = (proj(P_QM, P_KA) * (M_HEAD_DIM ** -0.5)).astype(BF16)
    cos, sin = cos_ref[...], sin_ref[...]
    lane = lax.broadcasted_iota(jnp.int32, cos.shape, 1)
    ka = proj(P_KA, P_CQ)
    o_ref[:, P_KA:P_KA + LANES] = _rot_tile(ka[:, :LANES], cos, sin, (lane % A_HEAD_DIM) < half, half).astype(BF16)
    o_ref[:, P_KA + LANES:P_CQ] = ka[:, LANES:].astype(BF16)
    o_ref[:, P_CQ:P_WIDTH] = proj(P_CQ, P_WIDTH).astype(BF16)

    nt = (((1,), (1,)), ((), ()))
    ct, st = cost_ref[...], sint_ref[...]
    qscale = A_HEAD_DIM ** -0.5 * LOG2E
    qt = lax.dot_general(wqt_ref[...], h, nt, preferred_element_type=F32)
    for hd in range(A_HEADS):
        r0 = hd * A_HEAD_DIM
        x1, x2 = qt[r0:r0 + half], qt[r0 + half:r0 + A_HEAD_DIM]
        qt_ref[r0:r0 + half, :] = ((x1 * ct - x2 * st) * qscale).astype(BF16)
        qt_ref[r0 + half:r0 + A_HEAD_DIM, :] = ((x2 * ct + x1 * st) * qscale).astype(BF16)
    vt = lax.dot_general(wvt_ref[...], h, nt, preferred_element_type=F32)
    row = lax.broadcasted_iota(jnp.int32, (LANES - A_HEAD_DIM, vt.shape[1]), 0)
    ones_pad = (row == 0).astype(BF16)
    for g in range(A_KV_HEADS):
        vt_ref[g * LANES:g * LANES + A_HEAD_DIM, :] = vt[g * A_HEAD_DIM:(g + 1) * A_HEAD_DIM].astype(BF16)
        vt_ref[g * LANES + A_HEAD_DIM:(g + 1) * LANES, :] = ones_pad


def _in_proj(x, g, w, wqt, wvt, layer, cos_a, sin_a, cos_t, sin_t):
    t, d = x.shape
    tm = TM_PROJ
    half = A_HEAD_DIM // 2
    qrows, vrows = A_HEADS * A_HEAD_DIM, A_KV_HEADS * LANES
    const = lambda i: (0, 0)
    this_layer = lambda i: (layer, 0, 0)
    return pl.pallas_call(
        _in_proj_kernel,
        grid=(t // tm,),
        in_specs=[pl.BlockSpec((tm, d), lambda i: (i, 0)),
                  pl.BlockSpec((1, d), const),
                  pl.BlockSpec((None, d, P_WIDTH), this_layer),
                  pl.BlockSpec((None, qrows, d), this_layer),
                  pl.BlockSpec((None, A_KV_HEADS * A_HEAD_DIM, d), this_layer),
                  pl.BlockSpec((tm, LANES), lambda i: (i, 0)),
                  pl.BlockSpec((tm, LANES), lambda i: (i, 0)),
                  pl.BlockSpec((half, tm), lambda i: (0, i)),
                  pl.BlockSpec((half, tm), lambda i: (0, i))],
        out_specs=[pl.BlockSpec((tm, P_WIDTH), lambda i: (i, 0)),
                   pl.BlockSpec((qrows, tm), lambda i: (0, i)),
                   pl.BlockSpec((vrows, tm), lambda i: (0, i))],
        out_shape=[jax.ShapeDtypeStruct((t, P_WIDTH), BF16),
                   jax.ShapeDtypeStruct((qrows, t), BF16),
                   jax.ShapeDtypeStruct((vrows, t), BF16)],
        name="in_proj",
    )(x, g, w, wqt, wvt, cos_a, sin_a, cos_t, sin_t)


def _mla_prep_kernel(cq_ref, ckv_ref, kr_ref, gq_ref, gkv_ref, wqt_ref, wk_ref, wvt_ref,
                     cos_ref, sin_ref, cost_ref, sint_ref, qt_ref, k_ref, vt_ref):
    half = B_ROPE // 2
    nt = (((1,), (1,)), ((), ()))
    cq = _rms(cq_ref[...].astype(F32), gq_ref[...]).astype(BF16)
    ckv = _rms(ckv_ref[...].astype(F32), gkv_ref[...]).astype(BF16)

    cos, sin = cos_ref[...], sin_ref[...]
    lane = lax.broadcasted_iota(jnp.int32, cos.shape, 1)
    first = (lane >= B_NOPE) & (lane < B_NOPE + half)
    kr = _rot_tile(pltpu.roll(kr_ref[:, LANES:].astype(F32), B_NOPE, 1), cos, sin, first, half)
    k = jnp.dot(ckv, wk_ref[...], preferred_element_type=F32)
    for hd in range(B_HEADS):
        sl = slice(hd * LANES, (hd + 1) * LANES)
        k_ref[:, sl] = (k[:, sl] + kr).astype(BF16)

    ct, st = cost_ref[...], sint_ref[...]
    qscale = (B_NOPE + B_ROPE) ** -0.5 * LOG2E
    qt = lax.dot_general(wqt_ref[...], cq, nt, preferred_element_type=F32)
    vt = lax.dot_general(wvt_ref[...], ckv, nt, preferred_element_type=F32)
    row = lax.broadcasted_iota(jnp.int32, (LANES, vt.shape[1]), 0)
    ones_row = (row == B_V).astype(F32)
    for hd in range(B_HEADS):
        r0 = hd * LANES
        x1 = qt[r0 + B_NOPE:r0 + B_NOPE + half]
        x2 = qt[r0 + B_NOPE + half:r0 + B_NOPE + B_ROPE]
        qh = jnp.concatenate([qt[r0:r0 + B_NOPE], x1 * ct - x2 * st, x2 * ct + x1 * st,
                              qt[r0 + B_NOPE + B_ROPE:r0 + LANES]], axis=0)
        qt_ref[r0:r0 + LANES, :] = (qh * qscale).astype(BF16)
        vt_ref[r0:r0 + LANES, :] = (vt[r0:r0 + LANES] + ones_row).astype(BF16)


def _mla_prep(p, gq, gkv, wqt, wk, wvt, cos_b, sin_b, cos_bt, sin_bt, batch):
    t = p.shape[0]
    s_len = t // batch
    tm = TM_PREP
    nt = s_len // tm
    hw = B_HEADS * LANES
    half = B_ROPE // 2
    const = lambda b, i: (0, 0)
    tok = lambda b, i: (b * nt + i, 0)
    return pl.pallas_call(
        _mla_prep_kernel,
        grid=(batch, nt),
        in_specs=[pl.BlockSpec((tm, B_Q_RANK), lambda b, i: (b * nt + i, P_CQ // B_Q_RANK)),
                  pl.BlockSpec((tm, B_KV_RANK), lambda b, i: (b * nt + i, P_CKV // B_KV_RANK)),
                  pl.BlockSpec((tm, 2 * LANES), lambda b, i: (b * nt + i, P_KA // (2 * LANES))),
                  pl.BlockSpec((1, B_Q_RANK), const),
                  pl.BlockSpec((1, B_KV_RANK), const),
                  pl.BlockSpec((hw, B_Q_RANK), const),
                  pl.BlockSpec((B_KV_RANK, hw), const),
                  pl.BlockSpec((hw, B_KV_RANK), const),
                  pl.BlockSpec((tm, LANES), tok),
                  pl.BlockSpec((tm, LANES), tok),
                  pl.BlockSpec((half, tm), lambda b, i: (A_HEAD_DIM // 2 // half, b * nt + i)),
                  pl.BlockSpec((half, tm), lambda b, i: (A_HEAD_DIM // 2 // half, b * nt + i))],
        out_specs=[pl.BlockSpec((None, hw, tm), lambda b, i: (b, 0, i)),
                   pl.BlockSpec((None, tm, hw), lambda b, i: (b, i, 0)),
                   pl.BlockSpec((None, hw, tm), lambda b, i: (b, 0, i))],
        out_shape=[jax.ShapeDtypeStruct((batch, hw, s_len), BF16),
                   jax.ShapeDtypeStruct((batch, s_len, hw), BF16),
                   jax.ShapeDtypeStruct((batch, hw, s_len), BF16)],
        name="mla_prep",
    )(p, p, p, gq, gkv, wqt, wk, wvt, cos_b, sin_b, cos_bt, sin_bt)


def _mla_attn_kernel(qt_ref, k_ref, vt_ref, o_ref, m_sc, acc_sc, s0_sc, s1_sc, *, tk):
    n_chunks = k_ref.shape[0] // tk
    m_sc[...] = jnp.full(m_sc.shape, -jnp.inf, F32)
    acc_sc[...] = jnp.zeros(acc_sc.shape, F32)
    heads = [slice(hd * LANES, (hd + 1) * LANES) for hd in range(2)]

    def keys(j):
        return pl.ds(pl.multiple_of(j * tk, tk), tk)

    def scores(j, dst):
        for hd, sl in enumerate(heads):
            dst[hd] = jnp.dot(k_ref[keys(j), sl], qt_ref[sl, :], preferred_element_type=F32)

    def accumulate(j, src):
        for hd, sl in enumerate(heads):
            st = src[hd]
            m_old = m_sc[hd]
            m_new = jnp.maximum(m_old, jnp.max(st, axis=0, keepdims=True))
            pt = jnp.exp2(st - m_new).astype(BF16)
            acc_sc[hd] = acc_sc[hd] * jnp.exp2(m_old - m_new) + jnp.dot(
                vt_ref[sl, keys(j)], pt, preferred_element_type=F32)
            m_sc[hd] = m_new

    scores(0, s0_sc)

    def body(jj, carry):
        j = 2 * jj
        scores(j + 1, s1_sc)
        accumulate(j, s0_sc)
        scores(j + 2, s0_sc)
        accumulate(j + 1, s1_sc)
        return carry

    lax.fori_loop(0, n_chunks // 2 - 1, body, 0)
    scores(n_chunks - 1, s1_sc)
    accumulate(n_chunks - 2, s0_sc)
    accumulate(n_chunks - 1, s1_sc)
    outs = []
    for hd in range(2):
        acc = acc_sc[hd]
        outs.append(acc[:B_V] * (1.0 / acc[B_V:B_V + 1]))
    o_ref[...] = jnp.concatenate(outs, axis=0).T.astype(BF16)


def _mla_attn(qt, k, vt):
    batch, s_len, hw = k.shape
    tq = TQ_MLA
    out = pl.pallas_call(
        functools.partial(_mla_attn_kernel, tk=TK_MLA),
        grid=(batch, B_HEADS // 2, s_len // tq),
        in_specs=[pl.BlockSpec((None, 2 * LANES, tq), lambda b, p, i: (b, p, i)),
                  pl.BlockSpec((None, s_len, 2 * LANES), lambda b, p, i: (b, 0, p)),
                  pl.BlockSpec((None, 2 * LANES, s_len), lambda b, p, i: (b, p, 0))],
        out_specs=pl.BlockSpec((None, tq, LANES), lambda b, p, i: (b, i, p)),
        out_shape=jax.ShapeDtypeStruct((batch, s_len, B_HEADS * B_V), BF16),
        scratch_shapes=[pltpu.VMEM((2, 1, tq), F32), pltpu.VMEM((2, LANES, tq), F32),
                        pltpu.VMEM((2, TK_MLA, tq), F32), pltpu.VMEM((2, TK_MLA, tq), F32)],
        name="mla_attn",
    )(qt, k, vt)
    return out.reshape(batch * s_len, B_HEADS * B_V)


def _win_attn_kernel(sink_ref, qt_ref, kp_ref, kc_ref, kn_ref, vp_ref, vc_ref, vn_ref, o_ref, *, s_len):
    n = pl.program_id(1)
    grp = A_HEADS // A_KV_HEADS
    dh = A_HEAD_DIM
    tq = kc_ref.shape[0]
    kj = lax.broadcasted_iota(jnp.int32, (tq + 2 * BLK, tq), 0)
    qi = lax.broadcasted_iota(jnp.int32, (tq + 2 * BLK, tq), 1)
    k_abs = n * tq - BLK + kj
    ok = (jnp.abs(kj - BLK - qi) <= A_WINDOW) & (k_abs >= 0) & (k_abs < s_len)
    ok = jnp.concatenate([ok] * grp, axis=1)
    lane_head = lax.broadcasted_iota(jnp.int32, (1, grp * tq), 1) // tq
    kband = jnp.concatenate([kp_ref[:, :LANES], kc_ref[:, :LANES], kn_ref[:, :LANES]], axis=0)
    zeros = jnp.zeros((dh, tq), BF16)
    heads = []
    for g in range(A_KV_HEADS):
        tiles = []
        for a in range(grp):
            qh = qt_ref[(g * grp + a) * dh:(g * grp + a + 1) * dh, :]
            tiles.append(jnp.concatenate([zeros] * g + [qh] + [zeros] * (A_KV_HEADS - 1 - g), axis=0))
        qg = jnp.concatenate(tiles, axis=1)
        st = jnp.where(ok, jnp.dot(kband, qg, preferred_element_type=F32), NEG_INF)
        sk = jnp.zeros((1, grp * tq), F32)
        for a in range(grp):
            sk = jnp.where(lane_head == a, sink_ref[g * grp + a] * LOG2E, sk)
        m = jnp.maximum(jnp.max(st, axis=0, keepdims=True), sk)
        pt = jnp.exp2(st - m).astype(BF16)
        rows = slice(g * LANES, (g + 1) * LANES)
        vband = jnp.concatenate([vp_ref[rows, :], vc_ref[rows, :], vn_ref[rows, :]], axis=1)
        ot = jnp.dot(vband, pt, preferred_element_type=F32)
        o = ot[:dh] * (1.0 / (ot[dh:dh + 1] + jnp.exp2(sk - m)))
        heads.extend(o[:, a * tq:(a + 1) * tq] for a in range(grp))
    for pair in range(A_HEADS // 2):
        o_ref[:, pair * LANES:(pair + 1) * LANES] = jnp.concatenate(
            [heads[2 * pair], heads[2 * pair + 1]], axis=0).T.astype(BF16)


def _win_attn(p, qt, vt, sink, batch):
    t = p.shape[0]
    s_len = t // batch
    tq = TQ_WIN
    nq, nb, r = s_len // tq, s_len // BLK, tq // BLK
    p3 = p.reshape(batch, s_len, P_WIDTH)
    kcol = P_KA // (2 * LANES)
    lo = lambda n: jnp.maximum(n * r - 1, 0)
    hi = lambda n: jnp.minimum((n + 1) * r, nb - 1)
    kedge = lambda f: pl.BlockSpec((None, BLK, 2 * LANES), lambda b, n, s: (b, f(n), kcol))
    vedge = lambda f: pl.BlockSpec((A_KV_HEADS * LANES, BLK), lambda b, n, s: (0, b * nb + f(n)))
    out = pl.pallas_call(
        functools.partial(_win_attn_kernel, s_len=s_len),
        grid_spec=pltpu.PrefetchScalarGridSpec(
            num_scalar_prefetch=1,
            grid=(batch, nq),
            in_specs=[pl.BlockSpec((A_HEADS * A_HEAD_DIM, tq), lambda b, n, s: (0, b * nq + n)),
                      kedge(lo), pl.BlockSpec((None, tq, 2 * LANES), lambda b, n, s: (b, n, kcol)), kedge(hi),
                      vedge(lo), pl.BlockSpec((A_KV_HEADS * LANES, tq), lambda b, n, s: (0, b * nq + n)), vedge(hi)],
            out_specs=pl.BlockSpec((None, tq, A_HEADS * A_HEAD_DIM), lambda b, n, s: (b, n, 0)),
        ),
        out_shape=jax.ShapeDtypeStruct((batch, s_len, A_HEADS * A_HEAD_DIM), BF16),
        name="win_attn",
    )(sink, qt, p3, p3, p3, vt, vt, vt)
    return out.reshape(t, A_HEADS * A_HEAD_DIM)


def _mem_kv_kernel(mem_ref, g_ref, w_ref, o_ref):
    mem_n = _rms(mem_ref[...], g_ref[...]).astype(BF16)
    o_ref[...] = jnp.dot(mem_n, w_ref[...], preferred_element_type=F32).astype(BF16)


def _mem_kv(mem2, g_mem, w_mem_kv):
    depth, d, width = w_mem_kv.shape
    rows = mem2.shape[0]
    return pl.pallas_call(
        _mem_kv_kernel,
        grid=(depth,),
        in_specs=[pl.BlockSpec((rows, d), lambda l: (0, 0)),
                  pl.BlockSpec((1, d), lambda l: (0, 0)),
                  pl.BlockSpec((None, d, width), lambda l: (l, 0, 0))],
        out_specs=pl.BlockSpec((None, rows, width), lambda l: (l, 0, 0)),
        out_shape=jax.ShapeDtypeStruct((depth, rows, width), BF16),
        name="mem_kv",
    )(mem2, g_mem, w_mem_kv)


def _mem_attn_kernel(q_ref, kv_ref, o_ref):
    dh = M_HEAD_DIM
    for hd in range(M_HEADS):
        q = q_ref[:, hd * dh:(hd + 1) * dh]
        k = kv_ref[:, hd * dh:(hd + 1) * dh]
        v = kv_ref[:, (M_HEADS + hd) * dh:(M_HEADS + hd + 1) * dh]
        sc = lax.dot_general(q, k, (((1,), (1,)), ((), ())), preferred_element_type=F32)
        e = jnp.exp(sc - jnp.max(sc, axis=-1, keepdims=True))
        pr = (e * (1.0 / jnp.sum(e, axis=-1, keepdims=True))).astype(BF16)
        o_ref[:, hd * dh:(hd + 1) * dh] = jnp.dot(pr, v, preferred_element_type=F32).astype(BF16)


def _mem_attn(p, mem_kv, layer, batch):
    t = p.shape[0]
    s_len = t // batch
    tm = TM_MEM
    n_mem = mem_kv.shape[1] // batch
    width = M_HEADS * M_HEAD_DIM
    p3 = p.reshape(batch, s_len, P_WIDTH)
    out = pl.pallas_call(
        _mem_attn_kernel,
        grid=(batch, s_len // tm),
        in_specs=[pl.BlockSpec((None, tm, width), lambda b, i: (b, i, P_QM // width)),
                  pl.BlockSpec((None, n_mem, 2 * width), lambda b, i: (layer, b, 0))],
        out_specs=pl.BlockSpec((None, tm, width), lambda b, i: (b, i, 0)),
        out_shape=jax.ShapeDtypeStruct((batch, s_len, width), BF16),
        name="mem_attn",
    )(p3, mem_kv)
    return out.reshape(t, width)


def _mix_kernel(x_ref, ya_ref, yb_ref, ym_ref, g_ref, wb_ref, wo_ref, o_ref):
    d = x_ref.shape[1]
    mixed = None
    for i, y_ref in enumerate((ya_ref, yb_ref, ym_ref)):
        gate = jax.nn.sigmoid(g_ref[:, i * d:(i + 1) * d].astype(F32))
        term = gate * jnp.dot(y_ref[...], wb_ref[i], preferred_element_type=F32)
        mixed = term if mixed is None else mixed + term
    o_ref[...] = x_ref[...] + jnp.dot(mixed.astype(BF16), wo_ref[...], preferred_element_type=F32)


def _mix(x, ya, yb, ym, p, wb, wo, layer):
    t, d = x.shape
    tm = TM_MIX
    bw = ya.shape[1]
    row = lambda i: (i, 0)
    return pl.pallas_call(
        _mix_kernel,
        grid=(t // tm,),
        in_specs=[pl.BlockSpec((tm, d), row),
                  pl.BlockSpec((tm, bw), row), pl.BlockSpec((tm, bw), row), pl.BlockSpec((tm, bw), row),
                  pl.BlockSpec((tm, 3 * d), lambda i: (i, P_GATES)),
                  pl.BlockSpec((None, 3, bw, d), lambda i: (layer, 0, 0, 0)),
                  pl.BlockSpec((None, d, d), lambda i: (layer, 0, 0))],
        out_specs=pl.BlockSpec((tm, d), row),
        out_shape=jax.ShapeDtypeStruct((t, d), F32),
        name="mix_out",
    )(x, ya, yb, ym, p, wb, wo)


def _swiglu_step(h, wg, wu, wd):
    gate = jnp.dot(h, wg, preferred_element_type=F32)
    up = jnp.dot(h, wu, preferred_element_type=F32)
    act = (gate * jax.nn.sigmoid(gate) * up).astype(BF16)
    return jnp.dot(act, wd, preferred_element_type=F32)


def _ffn_kernel(x_ref, g_ref, wg_ref, wu_ref, wd_ref, o_ref, h_sc, acc_sc):
    j = pl.program_id(1)

    @pl.when(j == 0)
    def _():
        h_sc[...] = _rms(x_ref[...], g_ref[...]).astype(BF16)
        acc_sc[...] = jnp.zeros(acc_sc.shape, F32)

    acc_sc[...] += _swiglu_step(h_sc[...], wg_ref[...], wu_ref[...], wd_ref[...])

    @pl.when(j == pl.num_programs(1) - 1)
    def _():
        o_ref[...] = x_ref[...] + acc_sc[...]


def _ffn(x, g, wg, wu, wd, layer):
    t, d = x.shape
    f = wg.shape[2]
    tm, tf = TM_FFN, TF_FFN
    return pl.pallas_call(
        _ffn_kernel,
        grid=(t // tm, f // tf),
        in_specs=[pl.BlockSpec((tm, d), lambda i, j: (i, 0)),
                  pl.BlockSpec((1, d), lambda i, j: (0, 0)),
                  pl.BlockSpec((None, d, tf), lambda i, j: (layer, 0, j)),
                  pl.BlockSpec((None, d, tf), lambda i, j: (layer, 0, j)),
                  pl.BlockSpec((None, tf, d), lambda i, j: (layer, j, 0))],
        out_specs=pl.BlockSpec((tm, d), lambda i, j: (i, 0)),
        out_shape=jax.ShapeDtypeStruct((t, d), F32),
        scratch_shapes=[pltpu.VMEM((tm, d), BF16), pltpu.VMEM((tm, d), F32)],
        name="ffn_dense",
    )(x, g, wg, wu, wd)


def _router_kernel(x_ref, g_ref, wr_ref, idx_ref, wgt_ref, xt_ref):
    x = x_ref[...]
    xt_ref[...] = x.reshape(xt_ref.shape)
    h = _rms(x, g_ref[...])
    logits = lax.dot_general(wr_ref[...], h, (((1,), (1,)), ((), ())),
                             precision=lax.Precision.HIGHEST, preferred_element_type=F32)
    e_id = lax.broadcasted_iota(jnp.int32, logits.shape, 0)
    m1 = jnp.max(logits, axis=0, keepdims=True)
    i1 = jnp.min(jnp.where(logits == m1, e_id, N_EXPERTS), axis=0, keepdims=True)
    rest = jnp.where(e_id == i1, -jnp.inf, logits)
    m2 = jnp.max(rest, axis=0, keepdims=True)
    i2 = jnp.min(jnp.where(rest == m2, e_id, N_EXPERTS), axis=0, keepdims=True)
    e2 = jnp.exp(m2 - m1)
    w1 = 1.0 / (1.0 + e2)
    idx_ref[...] = jnp.concatenate([i1, i2], axis=0)
    wgt_ref[...] = jnp.concatenate([w1, e2 * w1], axis=0)


def _router(x, g, wr_t):
    t, d = x.shape
    tm = TM_ROUTE
    return pl.pallas_call(
        _router_kernel,
        grid=(t // tm,),
        in_specs=[pl.BlockSpec((tm, d), lambda i: (i, 0)),
                  pl.BlockSpec((1, d), lambda i: (0, 0)),
                  pl.BlockSpec((N_EXPERTS, d), lambda i: (0, 0))],
        out_specs=[pl.BlockSpec((TOP_K, tm), lambda i: (0, i)),
                   pl.BlockSpec((TOP_K, tm), lambda i: (0, i)),
                   pl.BlockSpec((tm * SUBLANES, LANES), lambda i: (i, 0))],
        out_shape=[jax.ShapeDtypeStruct((TOP_K, t), jnp.int32),
                   jax.ShapeDtypeStruct((TOP_K, t), F32),
                   jax.ShapeDtypeStruct((t * SUBLANES, LANES), F32)],
        name="moe_router",
    )(x, g, wr_t)


def _moe_rows_per_step(n_steps):
    return -(-TM_EXP // (n_steps * SUBLANES)) * SUBLANES


def _slot_tables(top_i, n_tok, buf_rows):
    n_asg = n_tok * TOP_K
    flat_e = top_i.T.reshape(n_asg)
    order = jnp.argsort(flat_e, stable=True).astype(jnp.int32)
    counts = jnp.sum(flat_e[:, None] == jnp.arange(N_EXPERTS, dtype=jnp.int32)[None, :], axis=0, dtype=jnp.int32)
    padded = (counts + TM_EXP - 1) // TM_EXP * TM_EXP
    pad_end = jnp.cumsum(padded)
    start = jnp.cumsum(counts) - counts
    pads_before = jnp.cumsum(padded - counts) - (padded - counts)
    n_slots = n_asg + N_EXPERTS * TM_EXP
    n_blk = n_slots // TM_EXP
    extra = buf_rows - TM_EXP
    blk_e = jnp.minimum(jnp.searchsorted(pad_end, jnp.arange(n_blk, dtype=jnp.int32) * TM_EXP, side='right'),
                        N_EXPERTS - 1).astype(jnp.int32)
    slot_e = jnp.repeat(blk_e, TM_EXP)
    rank = jnp.arange(n_slots, dtype=jnp.int32) - (pad_end - padded)[slot_e]
    is_pad = rank >= counts[slot_e]
    asg = order[jnp.clip(start[slot_e] + rank, 0, n_asg - 1)]
    slot_tok = jnp.where(is_pad, 0, asg // TOP_K)
    slot_row = jnp.where(is_pad, n_asg + pads_before[slot_e] + rank - counts[slot_e],
                         (asg % TOP_K) * n_tok + slot_tok)
    gather_tok = jnp.concatenate([slot_tok.reshape(n_blk, TM_EXP), jnp.zeros((n_blk, extra), jnp.int32)], axis=1)
    spare = n_slots + jnp.arange((n_blk + 1) * buf_rows - n_slots, dtype=jnp.int32)
    lead, over = spare[:TM_EXP], spare[TM_EXP:].reshape(n_blk + 1, extra)
    scatter_row = jnp.concatenate(
        [jnp.concatenate([lead[None], slot_row.reshape(n_blk, TM_EXP)], axis=0), over], axis=1)
    n_live = (pad_end[-1] // TM_EXP).astype(jnp.int32)
    blk_meta = jnp.concatenate([blk_e, n_live[None]])
    return blk_meta, gather_tok.reshape(-1).astype(jnp.int32), scatter_row.reshape(-1).astype(jnp.int32)


def _moe_ffn_kernel(blk_e_ref, gtok_ref, srow_ref, x_hbm, g_ref, wg_ref, wu_ref, wd_ref, o_hbm,
                    xg_sc, ys_sc, h_sc, acc_sc, gsem, ssem, *, rows_per_step):
    i, j = pl.program_id(0), pl.program_id(1)
    n_blk, n_steps = pl.num_programs(0), pl.num_programs(1)
    rows, d = h_sc.shape
    tr = d // LANES
    buf_rows = xg_sc.shape[1] // tr
    slot = i % 2
    other = 1 - slot

    def tile(ref, k):
        return ref.at[pl.ds(pl.multiple_of(k * tr, tr), tr), :]

    def gathered(s):
        return pltpu.make_async_copy(x_hbm.at[pl.ds(0, buf_rows * tr), :], xg_sc.at[s], gsem.at[s])

    def scattered(s):
        return pltpu.make_async_copy(ys_sc.at[s], o_hbm.at[pl.ds(0, buf_rows * tr), :], ssem.at[s])

    def get_row(s, k, entry):
        return pltpu.make_async_copy(tile(x_hbm, gtok_ref[entry]), tile(xg_sc.at[s], k), gsem.at[s])

    def put_row(s, k, entry):
        return pltpu.make_async_copy(tile(ys_sc.at[s], k), tile(o_hbm, srow_ref[entry]), ssem.at[s])

    @pl.when((i == 0) & (j == 0))
    def _():
        ys_sc[...] = jnp.zeros(ys_sc.shape, F32)

        def start(k, c):
            get_row(0, k, k).start()
            return c

        lax.fori_loop(0, buf_rows, start, 0)

    @pl.when(j == 0)
    def _():
        gathered(slot).wait()
        for s in range(2):
            @pl.when(slot == s)
            def _():
                xs = [xg_sc[s, pl.ds(c, rows, stride=tr), :] for c in range(tr)]
                ms = sum(jnp.sum(x * x, axis=-1, keepdims=True) for x in xs) * (1.0 / d)
                inv = lax.rsqrt(ms + EPS)
                for c, x in enumerate(xs):
                    cols = slice(c * LANES, (c + 1) * LANES)
                    h_sc[:, cols] = (x * inv * g_ref[:, cols]).astype(BF16)

        acc_sc[...] = jnp.zeros(acc_sc.shape, F32)

    nxt = jnp.minimum(i + 1, n_blk - 1)

    def issue_row_copies():
        for r in range(rows_per_step):
            k = j * rows_per_step + r
            get_row(other, k, nxt * buf_rows + k).start(priority=r % DMA_QUEUES)
            put_row(other, k, i * buf_rows + k).start(priority=(r + 1) % DMA_QUEUES)

    live = i < blk_e_ref[n_blk]

    @pl.when(live)
    def _():
        issue_row_copies()
        acc_sc[...] += _swiglu_step(h_sc[...], wg_ref[...].astype(BF16), wu_ref[...].astype(BF16),
                                    wd_ref[...].astype(BF16))

    @pl.when(jnp.logical_not(live))
    def _():
        issue_row_copies()

    @pl.when(j == n_steps - 1)
    def _():
        @pl.when(i > 0)
        def _():
            scattered(slot).wait()

        for s in range(2):
            @pl.when(slot == s)
            def _():
                for c in range(tr):
                    ys_sc[s, pl.ds(c, rows, stride=tr), :] = acc_sc[:, c * LANES:(c + 1) * LANES]

    @pl.when((i == n_blk - 1) & (j == n_steps - 1))
    def _():
        def start(k, c):
            put_row(slot, k, n_blk * buf_rows + k).start()
            return c

        lax.fori_loop(0, buf_rows, start, 0)
        scattered(other).wait()
        scattered(slot).wait()
        gathered(other).wait()


def _moe_ffn(x_tiles, g, wg, wu, wd, layer, blk_e, gather_tok, scatter_row):
    d, f = wg.shape[2], wg.shape[3]
    tr = d // LANES
    assert tr % SUBLANES == 0
    tm, tf = TM_EXP, TF_FFN
    n_steps = f // tf
    rows_per_step = _moe_rows_per_step(n_steps)
    buf_rows = rows_per_step * n_steps
    n_blk = gather_tok.shape[0] // buf_rows

    def ff(i, j, be):
        return jnp.where(i < be[n_blk], j, n_steps - 1)

    return pl.pallas_call(
        functools.partial(_moe_ffn_kernel, rows_per_step=rows_per_step),
        grid_spec=pltpu.PrefetchScalarGridSpec(
            num_scalar_prefetch=3,
            grid=(n_blk, n_steps),
            in_specs=[pl.BlockSpec(memory_space=pl.ANY),
                      pl.BlockSpec((1, d), lambda i, j, be, gt, sr: (0, 0)),
                      pl.BlockSpec((None, None, d, tf), lambda i, j, be, gt, sr: (layer, be[i], 0, ff(i, j, be))),
                      pl.BlockSpec((None, None, d, tf), lambda i, j, be, gt, sr: (layer, be[i], 0, ff(i, j, be))),
                      pl.BlockSpec((None, None, tf, d), lambda i, j, be, gt, sr: (layer, be[i], ff(i, j, be), 0))],
            out_specs=pl.BlockSpec(memory_space=pl.ANY),
            scratch_shapes=[pltpu.VMEM((2, buf_rows * tr, LANES), F32), pltpu.VMEM((2, buf_rows * tr, LANES), F32),
                            pltpu.VMEM((tm, d), BF16), pltpu.VMEM((tm, d), F32),
                            pltpu.SemaphoreType.DMA((2,)), pltpu.SemaphoreType.DMA((2,))],
        ),
        out_shape=jax.ShapeDtypeStruct((scatter_row.shape[0] * tr, LANES), F32),
        name="moe_ffn",
    )(blk_e, gather_tok, scatter_row, x_tiles, g, wg, wu, wd)


def _moe_combine_kernel(x_ref, w_ref, y0_ref, y1_ref, gf_ref, o_ref, *, final_norm):
    tm, d = x_ref.shape
    y0, y1 = y0_ref[...].reshape(tm, d), y1_ref[...].reshape(tm, d)
    out = x_ref[...] + w_ref[:, 0:1] * y0 + w_ref[:, 1:2] * y1
    if final_norm:
        out = _rms(out, gf_ref[...])
    o_ref[...] = out


def _moe_combine(x, top_w, y, g_final, final_norm):
    t, d = x.shape
    tr = d // LANES
    tm = TM_COMB
    nt = t // tm
    return pl.pallas_call(
        functools.partial(_moe_combine_kernel, final_norm=final_norm),
        grid=(nt,),
        in_specs=[pl.BlockSpec((tm, d), lambda i: (i, 0)),
                  pl.BlockSpec((tm, TOP_K), lambda i: (i, 0)),
                  pl.BlockSpec((tm * tr, LANES), lambda i: (i, 0)),
                  pl.BlockSpec((tm * tr, LANES), lambda i: (nt + i, 0)),
                  pl.BlockSpec((1, d), lambda i: (0, 0))],
        out_specs=pl.BlockSpec((tm, d), lambda i: (i, 0)),
        out_shape=jax.ShapeDtypeStruct((t, d), F32),
        name="moe_combine",
    )(x, top_w, y, y, g_final)


def _final_norm_kernel(x_ref, g_ref, o_ref):
    o_ref[...] = _rms(x_ref[...], g_ref[...])


def _final_norm(x, g):
    t, d = x.shape
    tm = 1024
    return pl.pallas_call(
        _final_norm_kernel,
        grid=(t // tm,),
        in_specs=[pl.BlockSpec((tm, d), lambda i: (i, 0)), pl.BlockSpec((1, d), lambda i: (0, 0))],
        out_specs=pl.BlockSpec((tm, d), lambda i: (i, 0)),
        out_shape=jax.ShapeDtypeStruct((t, d), F32),
        name="final_norm",
    )(x, g)


def _w_in_relayout_kernel(w_ref, tok_ref, qt_ref, vt_ref):
    w = w_ref[...]
    rows = w.shape[0]
    qa, ka, va = w[:, 0:512], w[:, 512:640], w[:, 640:768]
    cq, ckv, kr = w[:, 768:1024], w[:, 1024:1152], w[:, 1152:1184]
    qm, gates = w[:, 1184:1696], w[:, 1696:4768]
    pad = jnp.zeros((rows, LANES - B_ROPE), w.dtype)
    tok_ref[...] = jnp.concatenate([gates, qm, ka, kr, pad, cq, ckv], axis=1).astype(BF16)
    qt_ref[...] = qa.T.astype(BF16)
    vt_ref[...] = va.T.astype(BF16)


def _relayout_w_in(w_in):
    depth, d, width = w_in.shape
    tr = 256
    qrows, vrows = A_HEADS * A_HEAD_DIM, A_KV_HEADS * A_HEAD_DIM
    return pl.pallas_call(
        _w_in_relayout_kernel,
        grid=(depth, d // tr),
        in_specs=[pl.BlockSpec((None, tr, width), lambda l, i: (l, i, 0))],
        out_specs=[pl.BlockSpec((None, tr, P_WIDTH), lambda l, i: (l, i, 0)),
                   pl.BlockSpec((None, qrows, tr), lambda l, i: (l, 0, i)),
                   pl.BlockSpec((None, vrows, tr), lambda l, i: (l, 0, i))],
        out_shape=[jax.ShapeDtypeStruct((depth, d, P_WIDTH), BF16),
                   jax.ShapeDtypeStruct((depth, qrows, d), BF16),
                   jax.ShapeDtypeStruct((depth, vrows, d), BF16)],
        name="w_in_relayout",
    )(w_in)


def _relayout_latent(w_uq, w_ukv):
    depth = w_uq.shape[0]
    q = w_uq.reshape(depth, B_Q_RANK, B_HEADS, B_NOPE + B_ROPE)
    q = jnp.pad(q, ((0, 0), (0, 0), (0, 0), (0, LANES - B_NOPE - B_ROPE)))
    kv = w_ukv.reshape(depth, B_KV_RANK, B_HEADS, B_NOPE + B_V)
    k = jnp.pad(kv[..., :B_NOPE], ((0, 0), (0, 0), (0, 0), (0, LANES - B_NOPE)))
    v = jnp.pad(kv[..., B_NOPE:], ((0, 0), (0, 0), (0, 0), (0, LANES - B_V)))
    flat = lambda a, r: a.reshape(depth, r, B_HEADS * LANES).astype(BF16)
    return (jnp.swapaxes(flat(q, B_Q_RANK), 1, 2), flat(k, B_KV_RANK),
            jnp.swapaxes(flat(v, B_KV_RANK), 1, 2))


def kernel(x, mem, positions, g_mix, w_in, g_q_lat, w_uq, g_kv_lat, w_ukv, sink, w_mem_kv, w_branch, w_out,
           g_ffn, w_gate_dense, w_up_dense, w_down_dense, w_router, w_gate_exp, w_up_exp, w_down_exp,
           g_mem, g_final):
    batch, s_len, d = x.shape
    depth = w_in.shape[0]
    t = batch * s_len
    row = lambda a: a.reshape(1, -1).astype(F32)

    w_in_p, wqa_t, wva_t = _relayout_w_in(w_in)
    wq_p, wk_p, wv_p = _relayout_latent(w_uq, w_ukv)
    bf = lambda w: w.astype(BF16)
    w_branch_b, w_out_b = bf(w_branch), bf(w_out)
    wgd, wud, wdd = bf(w_gate_dense), bf(w_up_dense), bf(w_down_dense)
    wge, wue, wde = w_gate_exp, w_up_exp, w_down_exp
    wr_t = jnp.swapaxes(w_router, 1, 2).astype(F32)

    cos_a, sin_a, cos_b, sin_b, cos_t, sin_t = _rope_tables(positions)
    mem_kv = _mem_kv(mem.reshape(batch * mem.shape[1], d).astype(F32), row(g_mem), bf(w_mem_kv))

    xt = x.reshape(t, d).astype(F32)
    for l in range(depth):
        p, qa_t, va_t = _in_proj(xt, row(g_mix[l]), w_in_p, wqa_t, wva_t, l, cos_a, sin_a, cos_t, sin_t)
        y_a = _win_attn(p, qa_t, va_t, sink[l].astype(F32), batch)
        qt, k, vt = _mla_prep(p, row(g_q_lat[l]), row(g_kv_lat[l]), wq_p[l], wk_p[l], wv_p[l],
                              cos_b, sin_b, cos_t, sin_t, batch)
        y_b = _mla_attn(qt, k, vt)
        y_m = _mem_attn(p, mem_kv, l, batch)
        xt = _mix(xt, y_a, y_b, y_m, p, w_branch_b, w_out_b, l)
        i = l // 2
        if l % 2 == 0:
            xt = _ffn(xt, row(g_ffn[l]), wgd, wud, wdd, i)
            if l == depth - 1:
                xt = _final_norm(xt, row(g_final))
        else:
            top_i, top_w, x_tiles = _router(xt, row(g_ffn[l]), wr_t[i])
            n_steps = w_gate_exp.shape[3] // TF_FFN
            buf_rows = _moe_rows_per_step(n_steps) * n_steps
            blk_e, gather_tok, scatter_row = _slot_tables(top_i, t, buf_rows)
            y = _moe_ffn(x_tiles, row(g_ffn[l]), wge, wue, wde, i, blk_e, gather_tok, scatter_row)
            xt = _moe_combine(xt, top_w.T, y, row(g_final), final_norm=(l == depth - 1))
    return xt.reshape(batch, s_len, d).astype(x.dtype)
```

```python
import functools

import jax
import jax.numpy as jnp
from jax import lax
from jax.experimental import pallas as pl
from jax.experimental.pallas import tpu as pltpu

F32 = jnp.float32
BF16 = jnp.bfloat16

ROPE_THETA = 10000.0
EPS = 1e-6
NEG_INF = -1e30
LOG2E = 1.4426950408889634

A_HEADS, A_KV_HEADS, A_HEAD_DIM, A_WINDOW = 8, 2, 64, 128
B_HEADS, B_Q_RANK, B_KV_RANK, B_NOPE, B_ROPE, B_V = 8, 256, 128, 64, 32, 64
M_HEADS, M_HEAD_DIM = 4, 128
N_EXPERTS, TOP_K = 8, 2
LANES = 128
SUBLANES = 8
BLK = 128
B_VROWS = 80

P_GATES = 0
P_QM = 3072
P_KA = 3584
P_CQ = 3840
P_CKV = 4096
P_WIDTH = 4224

TM_PROJ = 512
TM_PREP = 512
TM_MIX = 512
TM_MEM = 512
TM_FFN = 1024
TF_FFN = 512
TQ_WIN = 256
TQ_MLA = 2048
TK_MLA = 512
TM_ROUTE = 1024
TM_EXP = 1024
TM_COMB = 512
DMA_QUEUES = 2


def _rms(x, g):
    return x * lax.rsqrt(jnp.mean(x * x, axis=-1, keepdims=True) + EPS) * g


def _rot_tile(x, cos, sin_signed, first, shift):
    up = pltpu.roll(x, LANES - shift, 1)
    dn = pltpu.roll(x, shift, 1)
    return x * cos + jnp.where(first, up, dn) * sin_signed


def _rope_table_kernel(posr_ref, par_ref, invc_ref, ca_ref, sa_ref, cb_ref, sb_ref, cbt_ref, sbt_ref):
    ang_t = invc_ref[...] * posr_ref[...]
    ct, st = jnp.cos(ang_t), jnp.sin(ang_t)
    cbt_ref[...] = ct
    sbt_ref[...] = st
    half_a, half_b = A_HEAD_DIM // 2, B_ROPE // 2
    tm = ct.shape[1]
    ca, sa = ct[:half_a].T, st[:half_a].T
    ca_ref[...] = jnp.concatenate([ca] * (LANES // half_a), axis=1)
    sa_ref[...] = jnp.concatenate([sa] * (LANES // half_a), axis=1) * par_ref[1:2, :]
    cb, sb = ct[half_a:].T, st[half_a:].T
    one = jnp.ones((tm, B_NOPE), F32)
    pad = jnp.ones((tm, LANES - B_NOPE - B_ROPE), F32)
    cb_ref[...] = jnp.concatenate([one, cb, cb, pad], axis=1)
    sb_ref[...] = jnp.concatenate([one, sb, sb, pad], axis=1) * par_ref[3:4, :]


def _rope_tables(positions):
    t = positions.size
    tm = 2048
    lane = jnp.arange(LANES)
    half_a = A_HEAD_DIM // 2
    inv_a = jnp.power(F32(ROPE_THETA), -jnp.arange(half_a, dtype=F32) * (2.0 / A_HEAD_DIM))
    sgn_a = jnp.where((lane % A_HEAD_DIM) < half_a, -1.0, 1.0).astype(F32)
    half_b = B_ROPE // 2
    inv_b = jnp.power(F32(ROPE_THETA), -jnp.arange(half_b, dtype=F32) * (2.0 / B_ROPE))
    in_rope = (lane >= B_NOPE) & (lane < B_NOPE + B_ROPE)
    sgn_b = jnp.where(in_rope, jnp.where(lane < B_NOPE + half_b, -1.0, 1.0), 0.0).astype(F32)
    par = jnp.zeros((8, LANES), F32).at[1].set(sgn_a).at[3].set(sgn_b)
    tab = jax.ShapeDtypeStruct((t, LANES), F32)
    n_freq = half_a + half_b
    inv_col = jnp.concatenate([inv_a, inv_b]).reshape(n_freq, 1)
    tab_t = jax.ShapeDtypeStruct((n_freq, t), F32)
    return pl.pallas_call(
        _rope_table_kernel,
        grid=(t // tm,),
        in_specs=[pl.BlockSpec((1, tm), lambda i: (0, i)),
                  pl.BlockSpec((8, LANES), lambda i: (0, 0)),
                  pl.BlockSpec((n_freq, 1), lambda i: (0, 0))],
        out_specs=[pl.BlockSpec((tm, LANES), lambda i: (i, 0))] * 4
        + [pl.BlockSpec((n_freq, tm), lambda i: (0, i))] * 2,
        out_shape=[tab] * 4 + [tab_t] * 2,
        name="rope_tables",
    )(positions.reshape(1, t).astype(F32), par, inv_col)


def _in_proj_kernel(x_ref, g_ref, w_ref, wqt_ref, wvt_ref, cos_ref, sin_ref, cost_ref, sint_ref,
                    o_ref, qt_ref, vt_ref):
    h = _rms(x_ref[...], g_ref[...]).astype(BF16)
    half = A_HEAD_DIM // 2

    def proj(c0, c1):
        return jnp.dot(h, w_ref[:, c0:c1], preferred_element_type=F32)

    for c0 in range(P_GATES, P_QM, 512):
        o_ref[:, c0:c0 + 512] = proj(c0, c0 + 512).astype(BF16)
    o_ref[:, P_QM:P_KA] = (proj(P_QM, P_KA) * (M_HEAD_DIM ** -0.5)).astype(BF16)
    cos, sin = cos_ref[...], sin_ref[...]
    lane = lax.broadcasted_iota(jnp.int32, cos.shape, 1)
    ka = proj(P_KA, P_CQ)
    o_ref[:, P_KA:P_KA + LANES] = _rot_tile(ka[:, :LANES], cos, sin, (lane % A_HEAD_DIM) < half, half).astype(BF16)
    o_ref[:, P_KA + LANES:P_CQ] = ka[:, LANES:].astype(BF16)
    o_ref[:, P_CQ:P_WIDTH] = proj(P_CQ, P_WIDTH).astype(BF16)

    nt = (((1,), (1,)), ((), ()))
    ct, st = cost_ref[...], sint_ref[...]
    qscale = A_HEAD_DIM ** -0.5 * LOG2E
    qt = lax.dot_general(wqt_ref[...], h, nt, preferred_element_type=F32)
    for hd in range(A_HEADS):
        r0 = hd * A_HEAD_DIM
        x1, x2 = qt[r0:r0 + half], qt[r0 + half:r0 + A_HEAD_DIM]
        qt_ref[r0:r0 + half, :] = ((x1 * ct - x2 * st) * qscale).astype(BF16)
        qt_ref[r0 + half:r0 + A_HEAD_DIM, :] = ((x2 * ct + x1 * st) * qscale).astype(BF16)
    vt = lax.dot_general(wvt_ref[...], h, nt, preferred_element_type=F32)
    row = lax.broadcasted_iota(jnp.int32, (LANES - A_HEAD_DIM, vt.shape[1]), 0)
    ones_pad = (row == 0).astype(BF16)
    for g in range(A_KV_HEADS):
        vt_ref[g * LANES:g * LANES + A_HEAD_DIM, :] = vt[g * A_HEAD_DIM:(g + 1) * A_HEAD_DIM].astype(BF16)
        vt_ref[g * LANES + A_HEAD_DIM:(g + 1) * LANES, :] = ones_pad


def _in_proj(x, g, w, wqt, wvt, layer, cos_a, sin_a, cos_t, sin_t):
    t, d = x.shape
    tm = TM_PROJ
    half = A_HEAD_DIM // 2
    qrows, vrows = A_HEADS * A_HEAD_DIM, A_KV_HEADS * LANES
    const = lambda i: (0, 0)
    this_layer = lambda i: (layer, 0, 0)
    return pl.pallas_call(
        _in_proj_kernel,
        grid=(t // tm,),
        in_specs=[pl.BlockSpec((tm, d), lambda i: (i, 0)),
                  pl.BlockSpec((1, d), const),
                  pl.BlockSpec((None, d, P_WIDTH), this_layer),
                  pl.BlockSpec((None, qrows, d), this_layer),
                  pl.BlockSpec((None, A_KV_HEADS * A_HEAD_DIM, d), this_layer),
                  pl.BlockSpec((tm, LANES), lambda i: (i, 0)),
                  pl.BlockSpec((tm, LANES), lambda i: (i, 0)),
                  pl.BlockSpec((half, tm), lambda i: (0, i)),
                  pl.BlockSpec((half, tm), lambda i: (0, i))],
        out_specs=[pl.BlockSpec((tm, P_WIDTH), lambda i: (i, 0)),
                   pl.BlockSpec((qrows, tm), lambda i: (0, i)),
                   pl.BlockSpec((vrows, tm), lambda i: (0, i))],
        out_shape=[jax.ShapeDtypeStruct((t, P_WIDTH), BF16),
                   jax.ShapeDtypeStruct((qrows, t), BF16),
                   jax.ShapeDtypeStruct((vrows, t), BF16)],
        name="in_proj",
    )(x, g, w, wqt, wvt, cos_a, sin_a, cos_t, sin_t)


def _mla_prep_kernel(cq_ref, ckv_ref, kr_ref, gq_ref, gkv_ref, wqt_ref, wk_ref, wvt_ref,
                     cos_ref, sin_ref, cost_ref, sint_ref, qt_ref, k_ref, vt_ref):
    half = B_ROPE // 2
    nt = (((1,), (1,)), ((), ()))
    cq = _rms(cq_ref[...].astype(F32), gq_ref[...]).astype(BF16)
    ckv = _rms(ckv_ref[...].astype(F32), gkv_ref[...]).astype(BF16)

    cos, sin = cos_ref[...], sin_ref[...]
    lane = lax.broadcasted_iota(jnp.int32, cos.shape, 1)
    first = (lane >= B_NOPE) & (lane < B_NOPE + half)
    kr = _rot_tile(pltpu.roll(kr_ref[:, LANES:].astype(F32), B_NOPE, 1), cos, sin, first, half)
    k = jnp.dot(ckv, wk_ref[...], preferred_element_type=F32)
    for hd in range(B_HEADS):
        sl = slice(hd * LANES, (hd + 1) * LANES)
        k_ref[:, sl] = (k[:, sl] + kr).astype(BF16)

    ct, st = cost_ref[...], sint_ref[...]
    qscale = (B_NOPE + B_ROPE) ** -0.5 * LOG2E
    qt = lax.dot_general(wqt_ref[...], cq, nt, preferred_element_type=F32)
    vt = lax.dot_general(wvt_ref[...], ckv, nt, preferred_element_type=F32)
    row = lax.broadcasted_iota(jnp.int32, (B_VROWS, vt.shape[1]), 0)
    ones_row = (row == B_V).astype(F32)
    for hd in range(B_HEADS):
        r0 = hd * LANES
        x1 = qt[r0 + B_NOPE:r0 + B_NOPE + half]
        x2 = qt[r0 + B_NOPE + half:r0 + B_NOPE + B_ROPE]
        qh = jnp.concatenate([qt[r0:r0 + B_NOPE], x1 * ct - x2 * st, x2 * ct + x1 * st,
                              qt[r0 + B_NOPE + B_ROPE:r0 + LANES]], axis=0)
        qt_ref[r0:r0 + LANES, :] = (qh * qscale).astype(BF16)
        v0 = hd * B_VROWS
        vt_ref[v0:v0 + B_VROWS, :] = (vt[v0:v0 + B_VROWS] + ones_row).astype(BF16)


def _mla_prep(p, gq, gkv, wqt, wk, wvt, cos_b, sin_b, cos_bt, sin_bt, batch):
    t = p.shape[0]
    s_len = t // batch
    tm = TM_PREP
    nt = s_len // tm
    hw = B_HEADS * LANES
    hv = B_HEADS * B_VROWS
    half = B_ROPE // 2
    const = lambda b, i: (0, 0)
    tok = lambda b, i: (b * nt + i, 0)
    return pl.pallas_call(
        _mla_prep_kernel,
        grid=(batch, nt),
        in_specs=[pl.BlockSpec((tm, B_Q_RANK), lambda b, i: (b * nt + i, P_CQ // B_Q_RANK)),
                  pl.BlockSpec((tm, B_KV_RANK), lambda b, i: (b * nt + i, P_CKV // B_KV_RANK)),
                  pl.BlockSpec((tm, 2 * LANES), lambda b, i: (b * nt + i, P_KA // (2 * LANES))),
                  pl.BlockSpec((1, B_Q_RANK), const),
                  pl.BlockSpec((1, B_KV_RANK), const),
                  pl.BlockSpec((hw, B_Q_RANK), const),
                  pl.BlockSpec((B_KV_RANK, hw), const),
                  pl.BlockSpec((hv, B_KV_RANK), const),
                  pl.BlockSpec((tm, LANES), tok),
                  pl.BlockSpec((tm, LANES), tok),
                  pl.BlockSpec((half, tm), lambda b, i: (A_HEAD_DIM // 2 // half, b * nt + i)),
                  pl.BlockSpec((half, tm), lambda b, i: (A_HEAD_DIM // 2 // half, b * nt + i))],
        out_specs=[pl.BlockSpec((None, hw, tm), lambda b, i: (b, 0, i)),
                   pl.BlockSpec((None, tm, hw), lambda b, i: (b, i, 0)),
                   pl.BlockSpec((None, hv, tm), lambda b, i: (b, 0, i))],
        out_shape=[jax.ShapeDtypeStruct((batch, hw, s_len), BF16),
                   jax.ShapeDtypeStruct((batch, s_len, hw), BF16),
                   jax.ShapeDtypeStruct((batch, hv, s_len), BF16)],
        name="mla_prep",
    )(p, p, p, gq, gkv, wqt, wk, wvt, cos_b, sin_b, cos_bt, sin_bt)


def _mla_attn_kernel(qt_ref, k_ref, vt_ref, o_ref, m_sc, acc_sc, s0_sc, s1_sc, *, tk):
    n_chunks = k_ref.shape[0] // tk
    m_sc[...] = jnp.full(m_sc.shape, -jnp.inf, F32)
    acc_sc[...] = jnp.zeros(acc_sc.shape, F32)
    heads = [slice(hd * LANES, (hd + 1) * LANES) for hd in range(2)]

    def keys(j):
        return pl.ds(pl.multiple_of(j * tk, tk), tk)

    def scores(j, dst):
        for hd, sl in enumerate(heads):
            dst[hd] = jnp.dot(k_ref[keys(j), sl], qt_ref[sl, :], preferred_element_type=F32)

    def accumulate(j, src):
        for hd, sl in enumerate(heads):
            st = src[hd]
            m_old = m_sc[hd]
            m_new = jnp.maximum(m_old, jnp.max(st, axis=0, keepdims=True))
            pt = jnp.exp2(st - m_new).astype(BF16)
            vrows = slice(hd * B_VROWS, (hd + 1) * B_VROWS)
            acc_sc[hd] = acc_sc[hd] * jnp.exp2(m_old - m_new) + jnp.dot(
                vt_ref[vrows, keys(j)], pt, preferred_element_type=F32)
            m_sc[hd] = m_new

    scores(0, s0_sc)

    def body(jj, carry):
        j = 2 * jj
        scores(j + 1, s1_sc)
        accumulate(j, s0_sc)
        scores(j + 2, s0_sc)
        accumulate(j + 1, s1_sc)
        return carry

    lax.fori_loop(0, n_chunks // 2 - 1, body, 0)
    scores(n_chunks - 1, s1_sc)
    accumulate(n_chunks - 2, s0_sc)
    accumulate(n_chunks - 1, s1_sc)
    outs = []
    for hd in range(2):
        acc = acc_sc[hd]
        outs.append(acc[:B_V] * (1.0 / acc[B_V:B_V + 1]))
    o_ref[...] = jnp.concatenate(outs, axis=0).T.astype(BF16)


def _mla_attn(qt, k, vt):
    batch, s_len, hw = k.shape
    tq = TQ_MLA
    out = pl.pallas_call(
        functools.partial(_mla_attn_kernel, tk=TK_MLA),
        grid=(batch, B_HEADS // 2, s_len // tq),
        in_specs=[pl.BlockSpec((None, 2 * LANES, tq), lambda b, p, i: (b, p, i)),
                  pl.BlockSpec((None, s_len, 2 * LANES), lambda b, p, i: (b, 0, p)),
                  pl.BlockSpec((None, 2 * B_VROWS, s_len), lambda b, p, i: (b, p, 0))],
        out_specs=pl.BlockSpec((None, tq, LANES), lambda b, p, i: (b, i, p)),
        out_shape=jax.ShapeDtypeStruct((batch, s_len, B_HEADS * B_V), BF16),
        scratch_shapes=[pltpu.VMEM((2, 1, tq), F32), pltpu.VMEM((2, B_VROWS, tq), F32),
                        pltpu.VMEM((2, TK_MLA, tq), F32), pltpu.VMEM((2, TK_MLA, tq), F32)],
        name="mla_attn",
    )(qt, k, vt)
    return out.reshape(batch * s_len, B_HEADS * B_V)


def _win_attn_kernel(sink_ref, qt_ref, kp_ref, kc_ref, kn_ref, vp_ref, vc_ref, vn_ref, o_ref, *, s_len):
    n = pl.program_id(1)
    grp = A_HEADS // A_KV_HEADS
    dh = A_HEAD_DIM
    tq = kc_ref.shape[0]
    kj = lax.broadcasted_iota(jnp.int32, (tq + 2 * BLK, tq), 0)
    qi = lax.broadcasted_iota(jnp.int32, (tq + 2 * BLK, tq), 1)
    k_abs = n * tq - BLK + kj
    ok = (jnp.abs(kj - BLK - qi) <= A_WINDOW) & (k_abs >= 0) & (k_abs < s_len)
    ok = jnp.concatenate([ok] * grp, axis=1)
    lane_head = lax.broadcasted_iota(jnp.int32, (1, grp * tq), 1) // tq
    kband = jnp.concatenate([kp_ref[:, :LANES], kc_ref[:, :LANES], kn_ref[:, :LANES]], axis=0)
    zeros = jnp.zeros((dh, tq), BF16)
    heads = []
    for g in range(A_KV_HEADS):
        tiles = []
        for a in range(grp):
            qh = qt_ref[(g * grp + a) * dh:(g * grp + a + 1) * dh, :]
            tiles.append(jnp.concatenate([zeros] * g + [qh] + [zeros] * (A_KV_HEADS - 1 - g), axis=0))
        qg = jnp.concatenate(tiles, axis=1)
        st = jnp.where(ok, jnp.dot(kband, qg, preferred_element_type=F32), NEG_INF)
        sk = jnp.zeros((1, grp * tq), F32)
        for a in range(grp):
            sk = jnp.where(lane_head == a, sink_ref[g * grp + a] * LOG2E, sk)
        m = jnp.maximum(jnp.max(st, axis=0, keepdims=True), sk)
        pt = jnp.exp2(st - m).astype(BF16)
        rows = slice(g * LANES, (g + 1) * LANES)
        vband = jnp.concatenate([vp_ref[rows, :], vc_ref[rows, :], vn_ref[rows, :]], axis=1)
        ot = jnp.dot(vband, pt, preferred_element_type=F32)
        o = ot[:dh] * (1.0 / (ot[dh:dh + 1] + jnp.exp2(sk - m)))
        heads.extend(o[:, a * tq:(a + 1) * tq] for a in range(grp))
    for pair in range(A_HEADS // 2):
        o_ref[:, pair * LANES:(pair + 1) * LANES] = jnp.concatenate(
            [heads[2 * pair], heads[2 * pair + 1]], axis=0).T.astype(BF16)


def _win_attn(p, qt, vt, sink, batch):
    t = p.shape[0]
    s_len = t // batch
    tq = TQ_WIN
    nq, nb, r = s_len // tq, s_len // BLK, tq // BLK
    p3 = p.reshape(batch, s_len, P_WIDTH)
    kcol = P_KA // (2 * LANES)
    lo = lambda n: jnp.maximum(n * r - 1, 0)
    hi = lambda n: jnp.minimum((n + 1) * r, nb - 1)
    kedge = lambda f: pl.BlockSpec((None, BLK, 2 * LANES), lambda b, n, s: (b, f(n), kcol))
    vedge = lambda f: pl.BlockSpec((A_KV_HEADS * LANES, BLK), lambda b, n, s: (0, b * nb + f(n)))
    out = pl.pallas_call(
        functools.partial(_win_attn_kernel, s_len=s_len),
        grid_spec=pltpu.PrefetchScalarGridSpec(
            num_scalar_prefetch=1,
            grid=(batch, nq),
            in_specs=[pl.BlockSpec((A_HEADS * A_HEAD_DIM, tq), lambda b, n, s: (0, b * nq + n)),
                      kedge(lo), pl.BlockSpec((None, tq, 2 * LANES), lambda b, n, s: (b, n, kcol)), kedge(hi),
                      vedge(lo), pl.BlockSpec((A_KV_HEADS * LANES, tq), lambda b, n, s: (0, b * nq + n)), vedge(hi)],
            out_specs=pl.BlockSpec((None, tq, A_HEADS * A_HEAD_DIM), lambda b, n, s: (b, n, 0)),
        ),
        out_shape=jax.ShapeDtypeStruct((batch, s_len, A_HEADS * A_HEAD_DIM), BF16),
        name="win_attn",
    )(sink, qt, p3, p3, p3, vt, vt, vt)
    return out.reshape(t, A_HEADS * A_HEAD_DIM)


def _mem_kv_kernel(mem_ref, g_ref, w_ref, o_ref):
    mem_n = _rms(mem_ref[...], g_ref[...]).astype(BF16)
    o_ref[...] = jnp.dot(mem_n, w_ref[...], preferred_element_type=F32).astype(BF16)


def _mem_kv(mem2, g_mem, w_mem_kv):
    depth, d, width = w_mem_kv.shape
    rows = mem2.shape[0]
    return pl.pallas_call(
        _mem_kv_kernel,
        grid=(depth,),
        in_specs=[pl.BlockSpec((rows, d), lambda l: (0, 0)),
                  pl.BlockSpec((1, d), lambda l: (0, 0)),
                  pl.BlockSpec((None, d, width), lambda l: (l, 0, 0))],
        out_specs=pl.BlockSpec((None, rows, width), lambda l: (l, 0, 0)),
        out_shape=jax.ShapeDtypeStruct((depth, rows, width), BF16),
        name="mem_kv",
    )(mem2, g_mem, w_mem_kv)


def _mem_attn_kernel(q_ref, kv_ref, o_ref):
    dh = M_HEAD_DIM
    for hd in range(M_HEADS):
        q = q_ref[:, hd * dh:(hd + 1) * dh]
        k = kv_ref[:, hd * dh:(hd + 1) * dh]
        v = kv_ref[:, (M_HEADS + hd) * dh:(M_HEADS + hd + 1) * dh]
        sc = lax.dot_general(q, k, (((1,), (1,)), ((), ())), preferred_element_type=F32)
        e = jnp.exp(sc - jnp.max(sc, axis=-1, keepdims=True))
        pr = (e * (1.0 / jnp.sum(e, axis=-1, keepdims=True))).astype(BF16)
        o_ref[:, hd * dh:(hd + 1) * dh] = jnp.dot(pr, v, preferred_element_type=F32).astype(BF16)


def _mem_attn(p, mem_kv, layer, batch):
    t = p.shape[0]
    s_len = t // batch
    tm = TM_MEM
    n_mem = mem_kv.shape[1] // batch
    width = M_HEADS * M_HEAD_DIM
    p3 = p.reshape(batch, s_len, P_WIDTH)
    out = pl.pallas_call(
        _mem_attn_kernel,
        grid=(batch, s_len // tm),
        in_specs=[pl.BlockSpec((None, tm, width), lambda b, i: (b, i, P_QM // width)),
                  pl.BlockSpec((None, n_mem, 2 * width), lambda b, i: (layer, b, 0))],
        out_specs=pl.BlockSpec((None, tm, width), lambda b, i: (b, i, 0)),
        out_shape=jax.ShapeDtypeStruct((batch, s_len, width), BF16),
        name="mem_attn",
    )(p3, mem_kv)
    return out.reshape(t, width)


def _mix_kernel(x_ref, ya_ref, yb_ref, ym_ref, g_ref, wb_ref, wo_ref, o_ref):
    d = x_ref.shape[1]
    mixed = None
    for i, y_ref in enumerate((ya_ref, yb_ref, ym_ref)):
        gate = jax.nn.sigmoid(g_ref[:, i * d:(i + 1) * d].astype(F32))
        term = gate * jnp.dot(y_ref[...], wb_ref[i], preferred_element_type=F32)
        mixed = term if mixed is None else mixed + term
    o_ref[...] = x_ref[...] + jnp.dot(mixed.astype(BF16), wo_ref[...], preferred_element_type=F32)


def _mix(x, ya, yb, ym, p, wb, wo, layer):
    t, d = x.shape
    tm = TM_MIX
    bw = ya.shape[1]
    row = lambda i: (i, 0)
    return pl.pallas_call(
        _mix_kernel,
        grid=(t // tm,),
        in_specs=[pl.BlockSpec((tm, d), row),
                  pl.BlockSpec((tm, bw), row), pl.BlockSpec((tm, bw), row), pl.BlockSpec((tm, bw), row),
                  pl.BlockSpec((tm, 3 * d), lambda i: (i, P_GATES)),
                  pl.BlockSpec((None, 3, bw, d), lambda i: (layer, 0, 0, 0)),
                  pl.BlockSpec((None, d, d), lambda i: (layer, 0, 0))],
        out_specs=pl.BlockSpec((tm, d), row),
        out_shape=jax.ShapeDtypeStruct((t, d), F32),
        name="mix_out",
    )(x, ya, yb, ym, p, wb, wo)


def _swiglu_step(h, wg, wu, wd):
    gate = jnp.dot(h, wg, preferred_element_type=F32)
    up = jnp.dot(h, wu, preferred_element_type=F32)
    act = (gate * jax.nn.sigmoid(gate) * up).astype(BF16)
    return jnp.dot(act, wd, preferred_element_type=F32)


def _ffn_kernel(x_ref, g_ref, wg_ref, wu_ref, wd_ref, o_ref, h_sc, acc_sc):
    j = pl.program_id(1)

    @pl.when(j == 0)
    def _():
        h_sc[...] = _rms(x_ref[...], g_ref[...]).astype(BF16)
        acc_sc[...] = jnp.zeros(acc_sc.shape, F32)

    acc_sc[...] += _swiglu_step(h_sc[...], wg_ref[...], wu_ref[...], wd_ref[...])

    @pl.when(j == pl.num_programs(1) - 1)
    def _():
        o_ref[...] = x_ref[...] + acc_sc[...]


def _ffn(x, g, wg, wu, wd, layer):
    t, d = x.shape
    f = wg.shape[2]
    tm, tf = TM_FFN, TF_FFN
    return pl.pallas_call(
        _ffn_kernel,
        grid=(t // tm, f // tf),
        in_specs=[pl.BlockSpec((tm, d), lambda i, j: (i, 0)),
                  pl.BlockSpec((1, d), lambda i, j: (0, 0)),
                  pl.BlockSpec((None, d, tf), lambda i, j: (layer, 0, j)),
                  pl.BlockSpec((None, d, tf), lambda i, j: (layer, 0, j)),
                  pl.BlockSpec((None, tf, d), lambda i, j: (layer, j, 0))],
        out_specs=pl.BlockSpec((tm, d), lambda i, j: (i, 0)),
        out_shape=jax.ShapeDtypeStruct((t, d), F32),
        scratch_shapes=[pltpu.VMEM((tm, d), BF16), pltpu.VMEM((tm, d), F32)],
        name="ffn_dense",
    )(x, g, wg, wu, wd)


def _router_kernel(x_ref, g_ref, wr_ref, idx_ref, wgt_ref, xt_ref):
    x = x_ref[...]
    xt_ref[...] = x.reshape(xt_ref.shape)
    h = _rms(x, g_ref[...])
    logits = lax.dot_general(wr_ref[...], h, (((1,), (1,)), ((), ())),
                             precision=lax.Precision.HIGHEST, preferred_element_type=F32)
    e_id = lax.broadcasted_iota(jnp.int32, logits.shape, 0)
    m1 = jnp.max(logits, axis=0, keepdims=True)
    i1 = jnp.min(jnp.where(logits == m1, e_id, N_EXPERTS), axis=0, keepdims=True)
    rest = jnp.where(e_id == i1, -jnp.inf, logits)
    m2 = jnp.max(rest, axis=0, keepdims=True)
    i2 = jnp.min(jnp.where(rest == m2, e_id, N_EXPERTS), axis=0, keepdims=True)
    e2 = jnp.exp(m2 - m1)
    w1 = 1.0 / (1.0 + e2)
    idx_ref[...] = jnp.concatenate([i1, i2], axis=0)
    wgt_ref[...] = jnp.concatenate([w1, e2 * w1], axis=0)


def _router(x, g, wr_t):
    t, d = x.shape
    tm = TM_ROUTE
    return pl.pallas_call(
        _router_kernel,
        grid=(t // tm,),
        in_specs=[pl.BlockSpec((tm, d), lambda i: (i, 0)),
                  pl.BlockSpec((1, d), lambda i: (0, 0)),
                  pl.BlockSpec((N_EXPERTS, d), lambda i: (0, 0))],
        out_specs=[pl.BlockSpec((TOP_K, tm), lambda i: (0, i)),
                   pl.BlockSpec((TOP_K, tm), lambda i: (0, i)),
                   pl.BlockSpec((tm * SUBLANES, LANES), lambda i: (i, 0))],
        out_shape=[jax.ShapeDtypeStruct((TOP_K, t), jnp.int32),
                   jax.ShapeDtypeStruct((TOP_K, t), F32),
                   jax.ShapeDtypeStruct((t * SUBLANES, LANES), F32)],
        name="moe_router",
    )(x, g, wr_t)


def _moe_rows_per_step(n_steps):
    return -(-TM_EXP // (n_steps * SUBLANES)) * SUBLANES


def _slot_tables(top_i, n_tok, buf_rows):
    n_asg = n_tok * TOP_K
    flat_e = top_i.T.reshape(n_asg)
    order = jnp.argsort(flat_e, stable=True).astype(jnp.int32)
    counts = jnp.sum(flat_e[:, None] == jnp.arange(N_EXPERTS, dtype=jnp.int32)[None, :], axis=0, dtype=jnp.int32)
    padded = (counts + TM_EXP - 1) // TM_EXP * TM_EXP
    pad_end = jnp.cumsum(padded)
    start = jnp.cumsum(counts) - counts
    pads_before = jnp.cumsum(padded - counts) - (padded - counts)
    n_slots = n_asg + N_EXPERTS * TM_EXP
    n_blk = n_slots // TM_EXP
    extra = buf_rows - TM_EXP
    blk_e = jnp.minimum(jnp.searchsorted(pad_end, jnp.arange(n_blk, dtype=jnp.int32) * TM_EXP, side='right'),
                        N_EXPERTS - 1).astype(jnp.int32)
    slot_e = jnp.repeat(blk_e, TM_EXP)
    rank = jnp.arange(n_slots, dtype=jnp.int32) - (pad_end - padded)[slot_e]
    is_pad = rank >= counts[slot_e]
    asg = order[jnp.clip(start[slot_e] + rank, 0, n_asg - 1)]
    slot_tok = jnp.where(is_pad, 0, asg // TOP_K)
    slot_row = jnp.where(is_pad, n_asg + pads_before[slot_e] + rank - counts[slot_e],
                         (asg % TOP_K) * n_tok + slot_tok)
    gather_tok = jnp.concatenate([slot_tok.reshape(n_blk, TM_EXP), jnp.zeros((n_blk, extra), jnp.int32)], axis=1)
    spare = n_slots + jnp.arange((n_blk + 1) * buf_rows - n_slots, dtype=jnp.int32)
    lead, over = spare[:TM_EXP], spare[TM_EXP:].reshape(n_blk + 1, extra)
    scatter_row = jnp.concatenate(
        [jnp.concatenate([lead[None], slot_row.reshape(n_blk, TM_EXP)], axis=0), over], axis=1)
    n_live = (pad_end[-1] // TM_EXP).astype(jnp.int32)
    blk_meta = jnp.concatenate([blk_e, n_live[None]])
    return blk_meta, gather_tok.reshape(-1).astype(jnp.int32), scatter_row.reshape(-1).astype(jnp.int32)


def _moe_ffn_kernel(blk_e_ref, gtok_ref, srow_ref, x_hbm, g_ref, wg_ref, wu_ref, wd_ref, o_hbm,
                    xg_sc, ys_sc, h_sc, acc_sc, gsem, ssem, *, rows_per_step):
    i, j = pl.program_id(0), pl.program_id(1)
    n_blk, n_steps = pl.num_programs(0), pl.num_programs(1)
    rows, d = h_sc.shape
    tr = d // LANES
    buf_rows = xg_sc.shape[1] // tr
    slot = i % 2
    other = 1 - slot

    def tile(ref, k):
        return ref.at[pl.ds(pl.multiple_of(k * tr, tr), tr), :]

    def gathered(s):
        return pltpu.make_async_copy(x_hbm.at[pl.ds(0, buf_rows * tr), :], xg_sc.at[s], gsem.at[s])

    def scattered(s):
        return pltpu.make_async_copy(ys_sc.at[s], o_hbm.at[pl.ds(0, buf_rows * tr), :], ssem.at[s])

    def get_row(s, k, entry):
        return pltpu.make_async_copy(tile(x_hbm, gtok_ref[entry]), tile(xg_sc.at[s], k), gsem.at[s])

    def put_row(s, k, entry):
        return pltpu.make_async_copy(tile(ys_sc.at[s], k), tile(o_hbm, srow_ref[entry]), ssem.at[s])

    @pl.when((i == 0) & (j == 0))
    def _():
        ys_sc[...] = jnp.zeros(ys_sc.shape, F32)

        def start(k, c):
            get_row(0, k, k).start()
            return c

        lax.fori_loop(0, buf_rows, start, 0)

    @pl.when(j == 0)
    def _():
        gathered(slot).wait()
        for s in range(2):
            @pl.when(slot == s)
            def _():
                xs = [xg_sc[s, pl.ds(c, rows, stride=tr), :] for c in range(tr)]
                ms = sum(jnp.sum(x * x, axis=-1, keepdims=True) for x in xs) * (1.0 / d)
                inv = lax.rsqrt(ms + EPS)
                for c, x in enumerate(xs):
                    cols = slice(c * LANES, (c + 1) * LANES)
                    h_sc[:, cols] = (x * inv * g_ref[:, cols]).astype(BF16)

        acc_sc[...] = jnp.zeros(acc_sc.shape, F32)

    nxt = jnp.minimum(i + 1, n_blk - 1)

    def issue_row_copies():
        for r in range(rows_per_step):
            k = j * rows_per_step + r
            get_row(other, k, nxt * buf_rows + k).start(priority=r % DMA_QUEUES)
            put_row(other, k, i * buf_rows + k).start(priority=(r + 1) % DMA_QUEUES)

    live = i < blk_e_ref[n_blk]

    @pl.when(live)
    def _():
        issue_row_copies()
        acc_sc[...] += _swiglu_step(h_sc[...], wg_ref[...].astype(BF16), wu_ref[...].astype(BF16),
                                    wd_ref[...].astype(BF16))

    @pl.when(jnp.logical_not(live))
    def _():
        issue_row_copies()

    @pl.when(j == n_steps - 1)
    def _():
        @pl.when(i > 0)
        def _():
            scattered(slot).wait()

        for s in range(2):
            @pl.when(slot == s)
            def _():
                for c in range(tr):
                    ys_sc[s, pl.ds(c, rows, stride=tr), :] = acc_sc[:, c * LANES:(c + 1) * LANES]

    @pl.when((i == n_blk - 1) & (j == n_steps - 1))
    def _():
        def start(k, c):
            put_row(slot, k, n_blk * buf_rows + k).start()
            return c

        lax.fori_loop(0, buf_rows, start, 0)
        scattered(other).wait()
        scattered(slot).wait()
        gathered(other).wait()


def _moe_ffn(x_tiles, g, wg, wu, wd, layer, blk_e, gather_tok, scatter_row):
    d, f = wg.shape[2], wg.shape[3]
    tr = d // LANES
    assert tr % SUBLANES == 0
    tm, tf = TM_EXP, TF_FFN
    n_steps = f // tf
    rows_per_step = _moe_rows_per_step(n_steps)
    buf_rows = rows_per_step * n_steps
    n_blk = gather_tok.shape[0] // buf_rows

    def ff(i, j, be):
        return jnp.where(i < be[n_blk], j, n_steps - 1)

    return pl.pallas_call(
        functools.partial(_moe_ffn_kernel, rows_per_step=rows_per_step),
        grid_spec=pltpu.PrefetchScalarGridSpec(
            num_scalar_prefetch=3,
            grid=(n_blk, n_steps),
            in_specs=[pl.BlockSpec(memory_space=pl.ANY),
                      pl.BlockSpec((1, d), lambda i, j, be, gt, sr: (0, 0)),
                      pl.BlockSpec((None, None, d, tf), lambda i, j, be, gt, sr: (layer, be[i], 0, ff(i, j, be))),
                      pl.BlockSpec((None, None, d, tf), lambda i, j, be, gt, sr: (layer, be[i], 0, ff(i, j, be))),
                      pl.BlockSpec((None, None, tf, d), lambda i, j, be, gt, sr: (layer, be[i], ff(i, j, be), 0))],
            out_specs=pl.BlockSpec(memory_space=pl.ANY),
            scratch_shapes=[pltpu.VMEM((2, buf_rows * tr, LANES), F32), pltpu.VMEM((2, buf_rows * tr, LANES), F32),
                            pltpu.VMEM((tm, d), BF16), pltpu.VMEM((tm, d), F32),
                            pltpu.SemaphoreType.DMA((2,)), pltpu.SemaphoreType.DMA((2,))],
        ),
        out_shape=jax.ShapeDtypeStruct((scatter_row.shape[0] * tr, LANES), F32),
        name="moe_ffn",
    )(blk_e, gather_tok, scatter_row, x_tiles, g, wg, wu, wd)


def _moe_combine_kernel(x_ref, w_ref, y0_ref, y1_ref, gf_ref, o_ref, *, final_norm):
    tm, d = x_ref.shape
    y0, y1 = y0_ref[...].reshape(tm, d), y1_ref[...].reshape(tm, d)
    out = x_ref[...] + w_ref[:, 0:1] * y0 + w_ref[:, 1:2] * y1
    if final_norm:
        out = _rms(out, gf_ref[...])
    o_ref[...] = out


def _moe_combine(x, top_w, y, g_final, final_norm):
    t, d = x.shape
    tr = d // LANES
    tm = TM_COMB
    nt = t // tm
    return pl.pallas_call(
        functools.partial(_moe_combine_kernel, final_norm=final_norm),
        grid=(nt,),
        in_specs=[pl.BlockSpec((tm, d), lambda i: (i, 0)),
                  pl.BlockSpec((tm, TOP_K), lambda i: (i, 0)),
                  pl.BlockSpec((tm * tr, LANES), lambda i: (i, 0)),
                  pl.BlockSpec((tm * tr, LANES), lambda i: (nt + i, 0)),
                  pl.BlockSpec((1, d), lambda i: (0, 0))],
        out_specs=pl.BlockSpec((tm, d), lambda i: (i, 0)),
        out_shape=jax.ShapeDtypeStruct((t, d), F32),
        name="moe_combine",
    )(x, top_w, y, y, g_final)


def _final_norm_kernel(x_ref, g_ref, o_ref):
    o_ref[...] = _rms(x_ref[...], g_ref[...])


def _final_norm(x, g):
    t, d = x.shape
    tm = 1024
    return pl.pallas_call(
        _final_norm_kernel,
        grid=(t // tm,),
        in_specs=[pl.BlockSpec((tm, d), lambda i: (i, 0)), pl.BlockSpec((1, d), lambda i: (0, 0))],
        out_specs=pl.BlockSpec((tm, d), lambda i: (i, 0)),
        out_shape=jax.ShapeDtypeStruct((t, d), F32),
        name="final_norm",
    )(x, g)


def _w_in_relayout_kernel(w_ref, tok_ref, qt_ref, vt_ref):
    w = w_ref[...]
    rows = w.shape[0]
    qa, ka, va = w[:, 0:512], w[:, 512:640], w[:, 640:768]
    cq, ckv, kr = w[:, 768:1024], w[:, 1024:1152], w[:, 1152:1184]
    qm, gates = w[:, 1184:1696], w[:, 1696:4768]
    pad = jnp.zeros((rows, LANES - B_ROPE), w.dtype)
    tok_ref[...] = jnp.concatenate([gates, qm, ka, kr, pad, cq, ckv], axis=1).astype(BF16)
    qt_ref[...] = qa.T.astype(BF16)
    vt_ref[...] = va.T.astype(BF16)


def _relayout_w_in(w_in):
    depth, d, width = w_in.shape
    tr = 256
    qrows, vrows = A_HEADS * A_HEAD_DIM, A_KV_HEADS * A_HEAD_DIM
    return pl.pallas_call(
        _w_in_relayout_kernel,
        grid=(depth, d // tr),
        in_specs=[pl.BlockSpec((None, tr, width), lambda l, i: (l, i, 0))],
        out_specs=[pl.BlockSpec((None, tr, P_WIDTH), lambda l, i: (l, i, 0)),
                   pl.BlockSpec((None, qrows, tr), lambda l, i: (l, 0, i)),
                   pl.BlockSpec((None, vrows, tr), lambda l, i: (l, 0, i))],
        out_shape=[jax.ShapeDtypeStruct((depth, d, P_WIDTH), BF16),
                   jax.ShapeDtypeStruct((depth, qrows, d), BF16),
                   jax.ShapeDtypeStruct((depth, vrows, d), BF16)],
        name="w_in_relayout",
    )(w_in)


def _relayout_latent(w_uq, w_ukv):
    depth = w_uq.shape[0]
    q = w_uq.reshape(depth, B_Q_RANK, B_HEADS, B_NOPE + B_ROPE)
    q = jnp.pad(q, ((0, 0), (0, 0), (0, 0), (0, LANES - B_NOPE - B_ROPE)))
    kv = w_ukv.reshape(depth, B_KV_RANK, B_HEADS, B_NOPE + B_V)
    k = jnp.pad(kv[..., :B_NOPE], ((0, 0), (0, 0), (0, 0), (0, LANES - B_NOPE)))
    v = jnp.pad(kv[..., B_NOPE:], ((0, 0), (0, 0), (0, 0), (0, B_VROWS - B_V)))
    flat = lambda a, r: a.reshape(depth, r, -1).astype(BF16)
    return (jnp.swapaxes(flat(q, B_Q_RANK), 1, 2), flat(k, B_KV_RANK),
            jnp.swapaxes(flat(v, B_KV_RANK), 1, 2))


def kernel(x, mem, positions, g_mix, w_in, g_q_lat, w_uq, g_kv_lat, w_ukv, sink, w_mem_kv, w_branch, w_out,
           g_ffn, w_gate_dense, w_up_dense, w_down_dense, w_router, w_gate_exp, w_up_exp, w_down_exp,
           g_mem, g_final):
    batch, s_len, d = x.shape
    depth = w_in.shape[0]
    t = batch * s_len
    row = lambda a: a.reshape(1, -1).astype(F32)

    w_in_p, wqa_t, wva_t = _relayout_w_in(w_in)
    wq_p, wk_p, wv_p = _relayout_latent(w_uq, w_ukv)
    bf = lambda w: w.astype(BF16)
    w_branch_b, w_out_b = bf(w_branch), bf(w_out)
    wgd, wud, wdd = bf(w_gate_dense), bf(w_up_dense), bf(w_down_dense)
    wge, wue, wde = w_gate_exp, w_up_exp, w_down_exp
    wr_t = jnp.swapaxes(w_router, 1, 2).astype(F32)

    cos_a, sin_a, cos_b, sin_b, cos_t, sin_t = _rope_tables(positions)
    mem_kv = _mem_kv(mem.reshape(batch * mem.shape[1], d).astype(F32), row(g_mem), bf(w_mem_kv))

    xt = x.reshape(t, d).astype(F32)
    for l in range(depth):
        p, qa_t, va_t = _in_proj(xt, row(g_mix[l]), w_in_p, wqa_t, wva_t, l, cos_a, sin_a, cos_t, sin_t)
        y_a = _win_attn(p, qa_t, va_t, sink[l].astype(F32), batch)
        qt, k, vt = _mla_prep(p, row(g_q_lat[l]), row(g_kv_lat[l]), wq_p[l], wk_p[l], wv_p[l],
                              cos_b, sin_b, cos_t, sin_t, batch)
        y_b = _mla_attn(qt, k, vt)
        y_m = _mem_attn(p, mem_kv, l, batch)
        xt = _mix(xt, y_a, y_b, y_m, p, w_branch_b, w_out_b, l)
        i = l // 2
        if l % 2 == 0:
            xt = _ffn(xt, row(g_ffn[l]), wgd, wud, wdd, i)
            if l == depth - 1:
                xt = _final_norm(xt, row(g_final))
        else:
            top_i, top_w, x_tiles = _router(xt, row(g_ffn[l]), wr_t[i])
            n_steps = w_gate_exp.shape[3] // TF_FFN
            buf_rows = _moe_rows_per_step(n_steps) * n_steps
            blk_e, gather_tok, scatter_row = _slot_tables(top_i, t, buf_rows)
            y = _moe_ffn(x_tiles, row(g_ffn[l]), wge, wue, wde, i, blk_e, gather_tok, scatter_row)
            xt = _moe_combine(xt, top_w.T, y, row(g_final), final_norm=(l == depth - 1))
    return xt.reshape(batch, s_len, d).astype(x.dtype)
```
